```python
import jax, jax.numpy as jnp
from jax import lax
import numpy as np

D_MODEL = 1024
BATCH = 8
SEQ = 2048
DEPTH = 2
DEC_BATCH = 8
DEC_SEQ = 16
PAST_LEN = 2048

CHUNK = 64
HEAD_DIM = 64
N_HEADS = D_MODEL // HEAD_DIM
N_HEADS_SB = N_HEADS // 2
N_HEADS_CB = N_HEADS - N_HEADS_SB
W_SB = N_HEADS_SB * HEAD_DIM
W_CB = N_HEADS_CB * HEAD_DIM
MIX_WIDTH = W_SB + W_CB
BAND_CHUNKS = 8
BAND = BAND_CHUNKS * CHUNK
REL_MAX = 128
D_FF = -(-8 * D_MODEL // (3 * 256)) * 256
PLE_DIM = 256
QBLOCK = 128
EPS = 1e-6

kernel_name = "hymba_stickbreak_chunkband_stream_step"


def rmsnorm(x, g):
    xf = x.astype(jnp.float32)
    y = xf * lax.rsqrt(jnp.mean(xf * xf, axis=-1, keepdims=True) + EPS)
    return y.astype(x.dtype) * g


def split_heads(h, w_in):
    qkv = h @ w_in
    cuts = [W_SB, 2 * W_SB, 3 * W_SB, 3 * W_SB + W_CB, 3 * W_SB + 2 * W_CB]
    parts = jnp.split(qkv, cuts, axis=-1)
    b, t = h.shape[0], h.shape[1]
    heads = [N_HEADS_SB] * 3 + [N_HEADS_CB] * 3
    return [pt.reshape(b, t, n, HEAD_DIM) for pt, n in zip(parts, heads)]


def stick_breaking(q, k, v, q_pos, k_pos):
    z = jnp.einsum('bqhd,bkhd->bhqk', q, k).astype(jnp.float32) * (HEAD_DIM ** -0.5)
    causal = k_pos[None, :] < q_pos[:, None]
    log_keep = jnp.where(causal, jax.nn.log_sigmoid(-z), 0.0)
    between = lax.cumsum(log_keep, axis=3, reverse=True) - log_keep
    w = jnp.where(causal, jnp.exp(jax.nn.log_sigmoid(z) + between), 0.0)
    return jnp.einsum('bhqk,bkhd->bqhd', w.astype(v.dtype), v)


def band_attention(q, k, v, dist, rel_table, valid=None):
    idx = jnp.clip(dist, -REL_MAX, REL_MAX) + REL_MAX
    bias = jnp.transpose(rel_table[idx], (2, 0, 1)).astype(jnp.float32)
    s = jnp.einsum('...qhd,...khd->...hqk', q, k).astype(jnp.float32) * (HEAD_DIM ** -0.5) + bias
    if valid is not None:
        s = jnp.where(valid, s, -jnp.inf)
    p = jax.nn.softmax(s, axis=-1)
    return jnp.einsum('...hqk,...khd->...qhd', p.astype(v.dtype), v)


def chunk_band_prompt(q, k, v, rel_table):
    b, s, h, d = q.shape
    nc = s // CHUNK
    kb_len = (BAND_CHUNKS + 1) * CHUNK
    qc = q.reshape(b, nc, CHUNK, h, d)

    def band(x):
        xc = x.reshape(b, nc, CHUNK, h, d)
        xp = jnp.pad(xc, ((0, 0), (BAND_CHUNKS, 0), (0, 0), (0, 0), (0, 0)))
        return jnp.concatenate([xp[:, i:i + nc] for i in range(BAND_CHUNKS + 1)], axis=2)

    kb, vb = band(k), band(v)
    dist = BAND + jnp.arange(CHUNK)[:, None] - jnp.arange(kb_len)[None, :]
    key_pos = (jnp.arange(nc)[:, None] - BAND_CHUNKS) * CHUNK + jnp.arange(kb_len)[None, :]
    valid = (key_pos >= 0)[:, None, None, :]
    out = band_attention(qc, kb, vb, dist, rel_table, valid)
    return out.reshape(b, s, h, d)


def trunk_layer(x, p, g_mix, w_in, rel_table, g_out_sb, g_out_cb, w_out, g_ffn, w_gate, w_up, w_down,
                g_ple, w_ple_gate, w_ple_proj, past_sb_k=None, past_sb_v=None, past_cb_k=None, past_cb_v=None):
    b, t = x.shape[0], x.shape[1]
    h = rmsnorm(x, g_mix)
    q_a, k_a, v_a, q_b, k_b, v_b = split_heads(h, w_in)
    if past_sb_k is None:
        nb = t // QBLOCK
        k_pos = jnp.arange(t)
        qb = jnp.moveaxis(q_a.reshape(b, nb, QBLOCK, N_HEADS_SB, HEAD_DIM), 1, 0)
        a = lax.map(lambda args: stick_breaking(args[0], k_a, v_a, args[1] * QBLOCK + jnp.arange(QBLOCK), k_pos),
                    (qb, jnp.arange(nb)))
        a = jnp.moveaxis(a, 0, 1).reshape(b, t, N_HEADS_SB, HEAD_DIM)
        o_b = chunk_band_prompt(q_b, k_b, v_b, rel_table)
        new = (k_a, v_a, k_b[:, -BAND:], v_b[:, -BAND:])
    else:
        past = past_sb_k.shape[1]
        q_pos = past + jnp.arange(t)
        ka = jnp.concatenate([past_sb_k, k_a.astype(past_sb_k.dtype)], axis=1)
        va = jnp.concatenate([past_sb_v, v_a.astype(past_sb_v.dtype)], axis=1)
        a = stick_breaking(q_a, ka, va, q_pos, jnp.arange(past + t))
        lb = past_cb_k.shape[1]
        kc = jnp.concatenate([past_cb_k, k_b.astype(past_cb_k.dtype)], axis=1)
        vc = jnp.concatenate([past_cb_v, v_b.astype(past_cb_v.dtype)], axis=1)
        kc_pos = jnp.concatenate([past - lb + jnp.arange(lb), q_pos])
        o_b = band_attention(q_b, kc, vc, q_pos[:, None] - kc_pos[None, :], rel_table)
        new = (k_a, v_a, k_b, v_b)
    a = rmsnorm(a, g_out_sb.reshape(N_HEADS_SB, HEAD_DIM))
    o_b = rmsnorm(o_b, g_out_cb.reshape(N_HEADS_CB, HEAD_DIM))
    mix = jnp.concatenate([a.reshape(b, t, W_SB), o_b.reshape(b, t, W_CB)], axis=-1)
    x = x + mix @ w_out
    h = rmsnorm(x, g_ffn)
    x = x + (jax.nn.silu(h @ w_gate) * (h @ w_up)) @ w_down
    gate = jax.nn.sigmoid(rmsnorm(x, g_ple) @ w_ple_gate)
    x = x + gate * (p @ w_ple_proj)
    return x, new


def setup_inputs(seed: int = 0) -> dict:
    key = jax.random.key(seed)
    ks = jax.random.split(key, 24)
    n = jax.random.normal
    f = jnp.float32
    lb = min(BAND, PAST_LEN)
    return {
        "x_prompt": n(ks[0], (BATCH, SEQ, D_MODEL), f),
        "x_sample": n(ks[1], (DEC_BATCH, DEC_SEQ, D_MODEL), f),
        "p_prompt": n(ks[2], (DEPTH, BATCH, SEQ, PLE_DIM), f),
        "p_sample": n(ks[3], (DEPTH, DEC_BATCH, DEC_SEQ, PLE_DIM), f),
        "cache_sb_k": n(ks[4], (DEPTH, DEC_BATCH, PAST_LEN, N_HEADS_SB, HEAD_DIM), f),
        "cache_sb_v": n(ks[5], (DEPTH, DEC_BATCH, PAST_LEN, N_HEADS_SB, HEAD_DIM), f),
        "cache_cb_k": n(ks[6], (DEPTH, DEC_BATCH, lb, N_HEADS_CB, HEAD_DIM), f),
        "cache_cb_v": n(ks[7], (DEPTH, DEC_BATCH, lb, N_HEADS_CB, HEAD_DIM), f),
        "g_mix": 1.0 + 0.05 * n(ks[8], (DEPTH, D_MODEL), f),
        "w_in": n(ks[9], (DEPTH, D_MODEL, 3 * MIX_WIDTH), f) * D_MODEL ** -0.5,
        "rel_table": 0.5 * n(ks[10], (DEPTH, 2 * REL_MAX + 1, N_HEADS_CB), f),
        "g_out_sb": 1.0 + 0.05 * n(ks[11], (DEPTH, W_SB), f),
        "g_out_cb": 1.0 + 0.05 * n(ks[12], (DEPTH, W_CB), f),
        "w_out": n(ks[13], (DEPTH, MIX_WIDTH, D_MODEL), f) * MIX_WIDTH ** -0.5,
        "g_ffn": 1.0 + 0.05 * n(ks[14], (DEPTH, D_MODEL), f),
        "w_gate": n(ks[15], (DEPTH, D_MODEL, D_FF), f) * D_MODEL ** -0.5,
        "w_up": n(ks[16], (DEPTH, D_MODEL, D_FF), f) * D_MODEL ** -0.5,
        "w_down": n(ks[17], (DEPTH, D_FF, D_MODEL), f) * D_FF ** -0.5,
        "g_ple": 1.0 + 0.05 * n(ks[18], (DEPTH, D_MODEL), f),
        "w_ple_gate": n(ks[19], (DEPTH, D_MODEL, D_MODEL), f) * D_MODEL ** -0.5,
        "w_ple_proj": n(ks[20], (DEPTH, PLE_DIM, D_MODEL), f) * PLE_DIM ** -0.5,
        "g_final": 1.0 + 0.05 * n(ks[21], (D_MODEL,), f),
    }


def reference(x_prompt, x_sample, p_prompt, p_sample, cache_sb_k, cache_sb_v, cache_cb_k, cache_cb_v,
              g_mix, w_in, rel_table, g_out_sb, g_out_cb, w_out, g_ffn, w_gate, w_up, w_down,
              g_ple, w_ple_gate, w_ple_proj, g_final):
    xp, xs = x_prompt, x_sample
    new_p, new_s = [], []
    for i in range(DEPTH):
        w = (g_mix[i], w_in[i], rel_table[i], g_out_sb[i], g_out_cb[i], w_out[i], g_ffn[i],
             w_gate[i], w_up[i], w_down[i], g_ple[i], w_ple_gate[i], w_ple_proj[i])
        xp, np_i = trunk_layer(xp, p_prompt[i], *w)
        xs, ns_i = trunk_layer(xs, p_sample[i], *w, cache_sb_k[i], cache_sb_v[i], cache_cb_k[i], cache_cb_v[i])
        new_p.append(np_i)
        new_s.append(ns_i)
    y_prompt = rmsnorm(xp, g_final)
    y_sample = rmsnorm(xs, g_final)
    return (y_prompt, y_sample,
            jnp.stack([e[0] for e in new_p]), jnp.stack([e[1] for e in new_p]),
            jnp.stack([e[2] for e in new_p]), jnp.stack([e[3] for e in new_p]),
            jnp.stack([e[0] for e in new_s]), jnp.stack([e[1] for e in new_s]),
            jnp.stack([e[2] for e in new_s]), jnp.stack([e[3] for e in new_s]))
```

```python
import functools

import jax
import jax.numpy as jnp
from jax import lax
from jax.experimental import pallas as pl
from jax.experimental.pallas import tpu as pltpu

F32 = jnp.float32
BF16 = jnp.bfloat16

HEAD_DIM = 64
CHUNK = 64
BAND_CHUNKS = 8
BAND = BAND_CHUNKS * CHUNK
REL_MAX = 128
EPS = 1e-6
Q_SCALE = HEAD_DIM ** -0.5

LANES = 128
NEG = -1e30

ROW_TILE = 256
SB_TQ = 256
SB_TK = 128
CB_TQ = 256
CB_TK = CB_TQ + BAND
VMEM_LIMIT = 56 * 1024 * 1024


def _params(*semantics):
    return pltpu.CompilerParams(dimension_semantics=semantics, vmem_limit_bytes=VMEM_LIMIT)


def _resident(shape):
    zeros = (0,) * len(shape)
    return pl.BlockSpec(shape, lambda *_: zeros, pipeline_mode=pl.Buffered(1))


def _rms_unit(x):
    return x * lax.rsqrt(jnp.mean(x * x, axis=-1, keepdims=True) + EPS)


def _dot(a, b):
    return jnp.dot(a, b, preferred_element_type=F32)


def _dot_nt(a, b):
    return lax.dot_general(a, b, (((1,), (1,)), ((), ())), preferred_element_type=F32)


def _softplus(z):
    return jnp.maximum(z, 0.0) + jnp.log1p(jnp.exp(-jnp.abs(z)))


def _split_bf16(x):
    hi = x.astype(BF16)
    lo = (x - hi.astype(F32)).astype(BF16)
    return hi, lo


def _head_pair_rms(o, g):
    first = lax.broadcasted_iota(jnp.int32, o.shape, 1) < HEAD_DIM
    o2 = o * o
    s0 = jnp.sum(jnp.where(first, o2, 0.0), axis=-1, keepdims=True)
    s1 = jnp.sum(jnp.where(first, 0.0, o2), axis=-1, keepdims=True)
    ms = jnp.where(first, s0, s1) * (1.0 / HEAD_DIM)
    return o * lax.rsqrt(ms + EPS) * g


def _store_qkv(h, w_in_ref, outs):
    qa_ref, ka_ref, va_ref, qb_ref, kb_ref, vb_ref = outs
    w = qa_ref.shape[-1]

    def proj(c):
        return _dot(h, w_in_ref[:, c * w:(c + 1) * w])

    qa_ref[...] = (proj(0) * Q_SCALE).astype(BF16)
    ka_ref[...] = proj(1)
    va_ref[...] = proj(2)
    qb_ref[...] = (proj(3) * Q_SCALE).astype(BF16)
    kb_ref[...] = proj(4)
    vb_ref[...] = proj(5)


def _qkv_kernel(x_ref, g_ref, w_in_ref, *outs):
    h = (_rms_unit(x_ref[...]) * g_ref[...]).astype(BF16)
    _store_qkv(h, w_in_ref, outs)


def _layer_tail(x_ref, a_ref, ob_ref, p_ref, wo_a_ref, wo_b_ref, gf_ref, wg_ref, wu_ref, wd_ref,
                gp_ref, wpg_ref, wpp_ref):
    x = x_ref[...] + _dot(a_ref[...], wo_a_ref[...]) + _dot(ob_ref[...], wo_b_ref[...])
    h = (_rms_unit(x) * gf_ref[...]).astype(BF16)
    gate = _dot(h, wg_ref[...])
    up = _dot(h, wu_ref[...])
    act = (gate * jax.nn.sigmoid(gate) * up).astype(BF16)
    x = x + _dot(act, wd_ref[...])
    hp = (_rms_unit(x) * gp_ref[...]).astype(BF16)
    ple_gate = jax.nn.sigmoid(_dot(hp, wpg_ref[...]))
    return x + ple_gate * _dot(p_ref[...].astype(BF16), wpp_ref[...])


def _mid_layer_kernel(*refs):
    tail, (gn_ref, w_in_ref, xo_ref), outs = refs[:13], refs[13:16], refs[16:]
    x = _layer_tail(*tail)
    xo_ref[...] = x
    _store_qkv((_rms_unit(x) * gn_ref[...]).astype(BF16), w_in_ref, outs)


def _last_layer_kernel(*refs):
    tail, (gn_ref, y_ref) = refs[:13], refs[13:]
    y_ref[...] = _rms_unit(_layer_tail(*tail)) * gn_ref[...]


def _row_spec(tm, width):
    return pl.BlockSpec((tm, width), lambda i: (i, 0))


def _qkv_out(rows, w, tm):
    shapes = [jax.ShapeDtypeStruct((rows, w), dt) for dt in (BF16, F32, F32, BF16, F32, F32)]
    return shapes, [_row_spec(tm, w) for _ in shapes]


def _qkv_call(x, g, w_in, tm):
    rows, d = x.shape
    w = w_in.shape[1] // 6
    out_shape, out_specs = _qkv_out(rows, w, tm)
    return pl.pallas_call(
        _qkv_kernel, grid=(rows // tm,),
        in_specs=[_row_spec(tm, d), _resident((1, d)), _resident(w_in.shape)],
        out_specs=out_specs, out_shape=out_shape,
        compiler_params=_params("parallel"), name="qkv_proj",
    )(x, g, w_in)


def _layer_call(x, a, ob, p, lw, g_next, w_in_next, tm):
    rows, d = x.shape
    tail_in = [x, a, ob, p, lw["wo_a"], lw["wo_b"], lw["g_ffn"], lw["w_gate"], lw["w_up"], lw["w_down"],
               lw["g_ple"], lw["w_ple_gate"], lw["w_ple_proj"], g_next]
    tail_specs = ([_row_spec(tm, d), _row_spec(tm, a.shape[1]), _row_spec(tm, ob.shape[1]), _row_spec(tm, p.shape[1])]
                  + [_resident(t.shape) for t in tail_in[4:]])
    if w_in_next is None:
        return pl.pallas_call(
            _last_layer_kernel, grid=(rows // tm,), in_specs=tail_specs,
            out_specs=_row_spec(tm, d), out_shape=jax.ShapeDtypeStruct((rows, d), F32),
            compiler_params=_params("parallel"), name="layer_tail_final",
        )(*tail_in)
    out_shape, out_specs = _qkv_out(rows, w_in_next.shape[1] // 6, tm)
    return pl.pallas_call(
        _mid_layer_kernel, grid=(rows // tm,), in_specs=tail_specs + [_resident(w_in_next.shape)],
        out_specs=[_row_spec(tm, d)] + out_specs,
        out_shape=[jax.ShapeDtypeStruct((rows, d), F32)] + out_shape,
        compiler_params=_params("parallel"), name="layer_tail_qkv",
    )(*tail_in, w_in_next)


def _suffix_matrix(tk):
    j = lax.broadcasted_iota(jnp.int32, (2 * tk, 2 * tk), 0) % tk
    c = lax.broadcasted_iota(jnp.int32, (2 * tk, 2 * tk), 1)
    return jnp.where((c >= tk) | (j > c), 1.0, 0.0).astype(BF16)


def _sb_block(q, kk, vv, suffix, carry_ref, acc_ref, causal):
    tk = suffix.shape[0] // 2
    heads = kk.shape[0] // tk
    z = _dot_nt(q, kk)
    sp = _softplus(z)
    w_parts = []
    for h in range(heads):
        cols = slice(h * tk, (h + 1) * tk)
        z_h, sp_h = z[:, cols], sp[:, cols]
        log_keep = -sp_h
        if causal is not None:
            log_keep = jnp.where(causal, log_keep, 0.0)
        hi, lo = _split_bf16(log_keep)
        sums = _dot(jnp.concatenate([hi, lo], axis=1), suffix)
        carry = carry_ref[:, cols]
        log_w = (z_h - sp_h) + (sums[:, :tk] + carry)
        w = jnp.exp(log_w)
        if causal is not None:
            w = jnp.where(causal, w, 0.0)
        carry_ref[:, cols] = carry + sums[:, tk:]
        w_parts.append(w.astype(BF16))
    acc_ref[...] += _dot(jnp.concatenate(w_parts, axis=1), vv)


def _pair_masked(x, tk):
    first = lax.broadcasted_iota(jnp.int32, x.shape, 1) < HEAD_DIM
    xb = x.astype(BF16)
    zero = jnp.zeros_like(xb)
    return jnp.concatenate([jnp.where(first, xb, zero), jnp.where(first, zero, xb)], axis=0)


def _sb_kernel(q_ref, k_ref, v_ref, suffix_ref, g_ref, o_ref, kk_ref, vv_ref, carry_ref, acc_ref):
    tq, tk = SB_TQ, SB_TK
    i = pl.program_id(2)

    @pl.when(i == 0)
    def _():
        def stage(j, _):
            rows = pl.ds(pl.multiple_of(j * tk, tk), tk)
            kk_ref[j] = _pair_masked(k_ref[0, rows, :], tk)
            vv_ref[j] = _pair_masked(v_ref[0, rows, :], tk)
            return 0
        lax.fori_loop(0, kk_ref.shape[0], stage, 0)

    carry_ref[...] = jnp.zeros_like(carry_ref)
    acc_ref[...] = jnp.zeros_like(acc_ref)
    q = q_ref[0]
    suffix = suffix_ref[...]
    q_pos = i * tq + lax.broadcasted_iota(jnp.int32, (tq, tk), 0)
    k_off = lax.broadcasted_iota(jnp.int32, (tq, tk), 1)
    first_diag = i * (tq // tk)

    for d in reversed(range(tq // tk)):
        j = first_diag + d
        _sb_block(q, kk_ref[j], vv_ref[j], suffix, carry_ref, acc_ref, causal=(j * tk + k_off) < q_pos)

    def past(t, _):
        j = first_diag - 1 - t
        _sb_block(q, kk_ref[j], vv_ref[j], suffix, carry_ref, acc_ref, causal=None)
        return 0
    lax.fori_loop(0, first_diag, past, 0)

    o_ref[0] = _head_pair_rms(acc_ref[...], g_ref[...]).astype(o_ref.dtype)


def _sb_call(q, k, v, suffix, g):
    b, s, w = q.shape
    pairs = w // LANES
    nkb = s // SB_TK
    kv_spec = pl.BlockSpec((1, s, LANES), lambda bi, pi, i: (bi, 0, pi))
    return pl.pallas_call(
        _sb_kernel, grid=(b, pairs, s // SB_TQ),
        in_specs=[pl.BlockSpec((1, SB_TQ, LANES), lambda bi, pi, i: (bi, i, pi)), kv_spec, kv_spec,
                  _resident(suffix.shape), pl.BlockSpec((1, LANES), lambda bi, pi, i: (0, pi))],
        out_specs=pl.BlockSpec((1, SB_TQ, LANES), lambda bi, pi, i: (bi, i, pi)),
        out_shape=jax.ShapeDtypeStruct((b, s, w), BF16),
        scratch_shapes=[pltpu.VMEM((nkb, 2 * SB_TK, LANES), BF16), pltpu.VMEM((nkb, 2 * SB_TK, LANES), BF16),
                        pltpu.VMEM((SB_TQ, 2 * SB_TK), F32), pltpu.VMEM((SB_TQ, LANES), F32)],
        compiler_params=_params("parallel", "parallel", "arbitrary"), name="sb_attention",
    )(q, k, v, suffix, g)


def _softmax_unnormalised(s):
    e = jnp.exp(s - jnp.max(s, axis=-1, keepdims=True))
    return e, jnp.sum(e, axis=-1, keepdims=True)


def _cb_kernel(q_ref, k_ref, v_ref, bias_ref, g_ref, o_ref, kk_ref, vv_ref):
    i = pl.program_id(2)
    s_len = k_ref.shape[1]

    @pl.when(i == 0)
    def _():
        first = lax.broadcasted_iota(jnp.int32, (s_len, LANES), 1) < HEAD_DIM
        for ref, src in ((kk_ref, k_ref), (vv_ref, v_ref)):
            xb = src[0].astype(BF16)
            zero = jnp.zeros_like(xb)
            ref[:, :BAND, :] = jnp.zeros((2, BAND, LANES), BF16)
            ref[0, BAND:, :] = jnp.where(first, xb, zero)
            ref[1, BAND:, :] = jnp.where(first, zero, xb)

    q = q_ref[0]
    start = pl.multiple_of(i * CB_TQ, CB_TQ)
    exists = lax.broadcasted_iota(jnp.int32, (CB_TQ, CB_TK), 1) >= BAND - start
    first = lax.broadcasted_iota(jnp.int32, (CB_TQ, LANES), 1) < HEAD_DIM
    o = jnp.zeros((CB_TQ, LANES), F32)
    for h in range(2):
        s = _dot_nt(q, kk_ref[h, pl.ds(start, CB_TK), :]) + bias_ref[h]
        e, l = _softmax_unnormalised(jnp.where(exists, s, NEG))
        o_h = _dot(e.astype(BF16), vv_ref[h, pl.ds(start, CB_TK), :]) / l
        o = o_h if h == 0 else jnp.where(first, o, o_h)
    o_ref[0] = _head_pair_rms(o, g_ref[...]).astype(o_ref.dtype)


def _cb_call(q, k, v, bias, g):
    b, s, w = q.shape
    pairs = w // LANES
    kv_spec = pl.BlockSpec((1, s, LANES), lambda bi, pi, i: (bi, 0, pi))
    return pl.pallas_call(
        _cb_kernel, grid=(b, pairs, s // CB_TQ),
        in_specs=[pl.BlockSpec((1, CB_TQ, LANES), lambda bi, pi, i: (bi, i, pi)), kv_spec, kv_spec,
                  pl.BlockSpec((2, CB_TQ, CB_TK), lambda bi, pi, i: (pi, 0, 0)),
                  pl.BlockSpec((1, LANES), lambda bi, pi, i: (0, pi))],
        out_specs=pl.BlockSpec((1, CB_TQ, LANES), lambda bi, pi, i: (bi, i, pi)),
        out_shape=jax.ShapeDtypeStruct((b, s, w), BF16),
        scratch_shapes=[pltpu.VMEM((2, BAND + s, LANES), BF16), pltpu.VMEM((2, BAND + s, LANES), BF16)],
        compiler_params=_params("parallel", "parallel", "arbitrary"), name="cb_attention",
    )(q, k, v, bias, g)


def _bias_tiles_kernel(table_ref, o_ref, *, heads):
    layer, head = pl.program_id(0), pl.program_id(1)
    n_rel = 2 * REL_MAX + 1
    shape = o_ref.shape[2:]
    dist = (2 * CHUNK + lax.broadcasted_iota(jnp.int32, shape, 0) - lax.broadcasted_iota(jnp.int32, shape, 1))
    idx = jnp.clip(dist, -REL_MAX, REL_MAX) + REL_MAX

    def pick(r, acc):
        return jnp.where(idx == r, table_ref[(layer * n_rel + r) * heads + head], acc)
    o_ref[0, 0] = lax.fori_loop(0, n_rel, pick, jnp.zeros(shape, F32))


def _bias_tiles(rel_table):
    n_layers, _, heads = rel_table.shape
    return pl.pallas_call(
        functools.partial(_bias_tiles_kernel, heads=heads), grid=(n_layers, heads),
        in_specs=[pl.BlockSpec(memory_space=pltpu.SMEM)],
        out_specs=pl.BlockSpec((1, 1, CHUNK, 3 * CHUNK), lambda l, h: (l, h, 0, 0)),
        out_shape=jax.ShapeDtypeStruct((n_layers, heads, CHUNK, 3 * CHUNK), F32),
        compiler_params=_params("parallel", "parallel"), name="rel_bias_tiles",
    )(rel_table.reshape(-1))


def _prompt_bias(tiles, far):
    heads = tiles.shape[0]
    qc, kc = CB_TQ // CHUNK, CB_TK // CHUNK
    far_tile = jnp.broadcast_to(far[:, None, None], (heads, CHUNK, CHUNK))
    hidden = jnp.full((heads, CHUNK, CHUNK), NEG, F32)
    rows = []
    for c in range(qc):
        row = []
        for k in range(kc):
            m = k - c
            if m < 0 or m > BAND_CHUNKS:
                row.append(hidden)
            elif m < BAND_CHUNKS - 2:
                row.append(far_tile)
            else:
                t = m - (BAND_CHUNKS - 2)
                row.append(tiles[:, :, t * CHUNK:(t + 1) * CHUNK])
        rows.append(jnp.concatenate(row, axis=2))
    return jnp.concatenate(rows, axis=1)


def _head_rows(x, heads):
    t, w = x.shape
    tiled = jnp.concatenate([x] * heads, axis=0)
    row_head = lax.div(lax.broadcasted_iota(jnp.int32, tiled.shape, 0), t)
    col_head = lax.div(lax.broadcasted_iota(jnp.int32, tiled.shape, 1), HEAD_DIM)
    return jnp.where(row_head == col_head, tiled, jnp.zeros_like(tiled))


def _fold_head_rows(o, heads):
    t = o.shape[0] // heads
    col_head = lax.div(lax.broadcasted_iota(jnp.int32, (t, o.shape[1]), 1), HEAD_DIM)
    out = jnp.zeros((t, o.shape[1]), F32)
    for h in range(heads):
        out = jnp.where(col_head == h, o[h * t:(h + 1) * t], out)
    return out


def _all_heads_rms(o, g):
    parts = [_head_pair_rms(o[:, c:c + LANES], g[:, c:c + LANES]) for c in range(0, o.shape[1], LANES)]
    return jnp.concatenate(parts, axis=1)


def _pad_rows(x, rows):
    return jnp.concatenate([x, jnp.zeros((rows - x.shape[0], x.shape[1]), x.dtype)], axis=0)


def _sample_kernel(qa_ref, ka_ref, va_ref, cka_ref, cva_ref, qb_ref, kb_ref, vb_ref, ckb_ref, cvb_ref,
                   suffix_ref, bias_ref, ga_ref, gb_ref, oa_ref, ob_ref, carry_ref, acc_ref):
    t, w = qa_ref.shape[1:]
    heads = w // HEAD_DIM
    tk = SB_TK
    past = cka_ref.shape[1]

    carry_ref[...] = jnp.zeros_like(carry_ref)
    acc_ref[...] = jnp.zeros_like(acc_ref)
    q = _head_rows(qa_ref[0], heads)
    suffix = suffix_ref[...]
    frame = lax.rem(lax.broadcasted_iota(jnp.int32, (heads * t, tk), 0), t)
    key = lax.broadcasted_iota(jnp.int32, (heads * t, tk), 1)
    _sb_block(q, _pad_rows(ka_ref[0], tk).astype(BF16), _pad_rows(va_ref[0], tk).astype(BF16),
              suffix, carry_ref, acc_ref, causal=key < frame)

    def cached(step, _):
        rows = pl.ds(pl.multiple_of(past - (step + 1) * tk, tk), tk)
        _sb_block(q, cka_ref[0, rows, :].astype(BF16), cva_ref[0, rows, :].astype(BF16),
                  suffix, carry_ref, acc_ref, causal=None)
        return 0
    lax.fori_loop(0, past // tk, cached, 0)
    oa_ref[0] = _all_heads_rms(_fold_head_rows(acc_ref[...], heads), ga_ref[...]).astype(oa_ref.dtype)

    qb = _head_rows(qb_ref[0], heads)
    kc = jnp.concatenate([ckb_ref[0], _pad_rows(kb_ref[0], LANES)], axis=0).astype(BF16)
    vc = jnp.concatenate([cvb_ref[0], _pad_rows(vb_ref[0], LANES)], axis=0).astype(BF16)
    e, l = _softmax_unnormalised(_dot_nt(qb, kc) + bias_ref[...])
    o = _fold_head_rows(_dot(e.astype(BF16), vc) / l, heads)
    ob_ref[0] = _all_heads_rms(o, gb_ref[...]).astype(ob_ref.dtype)


def _sample_call(qa, ka, va, cka, cva, qb, kb, vb, ckb, cvb, suffix, bias, ga, gb):
    b, t, w = qa.shape
    heads = w // HEAD_DIM
    new = pl.BlockSpec((1, t, w), lambda bi: (bi, 0, 0))
    sb_cache = pl.BlockSpec((1, cka.shape[1], w), lambda bi: (bi, 0, 0))
    cb_cache = pl.BlockSpec((1, ckb.shape[1], w), lambda bi: (bi, 0, 0))
    return pl.pallas_call(
        _sample_kernel, grid=(b,),
        in_specs=[new, new, new, sb_cache, sb_cache, new, new, new, cb_cache, cb_cache,
                  _resident(suffix.shape), _resident(bias.shape), _resident(ga.shape), _resident(gb.shape)],
        out_specs=[new, new], out_shape=[jax.ShapeDtypeStruct((b, t, w), BF16)] * 2,
        scratch_shapes=[pltpu.VMEM((heads * t, SB_TK), F32), pltpu.VMEM((heads * t, w), F32)],
        compiler_params=_params("parallel"), name="sample_attention",
    )(qa, ka, va, cka, cva, qb, kb, vb, ckb, cvb, suffix, bias, ga, gb)


def _sample_bias(tiles, far, t):
    heads = tiles.shape[0]
    near = BAND - 2 * CHUNK
    parts = [jnp.broadcast_to(far[:, None, None], (heads, t, near)), tiles[:, :t, :2 * CHUNK + t],
             jnp.full((heads, t, LANES - t), NEG, F32)]
    return jnp.concatenate(parts, axis=2).reshape(heads * t, BAND + LANES)


def kernel(x_prompt, x_sample, p_prompt, p_sample, cache_sb_k, cache_sb_v, cache_cb_k, cache_cb_v, g_mix, w_in,
           rel_table, g_out_sb, g_out_cb, w_out, g_ffn, w_gate, w_up, w_down, g_ple, w_ple_gate, w_ple_proj,
           g_final):
    depth = w_in.shape[0]
    b, s, d = x_prompt.shape
    bs, t, _ = x_sample.shape
    w_sb = g_out_sb.shape[1]
    w_cb = g_out_cb.shape[1]
    heads_sb, heads_cb = w_sb // HEAD_DIM, w_cb // HEAD_DIM
    past, band = cache_sb_k.shape[2], cache_cb_k.shape[2]
    assert w_sb == w_cb and w_in.shape[2] == 3 * (w_sb + w_cb) and w_sb % LANES == 0
    assert s % SB_TQ == 0 and s % CB_TQ == 0 and s >= BAND and (b * s) % ROW_TILE == 0
    assert band == BAND and past % SB_TK == 0 and t <= CHUNK and t % 8 == 0

    row = lambda g: g.reshape(1, -1)
    w_in_b = w_in.astype(BF16)
    layers = [dict(wo_a=w_out[i, :w_sb].astype(BF16), wo_b=w_out[i, w_sb:].astype(BF16), g_ffn=row(g_ffn[i]),
                   w_gate=w_gate[i].astype(BF16), w_up=w_up[i].astype(BF16), w_down=w_down[i].astype(BF16),
                   g_ple=row(g_ple[i]), w_ple_gate=w_ple_gate[i].astype(BF16),
                   w_ple_proj=w_ple_proj[i].astype(BF16)) for i in range(depth)]
    suffix = _suffix_matrix(SB_TK)
    tiles = _bias_tiles(rel_table)
    far = rel_table[:, 2 * REL_MAX, :]

    xp = x_prompt.reshape(b * s, d)
    xs = x_sample.reshape(bs * t, d)
    qkv_p = _qkv_call(xp, row(g_mix[0]), w_in_b[0], ROW_TILE)
    qkv_s = _qkv_call(xs, row(g_mix[0]), w_in_b[0], bs * t)
    new_p, new_s = [], []
    for i in range(depth):
        qa, ka, va, qb, kb, vb = [u.reshape(b, s, -1) for u in qkv_p]
        new_p.append((ka, va, kb[:, s - BAND:], vb[:, s - BAND:]))
        a = _sb_call(qa, ka, va, suffix, row(g_out_sb[i]))
        ob = _cb_call(qb, kb, vb, _prompt_bias(tiles[i], far[i]), row(g_out_cb[i]))

        sqa, ska, sva, sqb, skb, svb = [u.reshape(bs, t, -1) for u in qkv_s]
        new_s.append((ska, sva, skb, svb))
        sa, sob = _sample_call(sqa, ska, sva, cache_sb_k[i].reshape(bs, past, w_sb),
                               cache_sb_v[i].reshape(bs, past, w_sb), sqb, skb, svb,
                               cache_cb_k[i].reshape(bs, band, w_cb), cache_cb_v[i].reshape(bs, band, w_cb),
                               suffix, _sample_bias(tiles[i], far[i], t), row(g_out_sb[i]), row(g_out_cb[i]))

        last = i == depth - 1
        g_next = row(g_final) if last else row(g_mix[i + 1])
        w_next = None if last else w_in_b[i + 1]
        out_p = _layer_call(xp, a.reshape(b * s, w_sb), ob.reshape(b * s, w_cb), p_prompt[i].reshape(b * s, -1),
                            layers[i], g_next, w_next, ROW_TILE)
        out_s = _layer_call(xs, sa.reshape(bs * t, w_sb), sob.reshape(bs * t, w_cb), p_sample[i].reshape(bs * t, -1),
                            layers[i], g_next, w_next, bs * t)
        if last:
            xp, xs = out_p, out_s
        else:
            xp, qkv_p = out_p[0], out_p[1:]
            xs, qkv_s = out_s[0], out_s[1:]

    def stack(items, n_heads, length):
        return jnp.stack(items).reshape(depth, -1, length, n_heads, HEAD_DIM)

    return (xp.reshape(b, s, d), xs.reshape(bs, t, d),
            stack([e[0] for e in new_p], heads_sb, s), stack([e[1] for e in new_p], heads_sb, s),
            stack([e[2] for e in new_p], heads_cb, BAND), stack([e[3] for e in new_p], heads_cb, BAND),
            stack([e[0] for e in new_s], heads_sb, t), stack([e[1] for e in new_s], heads_sb, t),
            stack([e[2] for e in new_s], heads_cb, t), stack([e[3] for e in new_s], heads_cb, t))
```

```python
import functools

import jax
import jax.numpy as jnp
from jax import lax
from jax.experimental import pallas as pl
from jax.experimental.pallas import tpu as pltpu

F32 = jnp.float32
BF16 = jnp.bfloat16

HEAD_DIM = 64
CHUNK = 64
BAND_CHUNKS = 8
BAND = BAND_CHUNKS * CHUNK
REL_MAX = 128
EPS = 1e-6
Q_SCALE = HEAD_DIM ** -0.5

LANES = 128
NEG = -1e30

ROW_TILE = 256
SB_TQ = 256
SB_TK = 128
CB_TQ = 256
CB_TK = CB_TQ + BAND
VMEM_LIMIT = 56 * 1024 * 1024


def _params(*semantics):
    return pltpu.CompilerParams(dimension_semantics=semantics, vmem_limit_bytes=VMEM_LIMIT)


def _resident(shape, layer=None):
    zeros = (0,) * len(shape)
    if layer is None:
        return pl.BlockSpec(shape, lambda *_: zeros, pipeline_mode=pl.Buffered(1))
    return pl.BlockSpec((None,) + tuple(shape), lambda *_: (layer,) + zeros, pipeline_mode=pl.Buffered(1))


def _rms_unit(x):
    return x * lax.rsqrt(jnp.mean(x * x, axis=-1, keepdims=True) + EPS)


def _dot(a, b):
    return jnp.dot(a, b, preferred_element_type=F32)


def _dot_nt(a, b):
    return lax.dot_general(a, b, (((1,), (1,)), ((), ())), preferred_element_type=F32)


def _softplus(z):
    return jnp.maximum(z, 0.0) + jnp.log(1.0 + jnp.exp(-jnp.abs(z)))


def _head_pair_rms(o, g):
    first = lax.broadcasted_iota(jnp.int32, o.shape, 1) < HEAD_DIM
    o2 = o * o
    s0 = jnp.sum(jnp.where(first, o2, 0.0), axis=-1, keepdims=True)
    s1 = jnp.sum(jnp.where(first, 0.0, o2), axis=-1, keepdims=True)
    ms = jnp.where(first, s0, s1) * (1.0 / HEAD_DIM)
    return o * lax.rsqrt(ms + EPS) * g


def _store_qkv(h, wq_ref, wkvt_ref, outs, per_head_t):
    qa_ref, qb_ref = outs[:2]
    w = qa_ref.shape[-1]
    q = _dot(h, wq_ref[...]) * Q_SCALE
    qa_ref[...] = q[:, :w].astype(BF16)
    qb_ref[...] = q[:, w:].astype(BF16)
    for c, ref in enumerate(outs[2:]):
        w_t = wkvt_ref[c * w:(c + 1) * w, :]
        if per_head_t:
            ref[0] = _dot_nt(w_t, h).reshape(ref.shape[1:])
        else:
            ref[...] = _dot_nt(h, w_t)


def _qkv_kernel(x_ref, g_ref, wq_ref, wkvt_ref, *outs, per_head_t):
    h = (_rms_unit(x_ref[...]) * g_ref[...]).astype(BF16)
    _store_qkv(h, wq_ref, wkvt_ref, outs, per_head_t)


def _layer_tail(x_ref, a_ref, ob_ref, p_ref, wo_ref, gf_ref, wg_ref, wu_ref, wd_ref, gp_ref, wpg_ref, wpp_ref):
    w = a_ref.shape[-1]
    x = x_ref[...] + _dot(a_ref[...], wo_ref[:w, :]) + _dot(ob_ref[...], wo_ref[w:, :])
    h = (_rms_unit(x) * gf_ref[...]).astype(BF16)
    gate = _dot(h, wg_ref[...])
    up = _dot(h, wu_ref[...])
    act = (gate * jax.nn.sigmoid(gate) * up).astype(BF16)
    x = x + _dot(act, wd_ref[...])
    hp = (_rms_unit(x) * gp_ref[...]).astype(BF16)
    ple_gate = jax.nn.sigmoid(_dot(hp, wpg_ref[...]))
    return x + ple_gate * _dot(p_ref[...].astype(BF16), wpp_ref[...])


N_TAIL = 12


def _mid_layer_kernel(*refs, per_head_t):
    tail, (gn_ref, wq_ref, wkvt_ref, xo_ref), outs = refs[:N_TAIL], refs[N_TAIL:N_TAIL + 4], refs[N_TAIL + 4:]
    x = _layer_tail(*tail)
    xo_ref[...] = x
    _store_qkv((_rms_unit(x) * gn_ref[...]).astype(BF16), wq_ref, wkvt_ref, outs, per_head_t)


def _last_layer_kernel(*refs):
    tail, (gn_ref, y_ref) = refs[:N_TAIL], refs[N_TAIL:]
    y_ref[...] = _rms_unit(_layer_tail(*tail)) * gn_ref[...]


def _row_spec(tm, width):
    return pl.BlockSpec((tm, width), lambda i: (i, 0))


def _qkv_out(rows, w, tm, seq):
    q_shapes = [jax.ShapeDtypeStruct((rows, w), BF16)] * 2
    q_specs = [_row_spec(tm, w)] * 2
    if seq is None:
        return q_shapes + [jax.ShapeDtypeStruct((rows, w), F32)] * 4, q_specs + [_row_spec(tm, w)] * 4
    heads, tiles = w // HEAD_DIM, seq // tm
    kv_shape = jax.ShapeDtypeStruct((rows // seq, heads, HEAD_DIM, seq), F32)
    kv_spec = pl.BlockSpec((1, heads, HEAD_DIM, tm), lambda i: (i // tiles, 0, 0, i % tiles))
    return q_shapes + [kv_shape] * 4, q_specs + [kv_spec] * 4


def _qkv_call(x, wts, layer, tm, seq):
    rows, d = x.shape
    wq, wkvt, g = wts["wq"], wts["wkvt"], wts["g_mix"]
    out_shape, out_specs = _qkv_out(rows, wq.shape[2] // 2, tm, seq)
    return pl.pallas_call(
        functools.partial(_qkv_kernel, per_head_t=seq is not None), grid=(rows // tm,),
        in_specs=[_row_spec(tm, d), _resident(g.shape[1:], layer), _resident(wq.shape[1:], layer),
                  _resident(wkvt.shape[1:], layer)],
        out_specs=out_specs, out_shape=out_shape,
        compiler_params=_params("parallel"), name="qkv_proj",
    )(x, g, wq, wkvt)


TAIL_WEIGHTS = ("w_out", "g_ffn", "w_gate", "w_up", "w_down", "g_ple", "w_ple_gate", "w_ple_proj")


def _layer_call(x, a, ob, p, wts, layer, g_final, tm, seq):
    rows, d = x.shape
    last = layer == wts["w_out"].shape[0] - 1
    tail_in = [x, a, ob, p] + [wts[k] for k in TAIL_WEIGHTS]
    tail_specs = ([_row_spec(tm, d), _row_spec(tm, a.shape[1]), _row_spec(tm, ob.shape[1]),
                   pl.BlockSpec((None, tm, p.shape[2]), lambda i: (layer, i, 0))]
                  + [_resident(wts[k].shape[1:], layer) for k in TAIL_WEIGHTS])
    if last:
        return pl.pallas_call(
            _last_layer_kernel, grid=(rows // tm,), in_specs=tail_specs + [_resident(g_final.shape)],
            out_specs=_row_spec(tm, d), out_shape=jax.ShapeDtypeStruct((rows, d), F32),
            compiler_params=_params("parallel"), name="layer_tail_final",
        )(*tail_in, g_final)
    nxt = [wts["g_mix"], wts["wq"], wts["wkvt"]]
    out_shape, out_specs = _qkv_out(rows, wts["wq"].shape[2] // 2, tm, seq)
    return pl.pallas_call(
        functools.partial(_mid_layer_kernel, per_head_t=seq is not None), grid=(rows // tm,),
        in_specs=tail_specs + [_resident(t.shape[1:], layer + 1) for t in nxt],
        out_specs=[_row_spec(tm, d)] + out_specs,
        out_shape=[jax.ShapeDtypeStruct((rows, d), F32)] + out_shape,
        compiler_params=_params("parallel"), name="layer_tail_qkv",
    )(*tail_in, *nxt)


def _suffix_matrix(tk):
    j = lax.broadcasted_iota(jnp.int32, (tk, 2 * tk), 0)
    c = lax.broadcasted_iota(jnp.int32, (tk, 2 * tk), 1)
    return jnp.where((c >= tk) | (j > c), -1.0, 0.0).astype(BF16)


def _sb_weights(z, suffix, carry_ref, causal):
    tk = suffix.shape[0]
    sp = _softplus(z)
    w_parts = []
    for h in range(z.shape[1] // tk):
        cols = slice(h * tk, (h + 1) * tk)
        z_h, sp_h = z[:, cols], sp[:, cols]
        drop = sp_h if causal is None else jnp.where(causal, sp_h, 0.0)
        sums = _dot(drop.astype(BF16), suffix)
        carry = carry_ref[:, cols]
        log_w = (z_h - sp_h) + (sums[:, :tk] + carry)
        w = jnp.exp(log_w)
        if causal is not None:
            w = jnp.where(causal, w, 0.0)
        carry_ref[:, cols] = carry + sums[:, tk:]
        w_parts.append(w.astype(BF16))
    return jnp.concatenate(w_parts, axis=1)


def _sb_kernel(q_ref, kt_ref, vt_ref, suffix_ref, g_ref, o_ref, kk_ref, vv_ref, carry_ref, acc_ref, z_ref, w_ref):
    tq, tk = SB_TQ, SB_TK
    per_q = tq // tk
    i = pl.program_id(2)

    @pl.when(i == 0)
    def _():
        kk_ref[...] = jnp.zeros_like(kk_ref)
        vv_ref[...] = jnp.zeros_like(vv_ref)
        for j in range(kk_ref.shape[0]):
            for h in range(2):
                rows, cols = slice(h * HEAD_DIM, (h + 1) * HEAD_DIM), slice(h * tk, (h + 1) * tk)
                kk_ref[j, rows, cols] = kt_ref[0, h, :, j * tk:(j + 1) * tk].astype(BF16)
                vv_ref[j, rows, cols] = vt_ref[0, h, :, j * tk:(j + 1) * tk].astype(BF16)

    carry_ref[...] = jnp.zeros_like(carry_ref)
    acc_ref[...] = jnp.zeros_like(acc_ref)
    q = q_ref[0]
    suffix = suffix_ref[...]
    q_pos = i * tq + lax.broadcasted_iota(jnp.int32, (tq, tk), 0)
    k_off = lax.broadcasted_iota(jnp.int32, (tq, tk), 1)
    newest = (i + 1) * per_q - 1

    def step(n, slot, causal):
        j = newest - n
        z = z_ref[slot]
        z_ref[1 - slot] = _dot(q, kk_ref[jnp.maximum(j - 1, 0)])
        if not (isinstance(n, int) and n == 0):
            acc_ref[...] += _dot_nt(w_ref[1 - slot], vv_ref[j + 1])
        w_ref[slot] = _sb_weights(z, suffix, carry_ref, None if causal is None else causal(j))

    z_ref[0] = _dot(q, kk_ref[newest])
    for n in range(per_q):
        step(n, n % 2, lambda j: (j * tk + k_off) < q_pos)

    def past(t, _):
        for d in range(per_q):
            step(per_q * (t + 1) + d, d % 2, None)
        return 0
    lax.fori_loop(0, i, past, 0)

    acc_ref[...] += _dot_nt(w_ref[(per_q - 1) % 2], vv_ref[0])
    o_ref[0] = _head_pair_rms(acc_ref[...], g_ref[...]).astype(o_ref.dtype)


def _pair_kv_spec(s):
    return pl.BlockSpec((1, 2, HEAD_DIM, s), lambda bi, pi, i: (bi, pi, 0, 0))


def _sb_call(q, kt, vt, suffix, g, layer):
    b, s, w = q.shape
    nkb = s // SB_TK
    assert (SB_TQ // SB_TK) % 2 == 0
    return pl.pallas_call(
        _sb_kernel, grid=(b, w // LANES, s // SB_TQ),
        in_specs=[pl.BlockSpec((1, SB_TQ, LANES), lambda bi, pi, i: (bi, i, pi)), _pair_kv_spec(s), _pair_kv_spec(s),
                  _resident(suffix.shape), pl.BlockSpec((None, 1, LANES), lambda bi, pi, i: (layer, 0, pi))],
        out_specs=pl.BlockSpec((1, SB_TQ, LANES), lambda bi, pi, i: (bi, i, pi)),
        out_shape=jax.ShapeDtypeStruct((b, s, w), BF16),
        scratch_shapes=[pltpu.VMEM((nkb, LANES, 2 * SB_TK), BF16), pltpu.VMEM((nkb, LANES, 2 * SB_TK), BF16),
                        pltpu.VMEM((SB_TQ, 2 * SB_TK), F32), pltpu.VMEM((SB_TQ, LANES), F32),
                        pltpu.VMEM((2, SB_TQ, 2 * SB_TK), F32), pltpu.VMEM((2, SB_TQ, 2 * SB_TK), BF16)],
        compiler_params=_params("parallel", "parallel", "arbitrary"), name="sb_attention",
    )(q, kt, vt, suffix, g)


def _softmax_unnormalised(s):
    e = jnp.exp(s - jnp.max(s, axis=-1, keepdims=True))
    return e, jnp.sum(e, axis=-1, keepdims=True)


def _cb_kernel(q_ref, kt_ref, vt_ref, bias_ref, g_ref, o_ref, kk_ref, vv_ref):
    i = pl.program_id(2)
    lead = BAND // CB_TQ
    window = CB_TK // CB_TQ

    @pl.when(i == 0)
    def _():
        kk_ref[...] = jnp.zeros_like(kk_ref)
        vv_ref[...] = jnp.zeros_like(vv_ref)
        for h in range(2):
            rows = slice(h * HEAD_DIM, (h + 1) * HEAD_DIM)
            for c in range(kk_ref.shape[1] - lead):
                kk_ref[h, lead + c, rows, :] = kt_ref[0, h, :, c * CB_TQ:(c + 1) * CB_TQ].astype(BF16)
                vv_ref[h, lead + c, rows, :] = vt_ref[0, h, :, c * CB_TQ:(c + 1) * CB_TQ].astype(BF16)

    q = q_ref[0]
    exists = lax.broadcasted_iota(jnp.int32, (CB_TQ, CB_TK), 1) >= BAND - i * CB_TQ
    first = lax.broadcasted_iota(jnp.int32, (CB_TQ, LANES), 1) < HEAD_DIM
    o = jnp.zeros((CB_TQ, LANES), F32)
    for h in range(2):
        kt = jnp.concatenate([kk_ref[h, i + c] for c in range(window)], axis=1)
        vt = jnp.concatenate([vv_ref[h, i + c] for c in range(window)], axis=1)
        e, l = _softmax_unnormalised(jnp.where(exists, _dot(q, kt) + bias_ref[h], NEG))
        o_h = _dot_nt(e.astype(BF16), vt) / l
        o = o_h if h == 0 else jnp.where(first, o, o_h)
    o_ref[0] = _head_pair_rms(o, g_ref[...]).astype(o_ref.dtype)


def _cb_call(q, kt, vt, bias, g, layer):
    b, s, w = q.shape
    blocks = (BAND + s) // CB_TQ
    return pl.pallas_call(
        _cb_kernel, grid=(b, w // LANES, s // CB_TQ),
        in_specs=[pl.BlockSpec((1, CB_TQ, LANES), lambda bi, pi, i: (bi, i, pi)), _pair_kv_spec(s), _pair_kv_spec(s),
                  pl.BlockSpec((2, CB_TQ, CB_TK), lambda bi, pi, i: (pi, 0, 0)),
                  pl.BlockSpec((None, 1, LANES), lambda bi, pi, i: (layer, 0, pi))],
        out_specs=pl.BlockSpec((1, CB_TQ, LANES), lambda bi, pi, i: (bi, i, pi)),
        out_shape=jax.ShapeDtypeStruct((b, s, w), BF16),
        scratch_shapes=[pltpu.VMEM((2, blocks, LANES, CB_TQ), BF16), pltpu.VMEM((2, blocks, LANES, CB_TQ), BF16)],
        compiler_params=_params("parallel", "parallel", "arbitrary"), name="cb_attention",
    )(q, kt, vt, bias, g)


def _bias_tiles_kernel(table_ref, o_ref, *, heads):
    layer, head = pl.program_id(0), pl.program_id(1)
    n_rel = 2 * REL_MAX + 1
    shape = o_ref.shape[2:]
    dist = (2 * CHUNK + lax.broadcasted_iota(jnp.int32, shape, 0) - lax.broadcasted_iota(jnp.int32, shape, 1))
    idx = jnp.clip(dist, -REL_MAX, REL_MAX) + REL_MAX

    def pick(r, acc):
        return jnp.where(idx == r, table_ref[(layer * n_rel + r) * heads + head], acc)
    o_ref[0, 0] = lax.fori_loop(0, n_rel, pick, jnp.zeros(shape, F32))


def _bias_tiles(rel_table):
    n_layers, _, heads = rel_table.shape
    return pl.pallas_call(
        functools.partial(_bias_tiles_kernel, heads=heads), grid=(n_layers, heads),
        in_specs=[pl.BlockSpec(memory_space=pltpu.SMEM)],
        out_specs=pl.BlockSpec((1, 1, CHUNK, 3 * CHUNK), lambda l, h: (l, h, 0, 0)),
        out_shape=jax.ShapeDtypeStruct((n_layers, heads, CHUNK, 3 * CHUNK), F32),
        compiler_params=_params("parallel", "parallel"), name="rel_bias_tiles",
    )(rel_table.reshape(-1))


def _prompt_bias(tiles, far):
    heads = tiles.shape[0]
    qc, kc = CB_TQ // CHUNK, CB_TK // CHUNK
    far_tile = jnp.broadcast_to(far[:, None, None], (heads, CHUNK, CHUNK))
    hidden = jnp.full((heads, CHUNK, CHUNK), NEG, F32)
    rows = []
    for c in range(qc):
        row = []
        for k in range(kc):
            m = k - c
            if m < 0 or m > BAND_CHUNKS:
                row.append(hidden)
            elif m < BAND_CHUNKS - 2:
                row.append(far_tile)
            else:
                t = m - (BAND_CHUNKS - 2)
                row.append(tiles[:, :, t * CHUNK:(t + 1) * CHUNK])
        rows.append(jnp.concatenate(row, axis=2))
    return jnp.concatenate(rows, axis=1)


def _head_rows(x, heads):
    t, w = x.shape
    tiled = jnp.concatenate([x] * heads, axis=0)
    row_head = lax.div(lax.broadcasted_iota(jnp.int32, tiled.shape, 0), t)
    col_head = lax.div(lax.broadcasted_iota(jnp.int32, tiled.shape, 1), HEAD_DIM)
    return jnp.where(row_head == col_head, tiled, jnp.zeros_like(tiled))


def _fold_head_rows(o, heads):
    t = o.shape[0] // heads
    col_head = lax.div(lax.broadcasted_iota(jnp.int32, (t, o.shape[1]), 1), HEAD_DIM)
    out = jnp.zeros((t, o.shape[1]), F32)
    for h in range(heads):
        out = jnp.where(col_head == h, o[h * t:(h + 1) * t], out)
    return out


def _all_heads_rms(o, g):
    parts = [_head_pair_rms(o[:, c:c + LANES], g[:, c:c + LANES]) for c in range(0, o.shape[1], LANES)]
    return jnp.concatenate(parts, axis=1)


def _pad_rows(x, rows):
    return jnp.concatenate([x, jnp.zeros((rows - x.shape[0], x.shape[1]), x.dtype)], axis=0)


def _sample_kernel(qa_ref, ka_ref, va_ref, ckt_ref, cvt_ref, qb_ref, kb_ref, vb_ref, ckbt_ref, cvbt_ref,
                   suffix_ref, bias_ref, ga_ref, gb_ref, oa_ref, ob_ref, carry_ref, acc_ref):
    t, w = qa_ref.shape[1:]
    heads = w // HEAD_DIM
    tk = SB_TK
    past = ckt_ref.shape[2]

    carry_ref[...] = jnp.zeros_like(carry_ref)
    q = _head_rows(qa_ref[0], heads)
    suffix = suffix_ref[...]
    frame = lax.rem(lax.broadcasted_iota(jnp.int32, (heads * t, tk), 0), t)
    key = lax.broadcasted_iota(jnp.int32, (heads * t, tk), 1)
    wts = _sb_weights(_dot_nt(q, _pad_rows(ka_ref[0], tk).astype(BF16)), suffix, carry_ref, key < frame)
    acc_ref[...] = _dot(wts, _pad_rows(va_ref[0], tk).astype(BF16))
    for blk in reversed(range(past // tk)):
        cols = slice(blk * tk, (blk + 1) * tk)
        wts = _sb_weights(_dot(q, ckt_ref[0, :, cols].astype(BF16)), suffix, carry_ref, None)
        acc_ref[...] += _dot_nt(wts, cvt_ref[0, :, cols].astype(BF16))
    oa_ref[0] = _all_heads_rms(_fold_head_rows(acc_ref[...], heads), ga_ref[...]).astype(oa_ref.dtype)

    qb = _head_rows(qb_ref[0], heads)
    s = jnp.concatenate([_dot(qb, ckbt_ref[0].astype(BF16)),
                         _dot_nt(qb, _pad_rows(kb_ref[0], LANES).astype(BF16))], axis=1)
    e, l = _softmax_unnormalised(s + bias_ref[...])
    e = e.astype(BF16)
    band = ckbt_ref.shape[2]
    o = _dot_nt(e[:, :band], cvbt_ref[0].astype(BF16)) + _dot(e[:, band:], _pad_rows(vb_ref[0], LANES).astype(BF16))
    ob_ref[0] = _all_heads_rms(_fold_head_rows(o / l, heads), gb_ref[...]).astype(ob_ref.dtype)


def _sample_call(qa, ka, va, ckt, cvt, qb, kb, vb, ckbt, cvbt, suffix, bias, ga, gb, layer):
    b, t, w = qa.shape
    heads = w // HEAD_DIM
    new = pl.BlockSpec((1, t, w), lambda bi: (bi, 0, 0))

    def cache(c):
        return pl.BlockSpec((None, 1) + c.shape[2:], lambda bi: (layer, bi, 0, 0))
    return pl.pallas_call(
        _sample_kernel, grid=(b,),
        in_specs=[new, new, new, cache(ckt), cache(cvt), new, new, new, cache(ckbt), cache(cvbt),
                  _resident(suffix.shape), _resident(bias.shape), _resident(ga.shape[1:], layer),
                  _resident(gb.shape[1:], layer)],
        out_specs=[new, new], out_shape=[jax.ShapeDtypeStruct((b, t, w), BF16)] * 2,
        scratch_shapes=[pltpu.VMEM((heads * t, SB_TK), F32), pltpu.VMEM((heads * t, w), F32)],
        compiler_params=_params("parallel"), name="sample_attention",
    )(qa, ka, va, ckt, cvt, qb, kb, vb, ckbt, cvbt, suffix, bias, ga, gb)


def _sample_bias(tiles, far, t):
    heads = tiles.shape[0]
    near = BAND - 2 * CHUNK
    parts = [jnp.broadcast_to(far[:, None, None], (heads, t, near)), tiles[:, :t, :2 * CHUNK + t],
             jnp.full((heads, t, LANES - t), NEG, F32)]
    return jnp.concatenate(parts, axis=2).reshape(heads * t, BAND + LANES)


def _per_head_t(cache):
    n_l, n_b, n_p, n_h, n_d = cache.shape
    return jnp.transpose(cache, (0, 1, 3, 4, 2)).reshape(n_l, n_b, n_h * n_d, n_p)


def kernel(x_prompt, x_sample, p_prompt, p_sample, cache_sb_k, cache_sb_v, cache_cb_k, cache_cb_v, g_mix, w_in,
           rel_table, g_out_sb, g_out_cb, w_out, g_ffn, w_gate, w_up, w_down, g_ple, w_ple_gate, w_ple_proj,
           g_final):
    depth = w_in.shape[0]
    b, s, d = x_prompt.shape
    bs, t, _ = x_sample.shape
    w = g_out_sb.shape[1]
    heads = w // HEAD_DIM
    past, band = cache_sb_k.shape[2], cache_cb_k.shape[2]
    assert g_out_cb.shape[1] == w and w_in.shape[2] == 6 * w and w % LANES == 0
    assert s % SB_TQ == 0 and s % CB_TQ == 0 and s >= BAND and s % ROW_TILE == 0
    assert band == BAND and past % SB_TK == 0 and t <= CHUNK and t % 8 == 0

    gain = lambda g: g[:, None, :]
    wts = dict(
        g_mix=gain(g_mix), g_ffn=gain(g_ffn), g_ple=gain(g_ple),
        wq=jnp.concatenate([w_in[:, :, :w], w_in[:, :, 3 * w:4 * w]], axis=2).astype(BF16),
        wkvt=jnp.swapaxes(jnp.concatenate([w_in[:, :, w:3 * w], w_in[:, :, 4 * w:]], axis=2), 1, 2).astype(BF16),
        w_out=w_out.astype(BF16), w_gate=w_gate.astype(BF16), w_up=w_up.astype(BF16), w_down=w_down.astype(BF16),
        w_ple_gate=w_ple_gate.astype(BF16), w_ple_proj=w_ple_proj.astype(BF16))
    g_sb, g_cb = gain(g_out_sb), gain(g_out_cb)
    g_fin = g_final.reshape(1, d)
    suffix = _suffix_matrix(SB_TK)
    tiles = _bias_tiles(rel_table)
    far = rel_table[:, 2 * REL_MAX, :]
    caches = [_per_head_t(c) for c in (cache_sb_k, cache_sb_v, cache_cb_k, cache_cb_v)]
    pp = p_prompt.reshape(depth, b * s, -1)
    ps = p_sample.reshape(depth, bs * t, -1)

    xp = x_prompt.reshape(b * s, d)
    xs = x_sample.reshape(bs * t, d)
    qkv_p = _qkv_call(xp, wts, 0, ROW_TILE, s)
    qkv_s = _qkv_call(xs, wts, 0, bs * t, None)
    new_p, new_s = [], []
    for i in range(depth):
        qa, qb = [u.reshape(b, s, w) for u in qkv_p[:2]]
        kat, vat, kbt, vbt = qkv_p[2:]
        new_p.append((kat, vat, kbt[..., s - BAND:], vbt[..., s - BAND:]))
        a = _sb_call(qa, kat, vat, suffix, g_sb, i)
        ob = _cb_call(qb, kbt, vbt, _prompt_bias(tiles[i], far[i]), g_cb, i)

        sqa, sqb, ska, sva, skb, svb = [u.reshape(bs, t, w) for u in qkv_s]
        new_s.append((ska, sva, skb, svb))
        sa, sob = _sample_call(sqa, ska, sva, caches[0], caches[1], sqb, skb, svb, caches[2], caches[3],
                               suffix, _sample_bias(tiles[i], far[i], t), g_sb, g_cb, i)

        out_p = _layer_call(xp, a.reshape(b * s, w), ob.reshape(b * s, w), pp, wts, i, g_fin, ROW_TILE, s)
        out_s = _layer_call(xs, sa.reshape(bs * t, w), sob.reshape(bs * t, w), ps, wts, i, g_fin, bs * t, None)
        if i == depth - 1:
            xp, xs = out_p, out_s
        else:
            xp, qkv_p = out_p[0], out_p[1:]
            xs, qkv_s = out_s[0], out_s[1:]

    def stack_t(items):
        return jnp.transpose(jnp.stack(items), (0, 1, 4, 2, 3))

    def stack(items):
        return jnp.stack(items).reshape(depth, bs, t, heads, HEAD_DIM)

    return (xp.reshape(b, s, d), xs.reshape(bs, t, d),
            stack_t([e[0] for e in new_p]), stack_t([e[1] for e in new_p]),
            stack_t([e[2] for e in new_p]), stack_t([e[3] for e in new_p]),
            stack([e[0] for e in new_s]), stack([e[1] for e in new_s]),
            stack([e[2] for e in new_s]), stack([e[3] for e in new_s]))
```

```python
import functools

import jax
import jax.numpy as jnp
from jax import lax
from jax.experimental import pallas as pl
from jax.experimental.pallas import tpu as pltpu

F32 = jnp.float32
BF16 = jnp.bfloat16

HEAD_DIM = 64
CHUNK = 64
BAND_CHUNKS = 8
BAND = BAND_CHUNKS * CHUNK
REL_MAX = 128
EPS = 1e-6
Q_SCALE = HEAD_DIM ** -0.5

LANES = 128
NEG = -1e30

ROW_TILE = 256
SB_TQ = 256
SB_TK = 128
SB_DEAD = -120.0
CB_TQ = 256
CB_TK = CB_TQ + BAND
VMEM_LIMIT = 56 * 1024 * 1024


def _params(*semantics):
    return pltpu.CompilerParams(dimension_semantics=semantics, vmem_limit_bytes=VMEM_LIMIT)


def _resident(shape, layer=None):
    zeros = (0,) * len(shape)
    if layer is None:
        return pl.BlockSpec(shape, lambda *_: zeros, pipeline_mode=pl.Buffered(1))
    return pl.BlockSpec((None,) + tuple(shape), lambda *_: (layer,) + zeros, pipeline_mode=pl.Buffered(1))


def _rms_unit(x):
    return x * lax.rsqrt(jnp.mean(x * x, axis=-1, keepdims=True) + EPS)


def _dot(a, b):
    return jnp.dot(a, b, preferred_element_type=F32)


def _dot_nt(a, b):
    return lax.dot_general(a, b, (((1,), (1,)), ((), ())), preferred_element_type=F32)


def _softplus(z):
    return jnp.maximum(z, 0.0) + jnp.log(1.0 + jnp.exp(-jnp.abs(z)))


def _head_pair_rms(o, g):
    first = lax.broadcasted_iota(jnp.int32, o.shape, 1) < HEAD_DIM
    o2 = o * o
    s0 = jnp.sum(jnp.where(first, o2, 0.0), axis=-1, keepdims=True)
    s1 = jnp.sum(jnp.where(first, 0.0, o2), axis=-1, keepdims=True)
    ms = jnp.where(first, s0, s1) * (1.0 / HEAD_DIM)
    return o * lax.rsqrt(ms + EPS) * g


def _store_qkv(h, wq_ref, wkvt_ref, outs, per_head_t):
    qa_ref, qb_ref = outs[:2]
    w = qa_ref.shape[-1]
    q = _dot(h, wq_ref[...]) * Q_SCALE
    qa_ref[...] = q[:, :w].astype(BF16)
    qb_ref[...] = q[:, w:].astype(BF16)
    for c, ref in enumerate(outs[2:]):
        w_t = wkvt_ref[c * w:(c + 1) * w, :]
        if per_head_t:
            ref[0] = _dot_nt(w_t, h).reshape(ref.shape[1:])
        else:
            ref[...] = _dot_nt(h, w_t)


def _qkv_kernel(x_ref, g_ref, wq_ref, wkvt_ref, *outs, per_head_t):
    h = (_rms_unit(x_ref[...]) * g_ref[...]).astype(BF16)
    _store_qkv(h, wq_ref, wkvt_ref, outs, per_head_t)


def _layer_tail(x_ref, a_ref, ob_ref, p_ref, wo_ref, gf_ref, wg_ref, wu_ref, wd_ref, gp_ref, wpg_ref, wpp_ref):
    w = a_ref.shape[-1]
    x = x_ref[...] + _dot(a_ref[...], wo_ref[:w, :]) + _dot(ob_ref[...], wo_ref[w:, :])
    h = (_rms_unit(x) * gf_ref[...]).astype(BF16)
    gate = _dot(h, wg_ref[...])
    up = _dot(h, wu_ref[...])
    act = (gate * jax.nn.sigmoid(gate) * up).astype(BF16)
    x = x + _dot(act, wd_ref[...])
    hp = (_rms_unit(x) * gp_ref[...]).astype(BF16)
    ple_gate = jax.nn.sigmoid(_dot(hp, wpg_ref[...]))
    return x + ple_gate * _dot(p_ref[...].astype(BF16), wpp_ref[...])


N_TAIL = 12


def _mid_layer_kernel(*refs, per_head_t):
    tail, (gn_ref, wq_ref, wkvt_ref, xo_ref), outs = refs[:N_TAIL], refs[N_TAIL:N_TAIL + 4], refs[N_TAIL + 4:]
    x = _layer_tail(*tail)
    xo_ref[...] = x
    _store_qkv((_rms_unit(x) * gn_ref[...]).astype(BF16), wq_ref, wkvt_ref, outs, per_head_t)


def _last_layer_kernel(*refs):
    tail, (gn_ref, y_ref) = refs[:N_TAIL], refs[N_TAIL:]
    y_ref[...] = _rms_unit(_layer_tail(*tail)) * gn_ref[...]


def _row_spec(tm, width):
    return pl.BlockSpec((tm, width), lambda i: (i, 0))


def _qkv_out(rows, w, tm, seq):
    q_shapes = [jax.ShapeDtypeStruct((rows, w), BF16)] * 2
    q_specs = [_row_spec(tm, w)] * 2
    if seq is None:
        return q_shapes + [jax.ShapeDtypeStruct((rows, w), F32)] * 4, q_specs + [_row_spec(tm, w)] * 4
    heads, tiles = w // HEAD_DIM, seq // tm
    kv_shape = jax.ShapeDtypeStruct((rows // seq, heads, HEAD_DIM, seq), F32)
    kv_spec = pl.BlockSpec((1, heads, HEAD_DIM, tm), lambda i: (i // tiles, 0, 0, i % tiles))
    return q_shapes + [kv_shape] * 4, q_specs + [kv_spec] * 4


def _qkv_call(x, wts, layer, tm, seq):
    rows, d = x.shape
    wq, wkvt, g = wts["wq"], wts["wkvt"], wts["g_mix"]
    out_shape, out_specs = _qkv_out(rows, wq.shape[2] // 2, tm, seq)
    return pl.pallas_call(
        functools.partial(_qkv_kernel, per_head_t=seq is not None), grid=(rows // tm,),
        in_specs=[_row_spec(tm, d), _resident(g.shape[1:], layer), _resident(wq.shape[1:], layer),
                  _resident(wkvt.shape[1:], layer)],
        out_specs=out_specs, out_shape=out_shape,
        compiler_params=_params("parallel"), name="qkv_proj",
    )(x, g, wq, wkvt)


TAIL_WEIGHTS = ("w_out", "g_ffn", "w_gate", "w_up", "w_down", "g_ple", "w_ple_gate", "w_ple_proj")


def _layer_call(x, a, ob, p, wts, layer, g_final, tm, seq):
    rows, d = x.shape
    last = layer == wts["w_out"].shape[0] - 1
    tail_in = [x, a, ob, p] + [wts[k] for k in TAIL_WEIGHTS]
    tail_specs = ([_row_spec(tm, d), _row_spec(tm, a.shape[1]), _row_spec(tm, ob.shape[1]),
                   pl.BlockSpec((None, tm, p.shape[2]), lambda i: (layer, i, 0))]
                  + [_resident(wts[k].shape[1:], layer) for k in TAIL_WEIGHTS])
    if last:
        return pl.pallas_call(
            _last_layer_kernel, grid=(rows // tm,), in_specs=tail_specs + [_resident(g_final.shape)],
            out_specs=_row_spec(tm, d), out_shape=jax.ShapeDtypeStruct((rows, d), F32),
            compiler_params=_params("parallel"), name="layer_tail_final",
        )(*tail_in, g_final)
    nxt = [wts["g_mix"], wts["wq"], wts["wkvt"]]
    out_shape, out_specs = _qkv_out(rows, wts["wq"].shape[2] // 2, tm, seq)
    return pl.pallas_call(
        functools.partial(_mid_layer_kernel, per_head_t=seq is not None), grid=(rows // tm,),
        in_specs=tail_specs + [_resident(t.shape[1:], layer + 1) for t in nxt],
        out_specs=[_row_spec(tm, d)] + out_specs,
        out_shape=[jax.ShapeDtypeStruct((rows, d), F32)] + out_shape,
        compiler_params=_params("parallel"), name="layer_tail_qkv",
    )(*tail_in, *nxt)


def _suffix_matrix(tk):
    j = lax.broadcasted_iota(jnp.int32, (tk, 2 * tk), 0)
    c = lax.broadcasted_iota(jnp.int32, (tk, 2 * tk), 1)
    return jnp.where((c >= tk) | (j > c), -1.0, 0.0).astype(BF16)


def _sb_weights(z, suffix, carry_ref, causal, row0=0):
    tk = suffix.shape[0]
    sp = _softplus(z)
    w_parts = []
    for h in range(z.shape[1] // tk):
        cols = slice(h * tk, (h + 1) * tk)
        z_h, sp_h = z[:, cols], sp[:, cols]
        drop = sp_h if causal is None else jnp.where(causal, sp_h, 0.0)
        sums = _dot(drop.astype(BF16), suffix)
        carry = carry_ref[row0:, cols]
        log_w = (z_h - sp_h) + (sums[:, :tk] + carry)
        w = jnp.exp(log_w)
        if causal is not None:
            w = jnp.where(causal, w, 0.0)
        carry_ref[row0:, cols] = carry + sums[:, tk:]
        w_parts.append(w.astype(BF16))
    return jnp.concatenate(w_parts, axis=1)


def _sb_kernel(q_ref, kt_ref, vt_ref, suffix_ref, g_ref, o_ref, kk_ref, vv_ref, carry_ref, acc_ref, z_ref, w_ref):
    tq, tk = SB_TQ, SB_TK
    per_q = tq // tk
    i = pl.program_id(2)

    @pl.when(i == 0)
    def _():
        kk_ref[...] = jnp.zeros_like(kk_ref)
        vv_ref[...] = jnp.zeros_like(vv_ref)
        for j in range(kk_ref.shape[0]):
            for h in range(2):
                rows, cols = slice(h * HEAD_DIM, (h + 1) * HEAD_DIM), slice(h * tk, (h + 1) * tk)
                kk_ref[j, rows, cols] = kt_ref[0, h, :, j * tk:(j + 1) * tk].astype(BF16)
                vv_ref[j, rows, cols] = vt_ref[0, h, :, j * tk:(j + 1) * tk].astype(BF16)

    carry_ref[...] = jnp.zeros_like(carry_ref)
    acc_ref[...] = jnp.zeros_like(acc_ref)
    q = q_ref[0]
    suffix = suffix_ref[...]
    q_pos = i * tq + lax.broadcasted_iota(jnp.int32, (tq, tk), 0)
    k_off = lax.broadcasted_iota(jnp.int32, (tq, tk), 1)
    newest = (i + 1) * per_q - 1

    def row0(n):
        return (per_q - 1 - n) * tk if isinstance(n, int) and 0 <= n < per_q else 0

    def step(n, slot, diagonal):
        j = newest - n
        r_prev, r_cur, r_next = row0(n - 1) if diagonal else 0, row0(n), row0(n + 1)
        z = z_ref[slot, r_cur:, :]
        z_ref[1 - slot, r_next:, :] = _dot(q[r_next:], kk_ref[jnp.maximum(j - 1, 0)])
        if not (diagonal and n == 0):
            acc_ref[r_prev:, :] += _dot_nt(w_ref[1 - slot, r_prev:, :], vv_ref[j + 1])
        causal = ((j * tk + k_off) < q_pos)[r_cur:] if diagonal else None
        w_ref[slot, r_cur:, :] = _sb_weights(z, suffix, carry_ref, causal, r_cur)

    z_ref[0, row0(0):, :] = _dot(q[row0(0):], kk_ref[newest])
    for n in range(per_q):
        step(n, n % 2, True)

    def more_past(state):
        t, live = state
        return jnp.logical_and(t < i, live)

    def past(state):
        t, _ = state
        for d in range(per_q):
            step(per_q * (t + 1) + d, d % 2, False)
        return t + 1, jnp.max(carry_ref[...]) > SB_DEAD
    trips, _ = lax.while_loop(more_past, past, (jnp.int32(0), i >= 0))

    oldest = per_q * (i - trips)
    acc_ref[...] += _dot_nt(w_ref[(per_q - 1) % 2], vv_ref[oldest])
    o_ref[0] = _head_pair_rms(acc_ref[...], g_ref[...]).astype(o_ref.dtype)


def _pair_kv_spec(s):
    return pl.BlockSpec((1, 2, HEAD_DIM, s), lambda bi, pi, i: (bi, pi, 0, 0))


def _sb_call(q, kt, vt, suffix, g, layer):
    b, s, w = q.shape
    nkb = s // SB_TK
    assert (SB_TQ // SB_TK) % 2 == 0
    return pl.pallas_call(
        _sb_kernel, grid=(b, w // LANES, s // SB_TQ),
        in_specs=[pl.BlockSpec((1, SB_TQ, LANES), lambda bi, pi, i: (bi, i, pi)), _pair_kv_spec(s), _pair_kv_spec(s),
                  _resident(suffix.shape), pl.BlockSpec((None, 1, LANES), lambda bi, pi, i: (layer, 0, pi))],
        out_specs=pl.BlockSpec((1, SB_TQ, LANES), lambda bi, pi, i: (bi, i, pi)),
        out_shape=jax.ShapeDtypeStruct((b, s, w), BF16),
        scratch_shapes=[pltpu.VMEM((nkb, LANES, 2 * SB_TK), BF16), pltpu.VMEM((nkb, LANES, 2 * SB_TK), BF16),
                        pltpu.VMEM((SB_TQ, 2 * SB_TK), F32), pltpu.VMEM((SB_TQ, LANES), F32),
                        pltpu.VMEM((2, SB_TQ, 2 * SB_TK), F32), pltpu.VMEM((2, SB_TQ, 2 * SB_TK), BF16)],
        compiler_params=_params("parallel", "parallel", "arbitrary"), name="sb_attention",
    )(q, kt, vt, suffix, g)


def _softmax_unnormalised(s):
    e = jnp.exp(s - jnp.max(s, axis=-1, keepdims=True))
    return e, jnp.sum(e, axis=-1, keepdims=True)


def _cb_kernel(q_ref, kt_ref, vt_ref, bias_ref, g_ref, o_ref, kk_ref, vv_ref):
    i = pl.program_id(2)
    lead = BAND // CB_TQ
    window = CB_TK // CB_TQ

    @pl.when(i == 0)
    def _():
        kk_ref[...] = jnp.zeros_like(kk_ref)
        vv_ref[...] = jnp.zeros_like(vv_ref)
        for h in range(2):
            rows = slice(h * HEAD_DIM, (h + 1) * HEAD_DIM)
            for c in range(kk_ref.shape[1] - lead):
                kk_ref[h, lead + c, rows, :] = kt_ref[0, h, :, c * CB_TQ:(c + 1) * CB_TQ].astype(BF16)
                vv_ref[h, lead + c, rows, :] = vt_ref[0, h, :, c * CB_TQ:(c + 1) * CB_TQ].astype(BF16)

    q = q_ref[0]
    exists = lax.broadcasted_iota(jnp.int32, (CB_TQ, CB_TK), 1) >= BAND - i * CB_TQ
    first = lax.broadcasted_iota(jnp.int32, (CB_TQ, LANES), 1) < HEAD_DIM
    o = jnp.zeros((CB_TQ, LANES), F32)
    for h in range(2):
        kt = jnp.concatenate([kk_ref[h, i + c] for c in range(window)], axis=1)
        vt = jnp.concatenate([vv_ref[h, i + c] for c in range(window)], axis=1)
        e, l = _softmax_unnormalised(jnp.where(exists, _dot(q, kt) + bias_ref[h], NEG))
        o_h = _dot_nt(e.astype(BF16), vt) / l
        o = o_h if h == 0 else jnp.where(first, o, o_h)
    o_ref[0] = _head_pair_rms(o, g_ref[...]).astype(o_ref.dtype)


def _cb_call(q, kt, vt, bias, g, layer):
    b, s, w = q.shape
    blocks = (BAND + s) // CB_TQ
    return pl.pallas_call(
        _cb_kernel, grid=(b, w // LANES, s // CB_TQ),
        in_specs=[pl.BlockSpec((1, CB_TQ, LANES), lambda bi, pi, i: (bi, i, pi)), _pair_kv_spec(s), _pair_kv_spec(s),
                  pl.BlockSpec((2, CB_TQ, CB_TK), lambda bi, pi, i: (pi, 0, 0)),
                  pl.BlockSpec((None, 1, LANES), lambda bi, pi, i: (layer, 0, pi))],
        out_specs=pl.BlockSpec((1, CB_TQ, LANES), lambda bi, pi, i: (bi, i, pi)),
        out_shape=jax.ShapeDtypeStruct((b, s, w), BF16),
        scratch_shapes=[pltpu.VMEM((2, blocks, LANES, CB_TQ), BF16), pltpu.VMEM((2, blocks, LANES, CB_TQ), BF16)],
        compiler_params=_params("parallel", "parallel", "arbitrary"), name="cb_attention",
    )(q, kt, vt, bias, g)


def _bias_tiles_kernel(table_ref, o_ref, *, heads):
    layer, head = pl.program_id(0), pl.program_id(1)
    n_rel = 2 * REL_MAX + 1
    shape = o_ref.shape[2:]
    dist = (2 * CHUNK + lax.broadcasted_iota(jnp.int32, shape, 0) - lax.broadcasted_iota(jnp.int32, shape, 1))
    idx = jnp.clip(dist, -REL_MAX, REL_MAX) + REL_MAX

    def pick(r, acc):
        return jnp.where(idx == r, table_ref[(layer * n_rel + r) * heads + head], acc)
    o_ref[0, 0] = lax.fori_loop(0, n_rel, pick, jnp.zeros(shape, F32))


def _bias_tiles(rel_table):
    n_layers, _, heads = rel_table.shape
    return pl.pallas_call(
        functools.partial(_bias_tiles_kernel, heads=heads), grid=(n_layers, heads),
        in_specs=[pl.BlockSpec(memory_space=pltpu.SMEM)],
        out_specs=pl.BlockSpec((1, 1, CHUNK, 3 * CHUNK), lambda l, h: (l, h, 0, 0)),
        out_shape=jax.ShapeDtypeStruct((n_layers, heads, CHUNK, 3 * CHUNK), F32),
        compiler_params=_params("parallel", "parallel"), name="rel_bias_tiles",
    )(rel_table.reshape(-1))


def _prompt_bias(tiles, far):
    heads = tiles.shape[0]
    qc, kc = CB_TQ // CHUNK, CB_TK // CHUNK
    far_tile = jnp.broadcast_to(far[:, None, None], (heads, CHUNK, CHUNK))
    hidden = jnp.full((heads, CHUNK, CHUNK), NEG, F32)
    rows = []
    for c in range(qc):
        row = []
        for k in range(kc):
            m = k - c
            if m < 0 or m > BAND_CHUNKS:
                row.append(hidden)
            elif m < BAND_CHUNKS - 2:
                row.append(far_tile)
            else:
                t = m - (BAND_CHUNKS - 2)
                row.append(tiles[:, :, t * CHUNK:(t + 1) * CHUNK])
        rows.append(jnp.concatenate(row, axis=2))
    return jnp.concatenate(rows, axis=1)


def _head_rows(x, heads):
    t, w = x.shape
    tiled = jnp.concatenate([x] * heads, axis=0)
    row_head = lax.div(lax.broadcasted_iota(jnp.int32, tiled.shape, 0), t)
    col_head = lax.div(lax.broadcasted_iota(jnp.int32, tiled.shape, 1), HEAD_DIM)
    return jnp.where(row_head == col_head, tiled, jnp.zeros_like(tiled))


def _fold_head_rows(o, heads):
    t = o.shape[0] // heads
    col_head = lax.div(lax.broadcasted_iota(jnp.int32, (t, o.shape[1]), 1), HEAD_DIM)
    out = jnp.zeros((t, o.shape[1]), F32)
    for h in range(heads):
        out = jnp.where(col_head == h, o[h * t:(h + 1) * t], out)
    return out


def _all_heads_rms(o, g):
    parts = [_head_pair_rms(o[:, c:c + LANES], g[:, c:c + LANES]) for c in range(0, o.shape[1], LANES)]
    return jnp.concatenate(parts, axis=1)


def _pad_rows(x, rows):
    return jnp.concatenate([x, jnp.zeros((rows - x.shape[0], x.shape[1]), x.dtype)], axis=0)


def _sample_kernel(qa_ref, ka_ref, va_ref, ckt_ref, cvt_ref, qb_ref, kb_ref, vb_ref, ckbt_ref, cvbt_ref,
                   suffix_ref, bias_ref, ga_ref, gb_ref, oa_ref, ob_ref, carry_ref, acc_ref):
    t, w = qa_ref.shape[1:]
    heads = w // HEAD_DIM
    tk = SB_TK
    past = ckt_ref.shape[2]

    carry_ref[...] = jnp.zeros_like(carry_ref)
    q = _head_rows(qa_ref[0], heads)
    suffix = suffix_ref[...]
    frame = lax.rem(lax.broadcasted_iota(jnp.int32, (heads * t, tk), 0), t)
    key = lax.broadcasted_iota(jnp.int32, (heads * t, tk), 1)
    wts = _sb_weights(_dot_nt(q, _pad_rows(ka_ref[0], tk).astype(BF16)), suffix, carry_ref, key < frame)
    acc_ref[...] = _dot(wts, _pad_rows(va_ref[0], tk).astype(BF16))
    for blk in reversed(range(past // tk)):
        cols = slice(blk * tk, (blk + 1) * tk)
        wts = _sb_weights(_dot(q, ckt_ref[0, :, cols].astype(BF16)), suffix, carry_ref, None)
        acc_ref[...] += _dot_nt(wts, cvt_ref[0, :, cols].astype(BF16))
    oa_ref[0] = _all_heads_rms(_fold_head_rows(acc_ref[...], heads), ga_ref[...]).astype(oa_ref.dtype)

    qb = _head_rows(qb_ref[0], heads)
    s = jnp.concatenate([_dot(qb, ckbt_ref[0].astype(BF16)),
                         _dot_nt(qb, _pad_rows(kb_ref[0], LANES).astype(BF16))], axis=1)
    e, l = _softmax_unnormalised(s + bias_ref[...])
    e = e.astype(BF16)
    band = ckbt_ref.shape[2]
    o = _dot_nt(e[:, :band], cvbt_ref[0].astype(BF16)) + _dot(e[:, band:], _pad_rows(vb_ref[0], LANES).astype(BF16))
    ob_ref[0] = _all_heads_rms(_fold_head_rows(o / l, heads), gb_ref[...]).astype(ob_ref.dtype)


def _sample_call(qa, ka, va, ckt, cvt, qb, kb, vb, ckbt, cvbt, suffix, bias, ga, gb, layer):
    b, t, w = qa.shape
    heads = w // HEAD_DIM
    new = pl.BlockSpec((1, t, w), lambda bi: (bi, 0, 0))

    def cache(c):
        return pl.BlockSpec((None, 1) + c.shape[2:], lambda bi: (layer, bi, 0, 0))
    return pl.pallas_call(
        _sample_kernel, grid=(b,),
        in_specs=[new, new, new, cache(ckt), cache(cvt), new, new, new, cache(ckbt), cache(cvbt),
                  _resident(suffix.shape), _resident(bias.shape), _resident(ga.shape[1:], layer),
                  _resident(gb.shape[1:], layer)],
        out_specs=[new, new], out_shape=[jax.ShapeDtypeStruct((b, t, w), BF16)] * 2,
        scratch_shapes=[pltpu.VMEM((heads * t, SB_TK), F32), pltpu.VMEM((heads * t, w), F32)],
        compiler_params=_params("parallel"), name="sample_attention",
    )(qa, ka, va, ckt, cvt, qb, kb, vb, ckbt, cvbt, suffix, bias, ga, gb)


def _sample_bias(tiles, far, t):
    heads = tiles.shape[0]
    near = BAND - 2 * CHUNK
    parts = [jnp.broadcast_to(far[:, None, None], (heads, t, near)), tiles[:, :t, :2 * CHUNK + t],
             jnp.full((heads, t, LANES - t), NEG, F32)]
    return jnp.concatenate(parts, axis=2).reshape(heads * t, BAND + LANES)


def _per_head_t(cache):
    n_l, n_b, n_p, n_h, n_d = cache.shape
    return jnp.transpose(cache, (0, 1, 3, 4, 2)).reshape(n_l, n_b, n_h * n_d, n_p)


def kernel(x_prompt, x_sample, p_prompt, p_sample, cache_sb_k, cache_sb_v, cache_cb_k, cache_cb_v, g_mix, w_in,
           rel_table, g_out_sb, g_out_cb, w_out, g_ffn, w_gate, w_up, w_down, g_ple, w_ple_gate, w_ple_proj,
           g_final):
    depth = w_in.shape[0]
    b, s, d = x_prompt.shape
    bs, t, _ = x_sample.shape
    w = g_out_sb.shape[1]
    heads = w // HEAD_DIM
    past, band = cache_sb_k.shape[2], cache_cb_k.shape[2]
    assert g_out_cb.shape[1] == w and w_in.shape[2] == 6 * w and w % LANES == 0
    assert s % SB_TQ == 0 and s % CB_TQ == 0 and s >= BAND and s % ROW_TILE == 0
    assert band == BAND and past % SB_TK == 0 and t <= CHUNK and t % 8 == 0

    gain = lambda g: g[:, None, :]
    wts = dict(
        g_mix=gain(g_mix), g_ffn=gain(g_ffn), g_ple=gain(g_ple),
        wq=jnp.concatenate([w_in[:, :, :w], w_in[:, :, 3 * w:4 * w]], axis=2).astype(BF16),
        wkvt=jnp.swapaxes(jnp.concatenate([w_in[:, :, w:3 * w], w_in[:, :, 4 * w:]], axis=2), 1, 2).astype(BF16),
        w_out=w_out.astype(BF16), w_gate=w_gate.astype(BF16), w_up=w_up.astype(BF16), w_down=w_down.astype(BF16),
        w_ple_gate=w_ple_gate.astype(BF16), w_ple_proj=w_ple_proj.astype(BF16))
    g_sb, g_cb = gain(g_out_sb), gain(g_out_cb)
    g_fin = g_final.reshape(1, d)
    suffix = _suffix_matrix(SB_TK)
    tiles = _bias_tiles(rel_table)
    far = rel_table[:, 2 * REL_MAX, :]
    caches = [_per_head_t(c) for c in (cache_sb_k, cache_sb_v, cache_cb_k, cache_cb_v)]
    pp = p_prompt.reshape(depth, b * s, -1)
    ps = p_sample.reshape(depth, bs * t, -1)

    xp = x_prompt.reshape(b * s, d)
    xs = x_sample.reshape(bs * t, d)
    qkv_p = _qkv_call(xp, wts, 0, ROW_TILE, s)
    qkv_s = _qkv_call(xs, wts, 0, bs * t, None)
    new_p, new_s = [], []
    for i in range(depth):
        qa, qb = [u.reshape(b, s, w) for u in qkv_p[:2]]
        kat, vat, kbt, vbt = qkv_p[2:]
        new_p.append((kat, vat, kbt[..., s - BAND:], vbt[..., s - BAND:]))
        a = _sb_call(qa, kat, vat, suffix, g_sb, i)
        ob = _cb_call(qb, kbt, vbt, _prompt_bias(tiles[i], far[i]), g_cb, i)

        sqa, sqb, ska, sva, skb, svb = [u.reshape(bs, t, w) for u in qkv_s]
        new_s.append((ska, sva, skb, svb))
        sa, sob = _sample_call(sqa, ska, sva, caches[0], caches[1], sqb, skb, svb, caches[2], caches[3],
                               suffix, _sample_bias(tiles[i], far[i], t), g_sb, g_cb, i)

        out_p = _layer_call(xp, a.reshape(b * s, w), ob.reshape(b * s, w), pp, wts, i, g_fin, ROW_TILE, s)
        out_s = _layer_call(xs, sa.reshape(bs * t, w), sob.reshape(bs * t, w), ps, wts, i, g_fin, bs * t, None)
        if i == depth - 1:
            xp, xs = out_p, out_s
        else:
            xp, qkv_p = out_p[0], out_p[1:]
            xs, qkv_s = out_s[0], out_s[1:]

    def stack_t(items):
        return jnp.transpose(jnp.stack(items), (0, 1, 4, 2, 3))

    def stack(items):
        return jnp.stack(items).reshape(depth, bs, t, heads, HEAD_DIM)

    return (xp.reshape(b, s, d), xs.reshape(bs, t, d),
            stack_t([e[0] for e in new_p]), stack_t([e[1] for e in new_p]),
            stack_t([e[2] for e in new_p]), stack_t([e[3] for e in new_p]),
            stack([e[0] for e in new_s]), stack([e[1] for e in new_s]),
            stack([e[2] for e in new_s]), stack([e[3] for e in new_s]))
```

```python
import functools

import jax
import jax.numpy as jnp
from jax import lax
from jax.experimental import pallas as pl
from jax.experimental.pallas import tpu as pltpu

F32 = jnp.float32
BF16 = jnp.bfloat16

HEAD_DIM = 64
CHUNK = 64
BAND_CHUNKS = 8
BAND = BAND_CHUNKS * CHUNK
REL_MAX = 128
EPS = 1e-6
Q_SCALE = HEAD_DIM ** -0.5

LANES = 128
NEG = -1e30

ROW_TILE = 256
SB_TQ = 256
SB_TK = 128
SB_DEAD = -120.0
CB_TQ = 256
CB_TK = CB_TQ + BAND
CB_GROUPS = 4
VMEM_LIMIT = 56 * 1024 * 1024


def _params(*semantics):
    return pltpu.CompilerParams(dimension_semantics=semantics, vmem_limit_bytes=VMEM_LIMIT)


def _resident(shape, layer=None):
    zeros = (0,) * len(shape)
    if layer is None:
        return pl.BlockSpec(shape, lambda *_: zeros, pipeline_mode=pl.Buffered(1))
    return pl.BlockSpec((None,) + tuple(shape), lambda *_: (layer,) + zeros, pipeline_mode=pl.Buffered(1))


def _rms_unit(x):
    return x * lax.rsqrt(jnp.mean(x * x, axis=-1, keepdims=True) + EPS)


def _dot(a, b):
    return jnp.dot(a, b, preferred_element_type=F32)


def _dot_nt(a, b):
    return lax.dot_general(a, b, (((1,), (1,)), ((), ())), preferred_element_type=F32)


def _softplus(z):
    return jnp.maximum(z, 0.0) + jnp.log(1.0 + jnp.exp(-jnp.abs(z)))


def _head_pair_rms(o, g):
    first = lax.broadcasted_iota(jnp.int32, o.shape, 1) < HEAD_DIM
    o2 = o * o
    s0 = jnp.sum(jnp.where(first, o2, 0.0), axis=-1, keepdims=True)
    s1 = jnp.sum(jnp.where(first, 0.0, o2), axis=-1, keepdims=True)
    ms = jnp.where(first, s0, s1) * (1.0 / HEAD_DIM)
    return o * lax.rsqrt(ms + EPS) * g


def _store_qkv(h, wq_ref, wkvt_ref, outs, per_head_t):
    qa_ref, qb_ref = outs[:2]
    w = qa_ref.shape[-1]
    q = _dot(h, wq_ref[...]) * Q_SCALE
    qa_ref[...] = q[:, :w].astype(BF16)
    qb_ref[...] = q[:, w:].astype(BF16)
    for c, ref in enumerate(outs[2:]):
        w_t = wkvt_ref[c * w:(c + 1) * w, :]
        if per_head_t:
            ref[0] = _dot_nt(w_t, h).reshape(ref.shape[1:])
        else:
            ref[...] = _dot_nt(h, w_t)


def _qkv_kernel(x_ref, g_ref, wq_ref, wkvt_ref, *outs, per_head_t):
    h = (_rms_unit(x_ref[...]) * g_ref[...]).astype(BF16)
    _store_qkv(h, wq_ref, wkvt_ref, outs, per_head_t)


def _layer_tail(x_ref, a_ref, ob_ref, p_ref, wo_ref, gf_ref, wg_ref, wu_ref, wd_ref, gp_ref, wpg_ref, wpp_ref):
    w = a_ref.shape[-1]
    x = x_ref[...] + _dot(a_ref[...], wo_ref[:w, :]) + _dot(ob_ref[...], wo_ref[w:, :])
    h = (_rms_unit(x) * gf_ref[...]).astype(BF16)
    gate = _dot(h, wg_ref[...])
    up = _dot(h, wu_ref[...])
    act = (gate * jax.nn.sigmoid(gate) * up).astype(BF16)
    x = x + _dot(act, wd_ref[...])
    hp = (_rms_unit(x) * gp_ref[...]).astype(BF16)
    ple_gate = jax.nn.sigmoid(_dot(hp, wpg_ref[...]))
    return x + ple_gate * _dot(p_ref[...].astype(BF16), wpp_ref[...])


N_TAIL = 12


def _mid_layer_kernel(*refs, per_head_t, n_alias):
    tail, (gn_ref, wq_ref, wkvt_ref) = refs[:N_TAIL], refs[N_TAIL:N_TAIL + 3]
    xo_ref, outs = refs[N_TAIL + 3 + n_alias], refs[N_TAIL + 4 + n_alias:]
    x = _layer_tail(*tail)
    xo_ref[...] = x
    _store_qkv((_rms_unit(x) * gn_ref[...]).astype(BF16), wq_ref, wkvt_ref, outs, per_head_t)


def _last_layer_kernel(*refs):
    tail, (gn_ref, y_ref) = refs[:N_TAIL], refs[N_TAIL:]
    y_ref[...] = _rms_unit(_layer_tail(*tail)) * gn_ref[...]


def _row_spec(tm, width):
    return pl.BlockSpec((tm, width), lambda i: (i, 0))


def _qkv_out(rows, w, tm, seq, layer, depth):
    q_shapes = [jax.ShapeDtypeStruct((rows, w), BF16)] * 2
    q_specs = [_row_spec(tm, w)] * 2
    if seq is None:
        return q_shapes + [jax.ShapeDtypeStruct((rows, w), F32)] * 4, q_specs + [_row_spec(tm, w)] * 4
    heads, tiles = w // HEAD_DIM, seq // tm
    kv_shape = jax.ShapeDtypeStruct((depth, rows // seq, heads, HEAD_DIM, seq), F32)
    kv_spec = pl.BlockSpec((None, 1, heads, HEAD_DIM, tm), lambda i: (layer, i // tiles, 0, 0, i % tiles))
    return q_shapes + [kv_shape] * 4, q_specs + [kv_spec] * 4


def _qkv_call(x, wts, layer, tm, seq):
    rows, d = x.shape
    wq, wkvt, g = wts["wq"], wts["wkvt"], wts["g_mix"]
    out_shape, out_specs = _qkv_out(rows, wq.shape[2] // 2, tm, seq, layer, wq.shape[0])
    return pl.pallas_call(
        functools.partial(_qkv_kernel, per_head_t=seq is not None), grid=(rows // tm,),
        in_specs=[_row_spec(tm, d), _resident(g.shape[1:], layer), _resident(wq.shape[1:], layer),
                  _resident(wkvt.shape[1:], layer)],
        out_specs=out_specs, out_shape=out_shape,
        compiler_params=_params("parallel"), name="qkv_proj",
    )(x, g, wq, wkvt)


TAIL_WEIGHTS = ("w_out", "g_ffn", "w_gate", "w_up", "w_down", "g_ple", "w_ple_gate", "w_ple_proj")


def _layer_call(x, a, ob, p, wts, layer, g_final, tm, seq, kv_all=()):
    rows, d = x.shape
    last = layer == wts["w_out"].shape[0] - 1
    tail_in = [x, a, ob, p] + [wts[k] for k in TAIL_WEIGHTS]
    tail_specs = ([_row_spec(tm, d), _row_spec(tm, a.shape[1]), _row_spec(tm, ob.shape[1]),
                   pl.BlockSpec((None, tm, p.shape[2]), lambda i: (layer, i, 0))]
                  + [_resident(wts[k].shape[1:], layer) for k in TAIL_WEIGHTS])
    if last:
        return pl.pallas_call(
            _last_layer_kernel, grid=(rows // tm,), in_specs=tail_specs + [_resident(g_final.shape)],
            out_specs=_row_spec(tm, d), out_shape=jax.ShapeDtypeStruct((rows, d), F32),
            compiler_params=_params("parallel"), name="layer_tail_final",
        )(*tail_in, g_final)
    nxt = [wts["g_mix"], wts["wq"], wts["wkvt"]]
    out_shape, out_specs = _qkv_out(rows, wts["wq"].shape[2] // 2, tm, seq, layer + 1, wts["wq"].shape[0])
    n_in = len(tail_in) + len(nxt)
    return pl.pallas_call(
        functools.partial(_mid_layer_kernel, per_head_t=seq is not None, n_alias=len(kv_all)), grid=(rows // tm,),
        in_specs=(tail_specs + [_resident(t.shape[1:], layer + 1) for t in nxt]
                  + [pl.BlockSpec(memory_space=pl.ANY)] * len(kv_all)),
        out_specs=[_row_spec(tm, d)] + out_specs,
        out_shape=[jax.ShapeDtypeStruct((rows, d), F32)] + out_shape,
        input_output_aliases={n_in + k: 3 + k for k in range(len(kv_all))},
        compiler_params=_params("parallel"), name="layer_tail_qkv",
    )(*tail_in, *nxt, *kv_all)


def _suffix_matrix(tk):
    j = lax.broadcasted_iota(jnp.int32, (tk, 2 * tk), 0)
    c = lax.broadcasted_iota(jnp.int32, (tk, 2 * tk), 1)
    return jnp.where((c >= tk) | (j > c), -1.0, 0.0).astype(BF16)


def _sb_weights(z, suffix, carry_ref, causal, row0=0):
    tk = suffix.shape[0]
    sp = _softplus(z)
    w_parts = []
    for h in range(z.shape[1] // tk):
        cols = slice(h * tk, (h + 1) * tk)
        z_h, sp_h = z[:, cols], sp[:, cols]
        drop = sp_h if causal is None else jnp.where(causal, sp_h, 0.0)
        sums = _dot(drop.astype(BF16), suffix)
        carry = carry_ref[row0:, cols]
        log_w = (z_h - sp_h) + (sums[:, :tk] + carry)
        w = jnp.exp(log_w)
        if causal is not None:
            w = jnp.where(causal, w, 0.0)
        carry_ref[row0:, cols] = carry + sums[:, tk:]
        w_parts.append(w.astype(BF16))
    return jnp.concatenate(w_parts, axis=1)


def _sb_kernel(q_ref, kt_ref, vt_ref, suffix_ref, g_ref, o_ref, kk_ref, vv_ref, carry_ref, acc_ref, z_ref, w_ref):
    tq, tk = SB_TQ, SB_TK
    per_q = tq // tk
    i = pl.program_id(2)

    @pl.when(i == 0)
    def _():
        kk_ref[...] = jnp.zeros_like(kk_ref)
        vv_ref[...] = jnp.zeros_like(vv_ref)
        for j in range(kk_ref.shape[0]):
            for h in range(2):
                rows, cols = slice(h * HEAD_DIM, (h + 1) * HEAD_DIM), slice(h * tk, (h + 1) * tk)
                kk_ref[j, rows, cols] = kt_ref[0, h, :, j * tk:(j + 1) * tk].astype(BF16)
                vv_ref[j, rows, cols] = vt_ref[0, h, :, j * tk:(j + 1) * tk].astype(BF16)

    carry_ref[...] = jnp.zeros_like(carry_ref)
    acc_ref[...] = jnp.zeros_like(acc_ref)
    q = q_ref[0]
    suffix = suffix_ref[...]
    q_pos = i * tq + lax.broadcasted_iota(jnp.int32, (tq, tk), 0)
    k_off = lax.broadcasted_iota(jnp.int32, (tq, tk), 1)
    newest = (i + 1) * per_q - 1

    def row0(n):
        return (per_q - 1 - n) * tk if isinstance(n, int) and 0 <= n < per_q else 0

    def step(n, slot, diagonal):
        j = newest - n
        r_prev, r_cur, r_next = row0(n - 1) if diagonal else 0, row0(n), row0(n + 1)
        z = z_ref[slot, r_cur:, :]
        z_ref[1 - slot, r_next:, :] = _dot(q[r_next:], kk_ref[jnp.maximum(j - 1, 0)])
        if not (diagonal and n == 0):
            acc_ref[r_prev:, :] += _dot_nt(w_ref[1 - slot, r_prev:, :], vv_ref[j + 1])
        causal = ((j * tk + k_off) < q_pos)[r_cur:] if diagonal else None
        w_ref[slot, r_cur:, :] = _sb_weights(z, suffix, carry_ref, causal, r_cur)

    z_ref[0, row0(0):, :] = _dot(q[row0(0):], kk_ref[newest])
    for n in range(per_q):
        step(n, n % 2, True)

    def more_past(state):
        t, live = state
        return jnp.logical_and(t < i, live)

    def past(state):
        t, _ = state
        for d in range(per_q):
            step(per_q * (t + 1) + d, d % 2, False)
        return t + 1, jnp.max(carry_ref[...]) > SB_DEAD
    trips, _ = lax.while_loop(more_past, past, (jnp.int32(0), i >= 0))

    oldest = per_q * (i - trips)
    acc_ref[...] += _dot_nt(w_ref[(per_q - 1) % 2], vv_ref[oldest])
    o_ref[0] = _head_pair_rms(acc_ref[...], g_ref[...]).astype(o_ref.dtype)


def _pair_kv_spec(s, layer):
    return pl.BlockSpec((None, 1, 2, HEAD_DIM, s), lambda bi, pi, i: (layer, bi, pi, 0, 0))


def _sb_call(q, kt, vt, suffix, g, layer):
    b, s, w = q.shape
    nkb = s // SB_TK
    assert (SB_TQ // SB_TK) % 2 == 0
    return pl.pallas_call(
        _sb_kernel, grid=(b, w // LANES, s // SB_TQ),
        in_specs=[pl.BlockSpec((1, SB_TQ, LANES), lambda bi, pi, i: (bi, i, pi)), _pair_kv_spec(s, layer), _pair_kv_spec(s, layer),
                  _resident(suffix.shape), pl.BlockSpec((None, 1, LANES), lambda bi, pi, i: (layer, 0, pi))],
        out_specs=pl.BlockSpec((1, SB_TQ, LANES), lambda bi, pi, i: (bi, i, pi)),
        out_shape=jax.ShapeDtypeStruct((b, s, w), BF16),
        scratch_shapes=[pltpu.VMEM((nkb, LANES, 2 * SB_TK), BF16), pltpu.VMEM((nkb, LANES, 2 * SB_TK), BF16),
                        pltpu.VMEM((SB_TQ, 2 * SB_TK), F32), pltpu.VMEM((SB_TQ, LANES), F32),
                        pltpu.VMEM((2, SB_TQ, 2 * SB_TK), F32), pltpu.VMEM((2, SB_TQ, 2 * SB_TK), BF16)],
        compiler_params=_params("parallel", "parallel", "arbitrary"), name="sb_attention",
    )(q, kt, vt, suffix, g)


def _softmax_unnormalised(s):
    e = jnp.exp(s - jnp.max(s, axis=-1, keepdims=True))
    return e, jnp.sum(e, axis=-1, keepdims=True)


def _cb_kernel(q_ref, qn_ref, kt_ref, vt_ref, bias_ref, g_ref, o_ref, kk_ref, vv_ref, s_ref):
    i = pl.program_id(2)
    lead = BAND // CB_TQ
    n_window = CB_TK // CB_TQ
    last_block = kk_ref.shape[1] - n_window

    def window(ref, h, block):
        return jnp.concatenate([ref[h, block + c] for c in range(n_window)], axis=1)

    def put_scores(slot, q, block):
        for h in range(2):
            s_ref[slot, h] = _dot(q, window(kk_ref, h, block))

    @pl.when(i == 0)
    def _():
        kk_ref[...] = jnp.zeros_like(kk_ref)
        vv_ref[...] = jnp.zeros_like(vv_ref)
        for h in range(2):
            rows = slice(h * HEAD_DIM, (h + 1) * HEAD_DIM)
            for c in range(kk_ref.shape[1] - lead):
                kk_ref[h, lead + c, rows, :] = kt_ref[0, h, :, c * CB_TQ:(c + 1) * CB_TQ].astype(BF16)
                vv_ref[h, lead + c, rows, :] = vt_ref[0, h, :, c * CB_TQ:(c + 1) * CB_TQ].astype(BF16)
        put_scores(0, q_ref[0, :CB_TQ, :], 0)

    key = lax.broadcasted_iota(jnp.int32, (CB_TQ, CB_TK), 1)
    first = lax.broadcasted_iota(jnp.int32, (CB_TQ, LANES), 1) < HEAD_DIM
    for g in range(CB_GROUPS):
        block = i * CB_GROUPS + g
        rows = slice(g * CB_TQ, (g + 1) * CB_TQ)
        if g + 1 < CB_GROUPS:
            put_scores(g + 1, q_ref[0, (g + 1) * CB_TQ:(g + 2) * CB_TQ, :], block + 1)
        else:
            put_scores(0, qn_ref[0], jnp.minimum(block + 1, last_block))
        exists = key >= BAND - block * CB_TQ
        o = None
        for h in range(2):
            e, l = _softmax_unnormalised(jnp.where(exists, s_ref[g, h] + bias_ref[h], NEG))
            o_h = _dot_nt(e.astype(BF16), window(vv_ref, h, block)) / l
            o = o_h if h == 0 else jnp.where(first, o, o_h)
        o_ref[0, rows, :] = _head_pair_rms(o, g_ref[...]).astype(o_ref.dtype)


def _cb_call(q, kt, vt, bias, g, layer):
    b, s, w = q.shape
    blocks = (BAND + s) // CB_TQ
    step_rows = CB_GROUPS * CB_TQ
    n_q = s // CB_TQ
    return pl.pallas_call(
        _cb_kernel, grid=(b, w // LANES, s // step_rows),
        in_specs=[pl.BlockSpec((1, step_rows, LANES), lambda bi, pi, i: (bi, i, pi)),
                  pl.BlockSpec((1, CB_TQ, LANES), lambda bi, pi, i: (bi, jnp.minimum((i + 1) * CB_GROUPS, n_q - 1), pi)),
                  _pair_kv_spec(s, layer), _pair_kv_spec(s, layer),
                  pl.BlockSpec((2, CB_TQ, CB_TK), lambda bi, pi, i: (pi, 0, 0)),
                  pl.BlockSpec((None, 1, LANES), lambda bi, pi, i: (layer, 0, pi))],
        out_specs=pl.BlockSpec((1, step_rows, LANES), lambda bi, pi, i: (bi, i, pi)),
        out_shape=jax.ShapeDtypeStruct((b, s, w), BF16),
        scratch_shapes=[pltpu.VMEM((2, blocks, LANES, CB_TQ), BF16), pltpu.VMEM((2, blocks, LANES, CB_TQ), BF16),
                        pltpu.VMEM((CB_GROUPS, 2, CB_TQ, CB_TK), F32)],
        compiler_params=_params("parallel", "parallel", "arbitrary"), name="cb_attention",
    )(q, q, kt, vt, bias, g)


def _bias_tiles_kernel(table_ref, o_ref, *, heads):
    layer, head = pl.program_id(0), pl.program_id(1)
    n_rel = 2 * REL_MAX + 1
    shape = o_ref.shape[2:]
    dist = (2 * CHUNK + lax.broadcasted_iota(jnp.int32, shape, 0) - lax.broadcasted_iota(jnp.int32, shape, 1))
    idx = jnp.clip(dist, -REL_MAX, REL_MAX) + REL_MAX

    def pick(r, acc):
        return jnp.where(idx == r, table_ref[(layer * n_rel + r) * heads + head], acc)
    lowest = REL_MAX + 2 * CHUNK - (shape[1] - 1)
    o_ref[0, 0] = lax.fori_loop(max(lowest, 0), n_rel, pick, jnp.zeros(shape, F32), unroll=8)


def _bias_tiles(rel_table):
    n_layers, _, heads = rel_table.shape
    return pl.pallas_call(
        functools.partial(_bias_tiles_kernel, heads=heads), grid=(n_layers, heads),
        in_specs=[pl.BlockSpec(memory_space=pltpu.SMEM)],
        out_specs=pl.BlockSpec((1, 1, CHUNK, 3 * CHUNK), lambda l, h: (l, h, 0, 0)),
        out_shape=jax.ShapeDtypeStruct((n_layers, heads, CHUNK, 3 * CHUNK), F32),
        compiler_params=_params("parallel", "parallel"), name="rel_bias_tiles",
    )(rel_table.reshape(-1))


def _prompt_bias(tiles, far):
    heads = tiles.shape[0]
    qc, kc = CB_TQ // CHUNK, CB_TK // CHUNK
    far_tile = jnp.broadcast_to(far[:, None, None], (heads, CHUNK, CHUNK))
    hidden = jnp.full((heads, CHUNK, CHUNK), NEG, F32)
    rows = []
    for c in range(qc):
        row = []
        for k in range(kc):
            m = k - c
            if m < 0 or m > BAND_CHUNKS:
                row.append(hidden)
            elif m < BAND_CHUNKS - 2:
                row.append(far_tile)
            else:
                t = m - (BAND_CHUNKS - 2)
                row.append(tiles[:, :, t * CHUNK:(t + 1) * CHUNK])
        rows.append(jnp.concatenate(row, axis=2))
    return jnp.concatenate(rows, axis=1)


def _head_rows(x, heads):
    t, w = x.shape
    tiled = jnp.concatenate([x] * heads, axis=0)
    row_head = lax.div(lax.broadcasted_iota(jnp.int32, tiled.shape, 0), t)
    col_head = lax.div(lax.broadcasted_iota(jnp.int32, tiled.shape, 1), HEAD_DIM)
    return jnp.where(row_head == col_head, tiled, jnp.zeros_like(tiled))


def _fold_head_rows(o, heads):
    t = o.shape[0] // heads
    col_head = lax.div(lax.broadcasted_iota(jnp.int32, (t, o.shape[1]), 1), HEAD_DIM)
    out = jnp.zeros((t, o.shape[1]), F32)
    for h in range(heads):
        out = jnp.where(col_head == h, o[h * t:(h + 1) * t], out)
    return out


def _all_heads_rms(o, g):
    parts = [_head_pair_rms(o[:, c:c + LANES], g[:, c:c + LANES]) for c in range(0, o.shape[1], LANES)]
    return jnp.concatenate(parts, axis=1)


def _pad_rows(x, rows):
    return jnp.concatenate([x, jnp.zeros((rows - x.shape[0], x.shape[1]), x.dtype)], axis=0)


def _sample_kernel(qa_ref, ka_ref, va_ref, ckt_ref, cvt_ref, qb_ref, kb_ref, vb_ref, ckbt_ref, cvbt_ref,
                   suffix_ref, bias_ref, ga_ref, gb_ref, oa_ref, ob_ref, carry_ref, acc_ref, live_ref):
    t, w = qa_ref.shape[1:]
    heads = w // HEAD_DIM
    tk = SB_TK
    past = ckt_ref.shape[2]

    carry_ref[...] = jnp.zeros_like(carry_ref)
    q = _head_rows(qa_ref[0], heads)
    suffix = suffix_ref[...]
    frame = lax.rem(lax.broadcasted_iota(jnp.int32, (heads * t, tk), 0), t)
    key = lax.broadcasted_iota(jnp.int32, (heads * t, tk), 1)
    wts = _sb_weights(_dot_nt(q, _pad_rows(ka_ref[0], tk).astype(BF16)), suffix, carry_ref, key < frame)
    acc_ref[...] = _dot(wts, _pad_rows(va_ref[0], tk).astype(BF16))
    def cached(blk):
        cols = slice(blk * tk, (blk + 1) * tk)
        wts = _sb_weights(_dot(q, ckt_ref[0, :, cols].astype(BF16)), suffix, carry_ref, None)
        acc_ref[...] += _dot_nt(wts, cvt_ref[0, :, cols].astype(BF16))

    live_ref[0] = jnp.int32(1)
    newest_first = list(reversed(range(past // tk)))
    for first in range(0, len(newest_first), 2):
        @pl.when(live_ref[0] != 0)
        def _(first=first):
            for blk in newest_first[first:first + 2]:
                cached(blk)
            live_ref[0] = (jnp.max(carry_ref[...]) > SB_DEAD).astype(jnp.int32)
    oa_ref[0] = _all_heads_rms(_fold_head_rows(acc_ref[...], heads), ga_ref[...]).astype(oa_ref.dtype)

    qb = _head_rows(qb_ref[0], heads)
    s = jnp.concatenate([_dot(qb, ckbt_ref[0].astype(BF16)),
                         _dot_nt(qb, _pad_rows(kb_ref[0], LANES).astype(BF16))], axis=1)
    e, l = _softmax_unnormalised(s + bias_ref[...])
    e = e.astype(BF16)
    band = ckbt_ref.shape[2]
    o = _dot_nt(e[:, :band], cvbt_ref[0].astype(BF16)) + _dot(e[:, band:], _pad_rows(vb_ref[0], LANES).astype(BF16))
    ob_ref[0] = _all_heads_rms(_fold_head_rows(o / l, heads), gb_ref[...]).astype(ob_ref.dtype)


def _sample_call(qa, ka, va, ckt, cvt, qb, kb, vb, ckbt, cvbt, suffix, bias, ga, gb, layer):
    b, t, w = qa.shape
    heads = w // HEAD_DIM
    new = pl.BlockSpec((1, t, w), lambda bi: (bi, 0, 0))

    def cache(c):
        return pl.BlockSpec((None, 1) + c.shape[2:], lambda bi: (layer, bi, 0, 0))
    return pl.pallas_call(
        _sample_kernel, grid=(b,),
        in_specs=[new, new, new, cache(ckt), cache(cvt), new, new, new, cache(ckbt), cache(cvbt),
                  _resident(suffix.shape), _resident(bias.shape), _resident(ga.shape[1:], layer),
                  _resident(gb.shape[1:], layer)],
        out_specs=[new, new], out_shape=[jax.ShapeDtypeStruct((b, t, w), BF16)] * 2,
        scratch_shapes=[pltpu.VMEM((heads * t, SB_TK), F32), pltpu.VMEM((heads * t, w), F32),
                        pltpu.SMEM((1,), jnp.int32)],
        compiler_params=_params("parallel"), name="sample_attention",
    )(qa, ka, va, ckt, cvt, qb, kb, vb, ckbt, cvbt, suffix, bias, ga, gb)


def _sample_bias(tiles, far, t):
    heads = tiles.shape[0]
    near = BAND - 2 * CHUNK
    parts = [jnp.broadcast_to(far[:, None, None], (heads, t, near)), tiles[:, :t, :2 * CHUNK + t],
             jnp.full((heads, t, LANES - t), NEG, F32)]
    return jnp.concatenate(parts, axis=2).reshape(heads * t, BAND + LANES)


def _per_head_t(cache):
    n_l, n_b, n_p, n_h, n_d = cache.shape
    return jnp.transpose(cache, (0, 1, 3, 4, 2)).reshape(n_l, n_b, n_h * n_d, n_p)


def kernel(x_prompt, x_sample, p_prompt, p_sample, cache_sb_k, cache_sb_v, cache_cb_k, cache_cb_v, g_mix, w_in,
           rel_table, g_out_sb, g_out_cb, w_out, g_ffn, w_gate, w_up, w_down, g_ple, w_ple_gate, w_ple_proj,
           g_final):
    depth = w_in.shape[0]
    b, s, d = x_prompt.shape
    bs, t, _ = x_sample.shape
    w = g_out_sb.shape[1]
    heads = w // HEAD_DIM
    past, band = cache_sb_k.shape[2], cache_cb_k.shape[2]
    assert g_out_cb.shape[1] == w and w_in.shape[2] == 6 * w and w % LANES == 0
    assert s % SB_TQ == 0 and s % (CB_GROUPS * CB_TQ) == 0 and s >= BAND and s % ROW_TILE == 0
    assert band == BAND and past % SB_TK == 0 and t <= CHUNK and t % 8 == 0

    gain = lambda g: g[:, None, :]
    wts = dict(
        g_mix=gain(g_mix), g_ffn=gain(g_ffn), g_ple=gain(g_ple),
        wq=jnp.concatenate([w_in[:, :, :w], w_in[:, :, 3 * w:4 * w]], axis=2).astype(BF16),
        wkvt=jnp.swapaxes(jnp.concatenate([w_in[:, :, w:3 * w], w_in[:, :, 4 * w:]], axis=2), 1, 2).astype(BF16),
        w_out=w_out.astype(BF16), w_gate=w_gate.astype(BF16), w_up=w_up.astype(BF16), w_down=w_down.astype(BF16),
        w_ple_gate=w_ple_gate.astype(BF16), w_ple_proj=w_ple_proj.astype(BF16))
    g_sb, g_cb = gain(g_out_sb), gain(g_out_cb)
    g_fin = g_final.reshape(1, d)
    suffix = _suffix_matrix(SB_TK)
    tiles = _bias_tiles(rel_table)
    far = rel_table[:, 2 * REL_MAX, :]
    caches = [_per_head_t(c) for c in (cache_sb_k, cache_sb_v, cache_cb_k, cache_cb_v)]
    pp = p_prompt.reshape(depth, b * s, -1)
    ps = p_sample.reshape(depth, bs * t, -1)

    xp = x_prompt.reshape(b * s, d)
    xs = x_sample.reshape(bs * t, d)
    qkv_p = _qkv_call(xp, wts, 0, ROW_TILE, s)
    qkv_s = _qkv_call(xs, wts, 0, bs * t, None)
    new_s = []
    for i in range(depth):
        qa, qb = [u.reshape(b, s, w) for u in qkv_p[:2]]
        kat, vat, kbt, vbt = kv_all = qkv_p[2:]
        a = _sb_call(qa, kat, vat, suffix, g_sb, i)
        ob = _cb_call(qb, kbt, vbt, _prompt_bias(tiles[i], far[i]), g_cb, i)

        sqa, sqb, ska, sva, skb, svb = [u.reshape(bs, t, w) for u in qkv_s]
        new_s.append((ska, sva, skb, svb))
        sa, sob = _sample_call(sqa, ska, sva, caches[0], caches[1], sqb, skb, svb, caches[2], caches[3],
                               suffix, _sample_bias(tiles[i], far[i], t), g_sb, g_cb, i)

        out_p = _layer_call(xp, a.reshape(b * s, w), ob.reshape(b * s, w), pp, wts, i, g_fin, ROW_TILE, s, kv_all)
        out_s = _layer_call(xs, sa.reshape(bs * t, w), sob.reshape(bs * t, w), ps, wts, i, g_fin, bs * t, None)
        if i == depth - 1:
            xp, xs = out_p, out_s
        else:
            xp, qkv_p = out_p[0], out_p[1:]
            xs, qkv_s = out_s[0], out_s[1:]

    def positions_major(kv_t):
        return jnp.transpose(kv_t, (0, 1, 4, 2, 3))

    def stack(items):
        return jnp.stack(items).reshape(depth, bs, t, heads, HEAD_DIM)

    return (xp.reshape(b, s, d), xs.reshape(bs, t, d),
            positions_major(kat), positions_major(vat),
            positions_major(kbt[..., s - BAND:]), positions_major(vbt[..., s - BAND:]),
            stack([e[0] for e in new_s]), stack([e[1] for e in new_s]),
            stack([e[2] for e in new_s]), stack([e[3] for e in new_s]))
```

```python
import functools

import jax
import jax.numpy as jnp
from jax import lax
from jax.experimental import pallas as pl
from jax.experimental.pallas import tpu as pltpu

F32 = jnp.float32
BF16 = jnp.bfloat16

HEAD_DIM = 64
CHUNK = 64
BAND_CHUNKS = 8
BAND = BAND_CHUNKS * CHUNK
REL_MAX = 128
EPS = 1e-6
Q_SCALE = HEAD_DIM ** -0.5

LANES = 128
NEG = -1e30

ROW_TILE = 256
SB_TQ = 256
SB_TK = 128
SB_DEAD = -120.0
CB_TQ = 256
CB_TK = CB_TQ + BAND
CB_GROUPS = 4
VMEM_LIMIT = 56 * 1024 * 1024


def _params(*semantics):
    return pltpu.CompilerParams(dimension_semantics=semantics, vmem_limit_bytes=VMEM_LIMIT)


def _resident(shape, layer=None):
    zeros = (0,) * len(shape)
    if layer is None:
        return pl.BlockSpec(shape, lambda *_: zeros, pipeline_mode=pl.Buffered(1))
    return pl.BlockSpec((None,) + tuple(shape), lambda *_: (layer,) + zeros, pipeline_mode=pl.Buffered(1))


def _rms_unit(x):
    return x * lax.rsqrt(jnp.mean(x * x, axis=-1, keepdims=True) + EPS)


def _dot(a, b):
    return jnp.dot(a, b, preferred_element_type=F32)


def _dot_nt(a, b):
    return lax.dot_general(a, b, (((1,), (1,)), ((), ())), preferred_element_type=F32)


def _softplus(z):
    return jnp.maximum(z, 0.0) + jnp.log(1.0 + jnp.exp(-jnp.abs(z)))


def _head_pair_rms(o, g):
    first = lax.broadcasted_iota(jnp.int32, o.shape, 1) < HEAD_DIM
    o2 = o * o
    s0 = jnp.sum(jnp.where(first, o2, 0.0), axis=-1, keepdims=True)
    s1 = jnp.sum(jnp.where(first, 0.0, o2), axis=-1, keepdims=True)
    ms = jnp.where(first, s0, s1) * (1.0 / HEAD_DIM)
    return o * lax.rsqrt(ms + EPS) * g


def _store_qkv(h, wq_ref, wkvt_ref, outs, per_head_t):
    qa_ref, qb_ref = outs[:2]
    w = qa_ref.shape[-1]
    q = _dot(h, wq_ref[...]) * Q_SCALE
    qa_ref[...] = q[:, :w].astype(BF16)
    qb_ref[...] = q[:, w:].astype(BF16)
    for c, ref in enumerate(outs[2:]):
        w_t = wkvt_ref[c * w:(c + 1) * w, :]
        if per_head_t:
            ref[0] = _dot_nt(w_t, h).reshape(ref.shape[1:])
        else:
            ref[...] = _dot_nt(h, w_t)


def _qkv_kernel(x_ref, g_ref, wq_ref, wkvt_ref, *outs, per_head_t):
    h = (_rms_unit(x_ref[...]) * g_ref[...]).astype(BF16)
    _store_qkv(h, wq_ref, wkvt_ref, outs, per_head_t)


def _layer_tail(x_ref, a_ref, ob_ref, p_ref, wo_ref, gf_ref, wg_ref, wu_ref, wd_ref, gp_ref, wpg_ref, wpp_ref):
    w = a_ref.shape[-1]
    x = x_ref[...] + _dot(a_ref[...], wo_ref[:w, :]) + _dot(ob_ref[...], wo_ref[w:, :])
    h = (_rms_unit(x) * gf_ref[...]).astype(BF16)
    gate = _dot(h, wg_ref[...])
    up = _dot(h, wu_ref[...])
    act = (gate * jax.nn.sigmoid(gate) * up).astype(BF16)
    x = x + _dot(act, wd_ref[...])
    hp = (_rms_unit(x) * gp_ref[...]).astype(BF16)
    ple_gate = jax.nn.sigmoid(_dot(hp, wpg_ref[...]))
    return x + ple_gate * _dot(p_ref[...].astype(BF16), wpp_ref[...])


N_TAIL = 12


def _mid_layer_kernel(*refs, per_head_t, n_alias):
    tail, (gn_ref, wq_ref, wkvt_ref) = refs[:N_TAIL], refs[N_TAIL:N_TAIL + 3]
    xo_ref, outs = refs[N_TAIL + 3 + n_alias], refs[N_TAIL + 4 + n_alias:]
    x = _layer_tail(*tail)
    xo_ref[...] = x
    _store_qkv((_rms_unit(x) * gn_ref[...]).astype(BF16), wq_ref, wkvt_ref, outs, per_head_t)


def _last_layer_kernel(*refs):
    tail, (gn_ref, y_ref) = refs[:N_TAIL], refs[N_TAIL:]
    y_ref[...] = _rms_unit(_layer_tail(*tail)) * gn_ref[...]


def _row_spec(tm, width):
    return pl.BlockSpec((tm, width), lambda i: (i, 0))


def _qkv_out(rows, w, tm, seq, layer, depth):
    q_shapes = [jax.ShapeDtypeStruct((rows, w), BF16)] * 2
    q_specs = [_row_spec(tm, w)] * 2
    if seq is None:
        return q_shapes + [jax.ShapeDtypeStruct((rows, w), F32)] * 4, q_specs + [_row_spec(tm, w)] * 4
    heads, tiles = w // HEAD_DIM, seq // tm
    kv_shape = jax.ShapeDtypeStruct((depth, rows // seq, heads, HEAD_DIM, seq), F32)
    kv_spec = pl.BlockSpec((None, 1, heads, HEAD_DIM, tm), lambda i: (layer, i // tiles, 0, 0, i % tiles))
    return q_shapes + [kv_shape] * 4, q_specs + [kv_spec] * 4


def _qkv_call(x, wts, layer, tm, seq):
    rows, d = x.shape
    wq, wkvt, g = wts["wq"], wts["wkvt"], wts["g_mix"]
    out_shape, out_specs = _qkv_out(rows, wq.shape[2] // 2, tm, seq, layer, wq.shape[0])
    return pl.pallas_call(
        functools.partial(_qkv_kernel, per_head_t=seq is not None), grid=(rows // tm,),
        in_specs=[_row_spec(tm, d), _resident(g.shape[1:], layer), _resident(wq.shape[1:], layer),
                  _resident(wkvt.shape[1:], layer)],
        out_specs=out_specs, out_shape=out_shape,
        compiler_params=_params("parallel"), name="qkv_proj",
    )(x, g, wq, wkvt)


TAIL_WEIGHTS = ("w_out", "g_ffn", "w_gate", "w_up", "w_down", "g_ple", "w_ple_gate", "w_ple_proj")


def _layer_call(x, a, ob, p, wts, layer, g_final, tm, seq, kv_all=()):
    rows, d = x.shape
    last = layer == wts["w_out"].shape[0] - 1
    tail_in = [x, a, ob, p] + [wts[k] for k in TAIL_WEIGHTS]
    tail_specs = ([_row_spec(tm, d), _row_spec(tm, a.shape[1]), _row_spec(tm, ob.shape[1]),
                   pl.BlockSpec((None, tm, p.shape[2]), lambda i: (layer, i, 0))]
                  + [_resident(wts[k].shape[1:], layer) for k in TAIL_WEIGHTS])
    if last:
        return pl.pallas_call(
            _last_layer_kernel, grid=(rows // tm,), in_specs=tail_specs + [_resident(g_final.shape)],
            out_specs=_row_spec(tm, d), out_shape=jax.ShapeDtypeStruct((rows, d), F32),
            compiler_params=_params("parallel"), name="layer_tail_final",
        )(*tail_in, g_final)
    nxt = [wts["g_mix"], wts["wq"], wts["wkvt"]]
    out_shape, out_specs = _qkv_out(rows, wts["wq"].shape[2] // 2, tm, seq, layer + 1, wts["wq"].shape[0])
    n_in = len(tail_in) + len(nxt)
    return pl.pallas_call(
        functools.partial(_mid_layer_kernel, per_head_t=seq is not None, n_alias=len(kv_all)), grid=(rows // tm,),
        in_specs=(tail_specs + [_resident(t.shape[1:], layer + 1) for t in nxt]
                  + [pl.BlockSpec(memory_space=pl.ANY)] * len(kv_all)),
        out_specs=[_row_spec(tm, d)] + out_specs,
        out_shape=[jax.ShapeDtypeStruct((rows, d), F32)] + out_shape,
        input_output_aliases={n_in + k: 3 + k for k in range(len(kv_all))},
        compiler_params=_params("parallel"), name="layer_tail_qkv",
    )(*tail_in, *nxt, *kv_all)


def _suffix_matrix(tk):
    j = lax.broadcasted_iota(jnp.int32, (tk, 2 * tk), 0)
    c = lax.broadcasted_iota(jnp.int32, (tk, 2 * tk), 1)
    return jnp.where((c >= tk) | (j > c), -1.0, 0.0).astype(BF16)


def _sb_weights(z, suffix, carry_ref, causal, row0=0):
    tk = suffix.shape[0]
    sp = _softplus(z)
    w_parts = []
    for h in range(z.shape[1] // tk):
        cols = slice(h * tk, (h + 1) * tk)
        z_h, sp_h = z[:, cols], sp[:, cols]
        drop = sp_h if causal is None else jnp.where(causal, sp_h, 0.0)
        sums = _dot(drop.astype(BF16), suffix)
        carry = carry_ref[row0:, cols]
        log_w = (z_h - sp_h) + (sums[:, :tk] + carry)
        w = jnp.exp(log_w)
        if causal is not None:
            w = jnp.where(causal, w, 0.0)
        carry_ref[row0:, cols] = carry + sums[:, tk:]
        w_parts.append(w.astype(BF16))
    return jnp.concatenate(w_parts, axis=1)


def _sb_kernel(q_ref, kt_ref, vt_ref, suffix_ref, g_ref, o_ref, kk_ref, vv_ref, carry_ref, acc_ref, z_ref, w_ref):
    tq, tk = SB_TQ, SB_TK
    per_q = tq // tk
    i = pl.program_id(2)

    @pl.when(i == 0)
    def _():
        kk_ref[...] = jnp.zeros_like(kk_ref)
        vv_ref[...] = jnp.zeros_like(vv_ref)
        for j in range(kk_ref.shape[0]):
            for h in range(2):
                rows, cols = slice(h * HEAD_DIM, (h + 1) * HEAD_DIM), slice(h * tk, (h + 1) * tk)
                kk_ref[j, rows, cols] = kt_ref[0, h, :, j * tk:(j + 1) * tk].astype(BF16)
                vv_ref[j, rows, cols] = vt_ref[0, h, :, j * tk:(j + 1) * tk].astype(BF16)

    carry_ref[...] = jnp.zeros_like(carry_ref)
    acc_ref[...] = jnp.zeros_like(acc_ref)
    q = q_ref[0]
    suffix = suffix_ref[...]
    q_pos = i * tq + lax.broadcasted_iota(jnp.int32, (tq, tk), 0)
    k_off = lax.broadcasted_iota(jnp.int32, (tq, tk), 1)
    newest = (i + 1) * per_q - 1

    def row0(n):
        return (per_q - 1 - n) * tk if isinstance(n, int) and 0 <= n < per_q else 0

    def step(n, slot, diagonal, with_next=True):
        j = newest - n
        r_prev, r_cur, r_next = row0(n - 1) if diagonal else 0, row0(n), row0(n + 1)
        z = z_ref[slot, r_cur:, :]
        if with_next:
            z_ref[1 - slot, r_next:, :] = _dot(q[r_next:], kk_ref[jnp.maximum(j - 1, 0)])
        if not (diagonal and n == 0):
            acc_ref[r_prev:, :] += _dot_nt(w_ref[1 - slot, r_prev:, :], vv_ref[j + 1])
        causal = ((j * tk + k_off) < q_pos)[r_cur:] if diagonal else None
        w_ref[slot, r_cur:, :] = _sb_weights(z, suffix, carry_ref, causal, r_cur)

    last_slot = (per_q - 1) % 2

    def emit(oldest):
        o = acc_ref[...] + _dot_nt(w_ref[last_slot], vv_ref[oldest])
        o_ref[0] = _head_pair_rms(o, g_ref[...]).astype(o_ref.dtype)

    def first_steps(n_steps):
        z_ref[0, row0(0):, :] = _dot(q[row0(0):], kk_ref[newest])
        for n in range(n_steps):
            step(n, n % 2, n < per_q, with_next=n + 1 < n_steps)
        emit(newest - (n_steps - 1))

    @pl.when(i == 0)
    def _():
        first_steps(per_q)

    @pl.when(i > 0)
    def _():
        first_steps(2 * per_q)
        live = jnp.max(carry_ref[...]) > SB_DEAD

        @pl.when(jnp.logical_and(live, i > 1))
        def _():
            z_ref[0] = _dot(q, kk_ref[newest - 2 * per_q])

            def more_past(state):
                t, live = state
                return jnp.logical_and(t < i, live)

            def past(state):
                t, _ = state
                for d in range(per_q):
                    step(per_q * (t + 1) + d, d % 2, False)
                return t + 1, jnp.max(carry_ref[...]) > SB_DEAD
            trips, _ = lax.while_loop(more_past, past, (jnp.int32(1), i > 0))
            emit(per_q * (i - trips))


def _pair_kv_spec(s, layer):
    return pl.BlockSpec((None, 1, 2, HEAD_DIM, s), lambda bi, pi, i: (layer, bi, pi, 0, 0))


def _sb_call(q, kt, vt, suffix, g, layer):
    b, s, w = q.shape
    nkb = s // SB_TK
    assert (SB_TQ // SB_TK) % 2 == 0
    return pl.pallas_call(
        _sb_kernel, grid=(b, w // LANES, s // SB_TQ),
        in_specs=[pl.BlockSpec((1, SB_TQ, LANES), lambda bi, pi, i: (bi, i, pi)), _pair_kv_spec(s, layer), _pair_kv_spec(s, layer),
                  _resident(suffix.shape), pl.BlockSpec((None, 1, LANES), lambda bi, pi, i: (layer, 0, pi))],
        out_specs=pl.BlockSpec((1, SB_TQ, LANES), lambda bi, pi, i: (bi, i, pi)),
        out_shape=jax.ShapeDtypeStruct((b, s, w), BF16),
        scratch_shapes=[pltpu.VMEM((nkb, LANES, 2 * SB_TK), BF16), pltpu.VMEM((nkb, LANES, 2 * SB_TK), BF16),
                        pltpu.VMEM((SB_TQ, 2 * SB_TK), F32), pltpu.VMEM((SB_TQ, LANES), F32),
                        pltpu.VMEM((2, SB_TQ, 2 * SB_TK), F32), pltpu.VMEM((2, SB_TQ, 2 * SB_TK), BF16)],
        compiler_params=_params("parallel", "parallel", "arbitrary"), name="sb_attention",
    )(q, kt, vt, suffix, g)


def _softmax_unnormalised(s):
    e = jnp.exp(s - jnp.max(s, axis=-1, keepdims=True))
    return e, jnp.sum(e, axis=-1, keepdims=True)


def _cb_kernel(q_ref, qn_ref, kt_ref, vt_ref, bias_ref, g_ref, o_ref, kk_ref, vv_ref, s_ref):
    i = pl.program_id(2)
    lead = BAND // CB_TQ
    n_window = CB_TK // CB_TQ
    last_block = kk_ref.shape[1] - n_window

    def window(ref, h, block):
        return jnp.concatenate([ref[h, block + c] for c in range(n_window)], axis=1)

    def put_scores(slot, q, block):
        for h in range(2):
            s_ref[slot, h] = _dot(q, window(kk_ref, h, block))

    @pl.when(i == 0)
    def _():
        kk_ref[...] = jnp.zeros_like(kk_ref)
        vv_ref[...] = jnp.zeros_like(vv_ref)
        for h in range(2):
            rows = slice(h * HEAD_DIM, (h + 1) * HEAD_DIM)
            for c in range(kk_ref.shape[1] - lead):
                kk_ref[h, lead + c, rows, :] = kt_ref[0, h, :, c * CB_TQ:(c + 1) * CB_TQ].astype(BF16)
                vv_ref[h, lead + c, rows, :] = vt_ref[0, h, :, c * CB_TQ:(c + 1) * CB_TQ].astype(BF16)
        put_scores(0, q_ref[0, :CB_TQ, :], 0)

    key = lax.broadcasted_iota(jnp.int32, (CB_TQ, CB_TK), 1)
    first = lax.broadcasted_iota(jnp.int32, (CB_TQ, LANES), 1) < HEAD_DIM
    for g in range(CB_GROUPS):
        block = i * CB_GROUPS + g
        rows = slice(g * CB_TQ, (g + 1) * CB_TQ)
        if g + 1 < CB_GROUPS:
            put_scores(g + 1, q_ref[0, (g + 1) * CB_TQ:(g + 2) * CB_TQ, :], block + 1)
        else:
            put_scores(0, qn_ref[0], jnp.minimum(block + 1, last_block))
        exists = key >= BAND - block * CB_TQ
        o = None
        for h in range(2):
            e, l = _softmax_unnormalised(jnp.where(exists, s_ref[g, h] + bias_ref[h], NEG))
            o_h = _dot_nt(e.astype(BF16), window(vv_ref, h, block)) / l
            o = o_h if h == 0 else jnp.where(first, o, o_h)
        o_ref[0, rows, :] = _head_pair_rms(o, g_ref[...]).astype(o_ref.dtype)


def _cb_call(q, kt, vt, bias, g, layer):
    b, s, w = q.shape
    blocks = (BAND + s) // CB_TQ
    step_rows = CB_GROUPS * CB_TQ
    n_q = s // CB_TQ
    return pl.pallas_call(
        _cb_kernel, grid=(b, w // LANES, s // step_rows),
        in_specs=[pl.BlockSpec((1, step_rows, LANES), lambda bi, pi, i: (bi, i, pi)),
                  pl.BlockSpec((1, CB_TQ, LANES), lambda bi, pi, i: (bi, jnp.minimum((i + 1) * CB_GROUPS, n_q - 1), pi)),
                  _pair_kv_spec(s, layer), _pair_kv_spec(s, layer),
                  pl.BlockSpec((2, CB_TQ, CB_TK), lambda bi, pi, i: (pi, 0, 0)),
                  pl.BlockSpec((None, 1, LANES), lambda bi, pi, i: (layer, 0, pi))],
        out_specs=pl.BlockSpec((1, step_rows, LANES), lambda bi, pi, i: (bi, i, pi)),
        out_shape=jax.ShapeDtypeStruct((b, s, w), BF16),
        scratch_shapes=[pltpu.VMEM((2, blocks, LANES, CB_TQ), BF16), pltpu.VMEM((2, blocks, LANES, CB_TQ), BF16),
                        pltpu.VMEM((CB_GROUPS, 2, CB_TQ, CB_TK), F32)],
        compiler_params=_params("parallel", "parallel", "arbitrary"), name="cb_attention",
    )(q, q, kt, vt, bias, g)


def _bias_tiles_kernel(table_ref, o_ref, *, heads):
    layer, head = pl.program_id(0), pl.program_id(1)
    n_rel = 2 * REL_MAX + 1
    shape = o_ref.shape[2:]
    dist = (2 * CHUNK + lax.broadcasted_iota(jnp.int32, shape, 0) - lax.broadcasted_iota(jnp.int32, shape, 1))
    idx = jnp.clip(dist, -REL_MAX, REL_MAX) + REL_MAX

    def pick(r, acc):
        return jnp.where(idx == r, table_ref[(layer * n_rel + r) * heads + head], acc)
    lowest = REL_MAX + 2 * CHUNK - (shape[1] - 1)
    o_ref[0, 0] = lax.fori_loop(max(lowest, 0), n_rel, pick, jnp.zeros(shape, F32), unroll=8)


def _bias_tiles(rel_table):
    n_layers, _, heads = rel_table.shape
    return pl.pallas_call(
        functools.partial(_bias_tiles_kernel, heads=heads), grid=(n_layers, heads),
        in_specs=[pl.BlockSpec(memory_space=pltpu.SMEM)],
        out_specs=pl.BlockSpec((1, 1, CHUNK, 3 * CHUNK), lambda l, h: (l, h, 0, 0)),
        out_shape=jax.ShapeDtypeStruct((n_layers, heads, CHUNK, 3 * CHUNK), F32),
        compiler_params=_params("parallel", "parallel"), name="rel_bias_tiles",
    )(rel_table.reshape(-1))


def _prompt_bias(tiles, far):
    heads = tiles.shape[0]
    qc, kc = CB_TQ // CHUNK, CB_TK // CHUNK
    far_tile = jnp.broadcast_to(far[:, None, None], (heads, CHUNK, CHUNK))
    hidden = jnp.full((heads, CHUNK, CHUNK), NEG, F32)
    rows = []
    for c in range(qc):
        row = []
        for k in range(kc):
            m = k - c
            if m < 0 or m > BAND_CHUNKS:
                row.append(hidden)
            elif m < BAND_CHUNKS - 2:
                row.append(far_tile)
            else:
                t = m - (BAND_CHUNKS - 2)
                row.append(tiles[:, :, t * CHUNK:(t + 1) * CHUNK])
        rows.append(jnp.concatenate(row, axis=2))
    return jnp.concatenate(rows, axis=1)


def _head_rows(x, heads):
    t, w = x.shape
    tiled = jnp.concatenate([x] * heads, axis=0)
    row_head = lax.div(lax.broadcasted_iota(jnp.int32, tiled.shape, 0), t)
    col_head = lax.div(lax.broadcasted_iota(jnp.int32, tiled.shape, 1), HEAD_DIM)
    return jnp.where(row_head == col_head, tiled, jnp.zeros_like(tiled))


def _fold_head_rows(o, heads):
    t = o.shape[0] // heads
    col_head = lax.div(lax.broadcasted_iota(jnp.int32, (t, o.shape[1]), 1), HEAD_DIM)
    out = jnp.zeros((t, o.shape[1]), F32)
    for h in range(heads):
        out = jnp.where(col_head == h, o[h * t:(h + 1) * t], out)
    return out


def _all_heads_rms(o, g):
    parts = [_head_pair_rms(o[:, c:c + LANES], g[:, c:c + LANES]) for c in range(0, o.shape[1], LANES)]
    return jnp.concatenate(parts, axis=1)


def _pad_rows(x, rows):
    return jnp.concatenate([x, jnp.zeros((rows - x.shape[0], x.shape[1]), x.dtype)], axis=0)


def _sample_kernel(qa_ref, ka_ref, va_ref, ckt_ref, cvt_ref, qb_ref, kb_ref, vb_ref, ckbt_ref, cvbt_ref,
                   suffix_ref, bias_ref, ga_ref, gb_ref, oa_ref, ob_ref, carry_ref, acc_ref, live_ref):
    t, w = qa_ref.shape[1:]
    heads = w // HEAD_DIM
    tk = SB_TK
    past = ckt_ref.shape[2]

    carry_ref[...] = jnp.zeros_like(carry_ref)
    q = _head_rows(qa_ref[0], heads)
    suffix = suffix_ref[...]
    frame = lax.rem(lax.broadcasted_iota(jnp.int32, (heads * t, tk), 0), t)
    key = lax.broadcasted_iota(jnp.int32, (heads * t, tk), 1)
    wts = _sb_weights(_dot_nt(q, _pad_rows(ka_ref[0], tk).astype(BF16)), suffix, carry_ref, key < frame)
    acc_ref[...] = _dot(wts, _pad_rows(va_ref[0], tk).astype(BF16))
    def cached(blk):
        cols = slice(blk * tk, (blk + 1) * tk)
        wts = _sb_weights(_dot(q, ckt_ref[0, :, cols].astype(BF16)), suffix, carry_ref, None)
        acc_ref[...] += _dot_nt(wts, cvt_ref[0, :, cols].astype(BF16))

    live_ref[0] = jnp.int32(1)
    newest_first = list(reversed(range(past // tk)))
    for first in range(0, len(newest_first), 2):
        @pl.when(live_ref[0] != 0)
        def _(first=first):
            for blk in newest_first[first:first + 2]:
                cached(blk)
            live_ref[0] = (jnp.max(carry_ref[...]) > SB_DEAD).astype(jnp.int32)
    oa_ref[0] = _all_heads_rms(_fold_head_rows(acc_ref[...], heads), ga_ref[...]).astype(oa_ref.dtype)

    qb = _head_rows(qb_ref[0], heads)
    s = jnp.concatenate([_dot(qb, ckbt_ref[0].astype(BF16)),
                         _dot_nt(qb, _pad_rows(kb_ref[0], LANES).astype(BF16))], axis=1)
    e, l = _softmax_unnormalised(s + bias_ref[...])
    e = e.astype(BF16)
    band = ckbt_ref.shape[2]
    o = _dot_nt(e[:, :band], cvbt_ref[0].astype(BF16)) + _dot(e[:, band:], _pad_rows(vb_ref[0], LANES).astype(BF16))
    ob_ref[0] = _all_heads_rms(_fold_head_rows(o / l, heads), gb_ref[...]).astype(ob_ref.dtype)


def _sample_call(qa, ka, va, ckt, cvt, qb, kb, vb, ckbt, cvbt, suffix, bias, ga, gb, layer):
    b, t, w = qa.shape
    heads = w // HEAD_DIM
    new = pl.BlockSpec((1, t, w), lambda bi: (bi, 0, 0))

    def cache(c):
        return pl.BlockSpec((None, 1) + c.shape[2:], lambda bi: (layer, bi, 0, 0))
    return pl.pallas_call(
        _sample_kernel, grid=(b,),
        in_specs=[new, new, new, cache(ckt), cache(cvt), new, new, new, cache(ckbt), cache(cvbt),
                  _resident(suffix.shape), _resident(bias.shape), _resident(ga.shape[1:], layer),
                  _resident(gb.shape[1:], layer)],
        out_specs=[new, new], out_shape=[jax.ShapeDtypeStruct((b, t, w), BF16)] * 2,
        scratch_shapes=[pltpu.VMEM((heads * t, SB_TK), F32), pltpu.VMEM((heads * t, w), F32),
                        pltpu.SMEM((1,), jnp.int32)],
        compiler_params=_params("parallel"), name="sample_attention",
    )(qa, ka, va, ckt, cvt, qb, kb, vb, ckbt, cvbt, suffix, bias, ga, gb)


def _sample_bias(tiles, far, t):
    heads = tiles.shape[0]
    near = BAND - 2 * CHUNK
    parts = [jnp.broadcast_to(far[:, None, None], (heads, t, near)), tiles[:, :t, :2 * CHUNK + t],
             jnp.full((heads, t, LANES - t), NEG, F32)]
    return jnp.concatenate(parts, axis=2).reshape(heads * t, BAND + LANES)


def _per_head_t(cache):
    n_l, n_b, n_p, n_h, n_d = cache.shape
    return jnp.transpose(cache, (0, 1, 3, 4, 2)).reshape(n_l, n_b, n_h * n_d, n_p)


def kernel(x_prompt, x_sample, p_prompt, p_sample, cache_sb_k, cache_sb_v, cache_cb_k, cache_cb_v, g_mix, w_in,
           rel_table, g_out_sb, g_out_cb, w_out, g_ffn, w_gate, w_up, w_down, g_ple, w_ple_gate, w_ple_proj,
           g_final):
    depth = w_in.shape[0]
    b, s, d = x_prompt.shape
    bs, t, _ = x_sample.shape
    w = g_out_sb.shape[1]
    heads = w // HEAD_DIM
    past, band = cache_sb_k.shape[2], cache_cb_k.shape[2]
    assert g_out_cb.shape[1] == w and w_in.shape[2] == 6 * w and w % LANES == 0
    assert s % SB_TQ == 0 and s % (CB_GROUPS * CB_TQ) == 0 and s >= BAND and s % ROW_TILE == 0
    assert band == BAND and past % SB_TK == 0 and t <= CHUNK and t % 8 == 0

    gain = lambda g: g[:, None, :]
    wts = dict(
        g_mix=gain(g_mix), g_ffn=gain(g_ffn), g_ple=gain(g_ple),
        wq=jnp.concatenate([w_in[:, :, :w], w_in[:, :, 3 * w:4 * w]], axis=2).astype(BF16),
        wkvt=jnp.swapaxes(jnp.concatenate([w_in[:, :, w:3 * w], w_in[:, :, 4 * w:]], axis=2), 1, 2).astype(BF16),
        w_out=w_out.astype(BF16), w_gate=w_gate.astype(BF16), w_up=w_up.astype(BF16), w_down=w_down.astype(BF16),
        w_ple_gate=w_ple_gate.astype(BF16), w_ple_proj=w_ple_proj.astype(BF16))
    g_sb, g_cb = gain(g_out_sb), gain(g_out_cb)
    g_fin = g_final.reshape(1, d)
    suffix = _suffix_matrix(SB_TK)
    tiles = _bias_tiles(rel_table)
    far = rel_table[:, 2 * REL_MAX, :]
    caches = [_per_head_t(c) for c in (cache_sb_k, cache_sb_v, cache_cb_k, cache_cb_v)]
    pp = p_prompt.reshape(depth, b * s, -1)
    ps = p_sample.reshape(depth, bs * t, -1)

    xp = x_prompt.reshape(b * s, d)
    xs = x_sample.reshape(bs * t, d)
    qkv_p = _qkv_call(xp, wts, 0, ROW_TILE, s)
    qkv_s = _qkv_call(xs, wts, 0, bs * t, None)
    new_s = []
    for i in range(depth):
        qa, qb = [u.reshape(b, s, w) for u in qkv_p[:2]]
        kat, vat, kbt, vbt = kv_all = qkv_p[2:]
        a = _sb_call(qa, kat, vat, suffix, g_sb, i)
        ob = _cb_call(qb, kbt, vbt, _prompt_bias(tiles[i], far[i]), g_cb, i)

        sqa, sqb, ska, sva, skb, svb = [u.reshape(bs, t, w) for u in qkv_s]
        new_s.append((ska, sva, skb, svb))
        sa, sob = _sample_call(sqa, ska, sva, caches[0], caches[1], sqb, skb, svb, caches[2], caches[3],
                               suffix, _sample_bias(tiles[i], far[i], t), g_sb, g_cb, i)

        out_p = _layer_call(xp, a.reshape(b * s, w), ob.reshape(b * s, w), pp, wts, i, g_fin, ROW_TILE, s, kv_all)
        out_s = _layer_call(xs, sa.reshape(bs * t, w), sob.reshape(bs * t, w), ps, wts, i, g_fin, bs * t, None)
        if i == depth - 1:
            xp, xs = out_p, out_s
        else:
            xp, qkv_p = out_p[0], out_p[1:]
            xs, qkv_s = out_s[0], out_s[1:]

    def positions_major(kv_t):
        return jnp.transpose(kv_t, (0, 1, 4, 2, 3))

    def stack(items):
        return jnp.stack(items).reshape(depth, bs, t, heads, HEAD_DIM)

    return (xp.reshape(b, s, d), xs.reshape(bs, t, d),
            positions_major(kat), positions_major(vat),
            positions_major(kbt[..., s - BAND:]), positions_major(vbt[..., s - BAND:]),
            stack([e[0] for e in new_s]), stack([e[1] for e in new_s]),
            stack([e[2] for e in new_s]), stack([e[3] for e in new_s]))
```

```python
import functools

import jax
import jax.numpy as jnp
from jax import lax
from jax.experimental import pallas as pl
from jax.experimental.pallas import tpu as pltpu

F32 = jnp.float32
BF16 = jnp.bfloat16

HEAD_DIM = 64
CHUNK = 64
BAND_CHUNKS = 8
BAND = BAND_CHUNKS * CHUNK
REL_MAX = 128
EPS = 1e-6
Q_SCALE = HEAD_DIM ** -0.5

LANES = 128
NEG = -1e30

ROW_TILE = 256
QKV_TILE = 512
SB_TQ = 256
SB_TK = 128
SB_CHAINS = 2
SB_DEAD = -120.0
CB_TQ = 256
CB_TK = CB_TQ + BAND
CB_GROUPS = 4
VMEM_LIMIT = 56 * 1024 * 1024


def _params(*semantics):
    return pltpu.CompilerParams(dimension_semantics=semantics, vmem_limit_bytes=VMEM_LIMIT)


def _resident(shape, layer=None):
    zeros = (0,) * len(shape)
    if layer is None:
        return pl.BlockSpec(shape, lambda *_: zeros, pipeline_mode=pl.Buffered(1))
    return pl.BlockSpec((None,) + tuple(shape), lambda *_: (layer,) + zeros, pipeline_mode=pl.Buffered(1))


def _rms_unit(x):
    return x * lax.rsqrt(jnp.mean(x * x, axis=-1, keepdims=True) + EPS)


def _dot(a, b):
    return jnp.dot(a, b, preferred_element_type=F32)


def _dot_nt(a, b):
    return lax.dot_general(a, b, (((1,), (1,)), ((), ())), preferred_element_type=F32)


def _softplus(z):
    return jnp.maximum(z, 0.0) + jnp.log(1.0 + jnp.exp(-jnp.abs(z)))


def _head_pair_rms(o, g):
    first = lax.broadcasted_iota(jnp.int32, o.shape, 1) < HEAD_DIM
    o2 = o * o
    s0 = jnp.sum(jnp.where(first, o2, 0.0), axis=-1, keepdims=True)
    s1 = jnp.sum(jnp.where(first, 0.0, o2), axis=-1, keepdims=True)
    ms = jnp.where(first, s0, s1) * (1.0 / HEAD_DIM)
    return o * lax.rsqrt(ms + EPS) * g


def _store_qkv(h, wq_ref, wkvt_ref, outs, per_head_t):
    qa_ref, qb_ref = outs[:2]
    w = qa_ref.shape[-1]
    q = _dot(h, wq_ref[...]) * Q_SCALE
    qa_ref[...] = q[:, :w].astype(BF16)
    qb_ref[...] = q[:, w:].astype(BF16)
    for c, ref in enumerate(outs[2:]):
        w_t = wkvt_ref[c * w:(c + 1) * w, :]
        if per_head_t:
            ref[0] = _dot_nt(w_t, h).reshape(ref.shape[1:])
        else:
            ref[...] = _dot_nt(h, w_t)


def _qkv_kernel(x_ref, g_ref, wq_ref, wkvt_ref, *outs, per_head_t):
    h = (_rms_unit(x_ref[...]) * g_ref[...]).astype(BF16)
    _store_qkv(h, wq_ref, wkvt_ref, outs, per_head_t)


def _layer_tail(x_ref, a_ref, ob_ref, p_ref, wo_ref, gf_ref, wg_ref, wu_ref, wd_ref, gp_ref, wpg_ref, wpp_ref):
    w = a_ref.shape[-1]
    x = x_ref[...] + _dot(a_ref[...], wo_ref[:w, :]) + _dot(ob_ref[...], wo_ref[w:, :])
    h = (_rms_unit(x) * gf_ref[...]).astype(BF16)
    gate = _dot(h, wg_ref[...])
    up = _dot(h, wu_ref[...])
    act = (gate * jax.nn.sigmoid(gate) * up).astype(BF16)
    x = x + _dot(act, wd_ref[...])
    hp = (_rms_unit(x) * gp_ref[...]).astype(BF16)
    ple_gate = jax.nn.sigmoid(_dot(hp, wpg_ref[...]))
    return x + ple_gate * _dot(p_ref[...].astype(BF16), wpp_ref[...])


N_TAIL = 12


def _mid_layer_kernel(*refs, per_head_t, n_alias):
    tail, (gn_ref, wq_ref, wkvt_ref) = refs[:N_TAIL], refs[N_TAIL:N_TAIL + 3]
    xo_ref, outs = refs[N_TAIL + 3 + n_alias], refs[N_TAIL + 4 + n_alias:]
    x = _layer_tail(*tail)
    xo_ref[...] = x
    _store_qkv((_rms_unit(x) * gn_ref[...]).astype(BF16), wq_ref, wkvt_ref, outs, per_head_t)


def _last_layer_kernel(*refs):
    tail, (gn_ref, y_ref) = refs[:N_TAIL], refs[N_TAIL:]
    y_ref[...] = _rms_unit(_layer_tail(*tail)) * gn_ref[...]


def _row_spec(tm, width):
    return pl.BlockSpec((tm, width), lambda i: (i, 0))


def _qkv_out(rows, w, tm, seq, layer, depth):
    q_shapes = [jax.ShapeDtypeStruct((rows, w), BF16)] * 2
    q_specs = [_row_spec(tm, w)] * 2
    if seq is None:
        return q_shapes + [jax.ShapeDtypeStruct((rows, w), F32)] * 4, q_specs + [_row_spec(tm, w)] * 4
    heads, tiles = w // HEAD_DIM, seq // tm
    kv_shape = jax.ShapeDtypeStruct((depth, rows // seq, heads, HEAD_DIM, seq), F32)
    kv_spec = pl.BlockSpec((None, 1, heads, HEAD_DIM, tm), lambda i: (layer, i // tiles, 0, 0, i % tiles))
    return q_shapes + [kv_shape] * 4, q_specs + [kv_spec] * 4


def _qkv_call(x, wts, layer, tm, seq):
    rows, d = x.shape
    wq, wkvt, g = wts["wq"], wts["wkvt"], wts["g_mix"]
    out_shape, out_specs = _qkv_out(rows, wq.shape[2] // 2, tm, seq, layer, wq.shape[0])
    return pl.pallas_call(
        functools.partial(_qkv_kernel, per_head_t=seq is not None), grid=(rows // tm,),
        in_specs=[_row_spec(tm, d), _resident(g.shape[1:], layer), _resident(wq.shape[1:], layer),
                  _resident(wkvt.shape[1:], layer)],
        out_specs=out_specs, out_shape=out_shape,
        compiler_params=_params("parallel"), name="qkv_proj",
    )(x, g, wq, wkvt)


TAIL_WEIGHTS = ("w_out", "g_ffn", "w_gate", "w_up", "w_down", "g_ple", "w_ple_gate", "w_ple_proj")


def _layer_call(x, a, ob, p, wts, layer, g_final, tm, seq, kv_all=()):
    rows, d = x.shape
    last = layer == wts["w_out"].shape[0] - 1
    tail_in = [x, a, ob, p] + [wts[k] for k in TAIL_WEIGHTS]
    tail_specs = ([_row_spec(tm, d), _row_spec(tm, a.shape[1]), _row_spec(tm, ob.shape[1]),
                   pl.BlockSpec((None, tm, p.shape[2]), lambda i: (layer, i, 0))]
                  + [_resident(wts[k].shape[1:], layer) for k in TAIL_WEIGHTS])
    if last:
        return pl.pallas_call(
            _last_layer_kernel, grid=(rows // tm,), in_specs=tail_specs + [_resident(g_final.shape)],
            out_specs=_row_spec(tm, d), out_shape=jax.ShapeDtypeStruct((rows, d), F32),
            compiler_params=_params("parallel"), name="layer_tail_final",
        )(*tail_in, g_final)
    nxt = [wts["g_mix"], wts["wq"], wts["wkvt"]]
    out_shape, out_specs = _qkv_out(rows, wts["wq"].shape[2] // 2, tm, seq, layer + 1, wts["wq"].shape[0])
    n_in = len(tail_in) + len(nxt)
    return pl.pallas_call(
        functools.partial(_mid_layer_kernel, per_head_t=seq is not None, n_alias=len(kv_all)), grid=(rows // tm,),
        in_specs=(tail_specs + [_resident(t.shape[1:], layer + 1) for t in nxt]
                  + [pl.BlockSpec(memory_space=pl.ANY)] * len(kv_all)),
        out_specs=[_row_spec(tm, d)] + out_specs,
        out_shape=[jax.ShapeDtypeStruct((rows, d), F32)] + out_shape,
        input_output_aliases={n_in + k: 3 + k for k in range(len(kv_all))},
        compiler_params=_params("parallel"), name="layer_tail_qkv",
    )(*tail_in, *nxt, *kv_all)


def _suffix_matrix(tk):
    j = lax.broadcasted_iota(jnp.int32, (tk, 2 * tk), 0)
    c = lax.broadcasted_iota(jnp.int32, (tk, 2 * tk), 1)
    return jnp.where((c >= tk) | (j > c), -1.0, 0.0).astype(BF16)


def _sb_weights(z, suffix, carry_ref, causal, row0=0):
    tk = suffix.shape[0]
    sp = _softplus(z)
    w_parts = []
    for h in range(z.shape[1] // tk):
        cols = slice(h * tk, (h + 1) * tk)
        z_h, sp_h = z[:, cols], sp[:, cols]
        drop = sp_h if causal is None else jnp.where(causal, sp_h, 0.0)
        sums = _dot(drop.astype(BF16), suffix)
        carry = carry_ref[row0:, cols]
        log_w = (z_h - sp_h) + (sums[:, :tk] + carry)
        w = jnp.exp(log_w)
        if causal is not None:
            w = jnp.where(causal, w, 0.0)
        carry_ref[row0:, cols] = carry + sums[:, tk:]
        w_parts.append(w.astype(BF16))
    return jnp.concatenate(w_parts, axis=1)


def _sb_kernel(q_ref, kt_ref, vt_ref, suffix_ref, g_ref, o_ref, kk_ref, vv_ref, carry_ref, acc_ref, z_ref, w_ref,
               more_ref):
    tq, tk = SB_TQ, SB_TK
    per_q = tq // tk
    i = pl.program_id(2)

    @pl.when(i == 0)
    def _():
        kk_ref[...] = jnp.zeros_like(kk_ref)
        vv_ref[...] = jnp.zeros_like(vv_ref)
        for j in range(kk_ref.shape[0]):
            for h in range(2):
                rows, cols = slice(h * HEAD_DIM, (h + 1) * HEAD_DIM), slice(h * tk, (h + 1) * tk)
                kk_ref[j, rows, cols] = kt_ref[0, h, :, j * tk:(j + 1) * tk].astype(BF16)
                vv_ref[j, rows, cols] = vt_ref[0, h, :, j * tk:(j + 1) * tk].astype(BF16)

    carry_ref[...] = jnp.zeros_like(carry_ref)
    acc_ref[...] = jnp.zeros_like(acc_ref)
    suffix = suffix_ref[...]
    k_off = lax.broadcasted_iota(jnp.int32, (tq, tk), 1)
    row = lax.broadcasted_iota(jnp.int32, (tq, tk), 0)
    last_slot = (per_q - 1) % 2

    def row0(n):
        return (per_q - 1 - n) * tk if isinstance(n, int) and 0 <= n < per_q else 0

    class Chain:
        def __init__(self, c):
            self.c = c
            self.qi = i * SB_CHAINS + c
            self.rows = slice(c * tq, (c + 1) * tq)
            self.q = q_ref[0, self.rows, :]
            self.newest = (self.qi + 1) * per_q - 1
            self.carry, self.acc, self.z, self.w = carry_ref.at[c], acc_ref.at[c], z_ref.at[c], w_ref.at[c]

        def step(self, n, slot, diagonal, with_next=True):
            j = self.newest - n
            r_prev, r_cur, r_next = row0(n - 1) if diagonal else 0, row0(n), row0(n + 1)
            z = self.z[slot, r_cur:, :]
            if with_next:
                self.z[1 - slot, r_next:, :] = _dot(self.q[r_next:], kk_ref[jnp.maximum(j - 1, 0)])
            if not (diagonal and n == 0):
                self.acc[r_prev:, :] += _dot_nt(self.w[1 - slot, r_prev:, :], vv_ref[j + 1])
            causal = ((j * tk + k_off) < self.qi * tq + row)[r_cur:] if diagonal else None
            self.w[slot, r_cur:, :] = _sb_weights(z, suffix, self.carry, causal, r_cur)

        def emit(self, oldest):
            o = self.acc[...] + _dot_nt(self.w[last_slot], vv_ref[oldest])
            o_ref[0, self.rows, :] = _head_pair_rms(o, g_ref[...]).astype(o_ref.dtype)

        def first_steps(self, n_steps):
            self.z[0, row0(0):, :] = _dot(self.q[row0(0):], kk_ref[self.newest])
            for n in range(n_steps):
                self.step(n, n % 2, n < per_q, with_next=n + 1 < n_steps)
            self.emit(self.newest - (n_steps - 1))
            more = jnp.logical_and(jnp.max(self.carry[...]) > SB_DEAD, self.newest >= n_steps)
            more_ref[self.c] = more.astype(jnp.int32)

        def older_steps(self):
            @pl.when(more_ref[self.c] != 0)
            def _():
                self.z[0] = _dot(self.q, kk_ref[self.newest - 2 * per_q])

                def more(state):
                    t, live = state
                    return jnp.logical_and(t < self.qi, live)

                def trip(state):
                    t, _ = state
                    for d in range(per_q):
                        self.step(per_q * (t + 1) + d, d % 2, False)
                    return t + 1, jnp.max(self.carry[...]) > SB_DEAD
                trips, _ = lax.while_loop(more, trip, (jnp.int32(1), self.qi > 0))
                self.emit(per_q * (self.qi - trips))

    chains = [Chain(c) for c in range(SB_CHAINS)]

    @pl.when(i == 0)
    def _():
        chains[0].first_steps(per_q)
        for ch in chains[1:]:
            ch.first_steps(2 * per_q)

    @pl.when(i > 0)
    def _():
        for ch in chains:
            ch.first_steps(2 * per_q)

    for ch in chains:
        ch.older_steps()


def _pair_kv_spec(s, layer):
    return pl.BlockSpec((None, 1, 2, HEAD_DIM, s), lambda bi, pi, i: (layer, bi, pi, 0, 0))


def _sb_call(q, kt, vt, suffix, g, layer):
    b, s, w = q.shape
    nkb = s // SB_TK
    step_rows = SB_CHAINS * SB_TQ
    assert (SB_TQ // SB_TK) % 2 == 0 and SB_CHAINS >= 2
    return pl.pallas_call(
        _sb_kernel, grid=(b, w // LANES, s // step_rows),
        in_specs=[pl.BlockSpec((1, step_rows, LANES), lambda bi, pi, i: (bi, i, pi)),
                  _pair_kv_spec(s, layer), _pair_kv_spec(s, layer),
                  _resident(suffix.shape), pl.BlockSpec((None, 1, LANES), lambda bi, pi, i: (layer, 0, pi))],
        out_specs=pl.BlockSpec((1, step_rows, LANES), lambda bi, pi, i: (bi, i, pi)),
        out_shape=jax.ShapeDtypeStruct((b, s, w), BF16),
        scratch_shapes=[pltpu.VMEM((nkb, LANES, 2 * SB_TK), BF16), pltpu.VMEM((nkb, LANES, 2 * SB_TK), BF16),
                        pltpu.VMEM((SB_CHAINS, SB_TQ, 2 * SB_TK), F32), pltpu.VMEM((SB_CHAINS, SB_TQ, LANES), F32),
                        pltpu.VMEM((SB_CHAINS, 2, SB_TQ, 2 * SB_TK), F32),
                        pltpu.VMEM((SB_CHAINS, 2, SB_TQ, 2 * SB_TK), BF16), pltpu.SMEM((SB_CHAINS,), jnp.int32)],
        compiler_params=_params("parallel", "parallel", "arbitrary"), name="sb_attention",
    )(q, kt, vt, suffix, g)


def _softmax_unnormalised(s):
    e = jnp.exp(s - jnp.max(s, axis=-1, keepdims=True))
    return e, jnp.sum(e, axis=-1, keepdims=True)


def _cb_kernel(q_ref, qn_ref, kt_ref, vt_ref, bias_ref, g_ref, o_ref, kk_ref, vv_ref, s_ref):
    i = pl.program_id(2)
    lead = BAND // CB_TQ
    n_window = CB_TK // CB_TQ
    last_block = kk_ref.shape[1] - n_window

    def window(ref, h, block):
        return jnp.concatenate([ref[h, block + c] for c in range(n_window)], axis=1)

    def put_scores(slot, q, block):
        for h in range(2):
            s_ref[slot, h] = _dot(q, window(kk_ref, h, block))

    @pl.when(i == 0)
    def _():
        kk_ref[...] = jnp.zeros_like(kk_ref)
        vv_ref[...] = jnp.zeros_like(vv_ref)
        for h in range(2):
            rows = slice(h * HEAD_DIM, (h + 1) * HEAD_DIM)
            for c in range(kk_ref.shape[1] - lead):
                kk_ref[h, lead + c, rows, :] = kt_ref[0, h, :, c * CB_TQ:(c + 1) * CB_TQ].astype(BF16)
                vv_ref[h, lead + c, rows, :] = vt_ref[0, h, :, c * CB_TQ:(c + 1) * CB_TQ].astype(BF16)
        put_scores(0, q_ref[0, :CB_TQ, :], 0)

    key = lax.broadcasted_iota(jnp.int32, (CB_TQ, CB_TK), 1)
    first = lax.broadcasted_iota(jnp.int32, (CB_TQ, LANES), 1) < HEAD_DIM
    for g in range(CB_GROUPS):
        block = i * CB_GROUPS + g
        rows = slice(g * CB_TQ, (g + 1) * CB_TQ)
        if g + 1 < CB_GROUPS:
            put_scores(g + 1, q_ref[0, (g + 1) * CB_TQ:(g + 2) * CB_TQ, :], block + 1)
        else:
            put_scores(0, qn_ref[0], jnp.minimum(block + 1, last_block))
        exists = key >= BAND - block * CB_TQ
        o = None
        for h in range(2):
            e, l = _softmax_unnormalised(jnp.where(exists, s_ref[g, h] + bias_ref[h], NEG))
            o_h = _dot_nt(e.astype(BF16), window(vv_ref, h, block)) / l
            o = o_h if h == 0 else jnp.where(first, o, o_h)
        o_ref[0, rows, :] = _head_pair_rms(o, g_ref[...]).astype(o_ref.dtype)


def _cb_call(q, kt, vt, bias, g, layer):
    b, s, w = q.shape
    blocks = (BAND + s) // CB_TQ
    step_rows = CB_GROUPS * CB_TQ
    n_q = s // CB_TQ
    return pl.pallas_call(
        _cb_kernel, grid=(b, w // LANES, s // step_rows),
        in_specs=[pl.BlockSpec((1, step_rows, LANES), lambda bi, pi, i: (bi, i, pi)),
                  pl.BlockSpec((1, CB_TQ, LANES), lambda bi, pi, i: (bi, jnp.minimum((i + 1) * CB_GROUPS, n_q - 1), pi)),
                  _pair_kv_spec(s, layer), _pair_kv_spec(s, layer),
                  pl.BlockSpec((2, CB_TQ, CB_TK), lambda bi, pi, i: (pi, 0, 0)),
                  pl.BlockSpec((None, 1, LANES), lambda bi, pi, i: (layer, 0, pi))],
        out_specs=pl.BlockSpec((1, step_rows, LANES), lambda bi, pi, i: (bi, i, pi)),
        out_shape=jax.ShapeDtypeStruct((b, s, w), BF16),
        scratch_shapes=[pltpu.VMEM((2, blocks, LANES, CB_TQ), BF16), pltpu.VMEM((2, blocks, LANES, CB_TQ), BF16),
                        pltpu.VMEM((CB_GROUPS, 2, CB_TQ, CB_TK), F32)],
        compiler_params=_params("parallel", "parallel", "arbitrary"), name="cb_attention",
    )(q, q, kt, vt, bias, g)


def _bias_tiles_kernel(table_ref, o_ref, *, heads):
    layer, head = pl.program_id(0), pl.program_id(1)
    n_rel = 2 * REL_MAX + 1
    shape = o_ref.shape[2:]
    dist = (2 * CHUNK + lax.broadcasted_iota(jnp.int32, shape, 0) - lax.broadcasted_iota(jnp.int32, shape, 1))
    idx = jnp.clip(dist, -REL_MAX, REL_MAX) + REL_MAX

    def pick(r, acc):
        return jnp.where(idx == r, table_ref[(layer * n_rel + r) * heads + head], acc)
    lowest = REL_MAX + 2 * CHUNK - (shape[1] - 1)
    o_ref[0, 0] = lax.fori_loop(max(lowest, 0), n_rel, pick, jnp.zeros(shape, F32), unroll=8)


def _bias_tiles(rel_table):
    n_layers, _, heads = rel_table.shape
    return pl.pallas_call(
        functools.partial(_bias_tiles_kernel, heads=heads), grid=(n_layers, heads),
        in_specs=[pl.BlockSpec(memory_space=pltpu.SMEM)],
        out_specs=pl.BlockSpec((1, 1, CHUNK, 3 * CHUNK), lambda l, h: (l, h, 0, 0)),
        out_shape=jax.ShapeDtypeStruct((n_layers, heads, CHUNK, 3 * CHUNK), F32),
        compiler_params=_params("parallel", "parallel"), name="rel_bias_tiles",
    )(rel_table.reshape(-1))


def _prompt_bias(tiles, far):
    heads = tiles.shape[0]
    qc, kc = CB_TQ // CHUNK, CB_TK // CHUNK
    far_tile = jnp.broadcast_to(far[:, None, None], (heads, CHUNK, CHUNK))
    hidden = jnp.full((heads, CHUNK, CHUNK), NEG, F32)
    rows = []
    for c in range(qc):
        row = []
        for k in range(kc):
            m = k - c
            if m < 0 or m > BAND_CHUNKS:
                row.append(hidden)
            elif m < BAND_CHUNKS - 2:
                row.append(far_tile)
            else:
                t = m - (BAND_CHUNKS - 2)
                row.append(tiles[:, :, t * CHUNK:(t + 1) * CHUNK])
        rows.append(jnp.concatenate(row, axis=2))
    return jnp.concatenate(rows, axis=1)


def _head_rows(x, heads):
    t, w = x.shape
    tiled = jnp.concatenate([x] * heads, axis=0)
    row_head = lax.div(lax.broadcasted_iota(jnp.int32, tiled.shape, 0), t)
    col_head = lax.div(lax.broadcasted_iota(jnp.int32, tiled.shape, 1), HEAD_DIM)
    return jnp.where(row_head == col_head, tiled, jnp.zeros_like(tiled))


def _fold_head_rows(o, heads):
    t = o.shape[0] // heads
    col_head = lax.div(lax.broadcasted_iota(jnp.int32, (t, o.shape[1]), 1), HEAD_DIM)
    out = jnp.zeros((t, o.shape[1]), F32)
    for h in range(heads):
        out = jnp.where(col_head == h, o[h * t:(h + 1) * t], out)
    return out


def _all_heads_rms(o, g):
    parts = [_head_pair_rms(o[:, c:c + LANES], g[:, c:c + LANES]) for c in range(0, o.shape[1], LANES)]
    return jnp.concatenate(parts, axis=1)


def _pad_rows(x, rows):
    return jnp.concatenate([x, jnp.zeros((rows - x.shape[0], x.shape[1]), x.dtype)], axis=0)


def _sample_kernel(qa_ref, ka_ref, va_ref, ckt_ref, cvt_ref, qb_ref, kb_ref, vb_ref, ckbt_ref, cvbt_ref,
                   suffix_ref, bias_ref, ga_ref, gb_ref, oa_ref, ob_ref, carry_ref, acc_ref, live_ref):
    t, w = qa_ref.shape[1:]
    heads = w // HEAD_DIM
    tk = SB_TK
    past = ckt_ref.shape[2]

    carry_ref[...] = jnp.zeros_like(carry_ref)
    q = _head_rows(qa_ref[0], heads)
    suffix = suffix_ref[...]
    frame = lax.rem(lax.broadcasted_iota(jnp.int32, (heads * t, tk), 0), t)
    key = lax.broadcasted_iota(jnp.int32, (heads * t, tk), 1)
    wts = _sb_weights(_dot_nt(q, _pad_rows(ka_ref[0], tk).astype(BF16)), suffix, carry_ref, key < frame)
    acc_ref[...] = _dot(wts, _pad_rows(va_ref[0], tk).astype(BF16))
    def cached(blk):
        cols = slice(blk * tk, (blk + 1) * tk)
        wts = _sb_weights(_dot(q, ckt_ref[0, :, cols].astype(BF16)), suffix, carry_ref, None)
        acc_ref[...] += _dot_nt(wts, cvt_ref[0, :, cols].astype(BF16))

    live_ref[0] = jnp.int32(1)
    newest_first = list(reversed(range(past // tk)))
    for first in range(0, len(newest_first), 2):
        @pl.when(live_ref[0] != 0)
        def _(first=first):
            for blk in newest_first[first:first + 2]:
                cached(blk)
            live_ref[0] = (jnp.max(carry_ref[...]) > SB_DEAD).astype(jnp.int32)
    oa_ref[0] = _all_heads_rms(_fold_head_rows(acc_ref[...], heads), ga_ref[...]).astype(oa_ref.dtype)

    qb = _head_rows(qb_ref[0], heads)
    s = jnp.concatenate([_dot(qb, ckbt_ref[0].astype(BF16)),
                         _dot_nt(qb, _pad_rows(kb_ref[0], LANES).astype(BF16))], axis=1)
    e, l = _softmax_unnormalised(s + bias_ref[...])
    e = e.astype(BF16)
    band = ckbt_ref.shape[2]
    o = _dot_nt(e[:, :band], cvbt_ref[0].astype(BF16)) + _dot(e[:, band:], _pad_rows(vb_ref[0], LANES).astype(BF16))
    ob_ref[0] = _all_heads_rms(_fold_head_rows(o / l, heads), gb_ref[...]).astype(ob_ref.dtype)


def _sample_call(qa, ka, va, ckt, cvt, qb, kb, vb, ckbt, cvbt, suffix, bias, ga, gb, layer):
    b, t, w = qa.shape
    heads = w // HEAD_DIM
    new = pl.BlockSpec((1, t, w), lambda bi: (bi, 0, 0))

    def cache(c):
        return pl.BlockSpec((None, 1) + c.shape[2:], lambda bi: (layer, bi, 0, 0))
    return pl.pallas_call(
        _sample_kernel, grid=(b,),
        in_specs=[new, new, new, cache(ckt), cache(cvt), new, new, new, cache(ckbt), cache(cvbt),
                  _resident(suffix.shape), _resident(bias.shape), _resident(ga.shape[1:], layer),
                  _resident(gb.shape[1:], layer)],
        out_specs=[new, new], out_shape=[jax.ShapeDtypeStruct((b, t, w), BF16)] * 2,
        scratch_shapes=[pltpu.VMEM((heads * t, SB_TK), F32), pltpu.VMEM((heads * t, w), F32),
                        pltpu.SMEM((1,), jnp.int32)],
        compiler_params=_params("parallel"), name="sample_attention",
    )(qa, ka, va, ckt, cvt, qb, kb, vb, ckbt, cvbt, suffix, bias, ga, gb)


def _sample_bias(tiles, far, t):
    heads = tiles.shape[0]
    near = BAND - 2 * CHUNK
    parts = [jnp.broadcast_to(far[:, None, None], (heads, t, near)), tiles[:, :t, :2 * CHUNK + t],
             jnp.full((heads, t, LANES - t), NEG, F32)]
    return jnp.concatenate(parts, axis=2).reshape(heads * t, BAND + LANES)


def _per_head_t(cache):
    n_l, n_b, n_p, n_h, n_d = cache.shape
    return jnp.transpose(cache, (0, 1, 3, 4, 2)).reshape(n_l, n_b, n_h * n_d, n_p)


def kernel(x_prompt, x_sample, p_prompt, p_sample, cache_sb_k, cache_sb_v, cache_cb_k, cache_cb_v, g_mix, w_in,
           rel_table, g_out_sb, g_out_cb, w_out, g_ffn, w_gate, w_up, w_down, g_ple, w_ple_gate, w_ple_proj,
           g_final):
    depth = w_in.shape[0]
    b, s, d = x_prompt.shape
    bs, t, _ = x_sample.shape
    w = g_out_sb.shape[1]
    heads = w // HEAD_DIM
    past, band = cache_sb_k.shape[2], cache_cb_k.shape[2]
    assert g_out_cb.shape[1] == w and w_in.shape[2] == 6 * w and w % LANES == 0
    assert s % (SB_CHAINS * SB_TQ) == 0 and s % (CB_GROUPS * CB_TQ) == 0 and s >= BAND and s % ROW_TILE == 0 and s % QKV_TILE == 0
    assert band == BAND and past % SB_TK == 0 and t <= CHUNK and t % 8 == 0

    gain = lambda g: g[:, None, :]
    wts = dict(
        g_mix=gain(g_mix), g_ffn=gain(g_ffn), g_ple=gain(g_ple),
        wq=jnp.concatenate([w_in[:, :, :w], w_in[:, :, 3 * w:4 * w]], axis=2).astype(BF16),
        wkvt=jnp.swapaxes(jnp.concatenate([w_in[:, :, w:3 * w], w_in[:, :, 4 * w:]], axis=2), 1, 2).astype(BF16),
        w_out=w_out.astype(BF16), w_gate=w_gate.astype(BF16), w_up=w_up.astype(BF16), w_down=w_down.astype(BF16),
        w_ple_gate=w_ple_gate.astype(BF16), w_ple_proj=w_ple_proj.astype(BF16))
    g_sb, g_cb = gain(g_out_sb), gain(g_out_cb)
    g_fin = g_final.reshape(1, d)
    suffix = _suffix_matrix(SB_TK)
    tiles = _bias_tiles(rel_table)
    far = rel_table[:, 2 * REL_MAX, :]
    caches = [_per_head_t(c) for c in (cache_sb_k, cache_sb_v, cache_cb_k, cache_cb_v)]
    pp = p_prompt.reshape(depth, b * s, -1)
    ps = p_sample.reshape(depth, bs * t, -1)

    xp = x_prompt.reshape(b * s, d)
    xs = x_sample.reshape(bs * t, d)
    qkv_p = _qkv_call(xp, wts, 0, QKV_TILE, s)
    qkv_s = _qkv_call(xs, wts, 0, bs * t, None)
    new_s = []
    for i in range(depth):
        qa, qb = [u.reshape(b, s, w) for u in qkv_p[:2]]
        kat, vat, kbt, vbt = kv_all = qkv_p[2:]
        a = _sb_call(qa, kat, vat, suffix, g_sb, i)
        ob = _cb_call(qb, kbt, vbt, _prompt_bias(tiles[i], far[i]), g_cb, i)

        sqa, sqb, ska, sva, skb, svb = [u.reshape(bs, t, w) for u in qkv_s]
        new_s.append((ska, sva, skb, svb))
        sa, sob = _sample_call(sqa, ska, sva, caches[0], caches[1], sqb, skb, svb, caches[2], caches[3],
                               suffix, _sample_bias(tiles[i], far[i], t), g_sb, g_cb, i)

        out_p = _layer_call(xp, a.reshape(b * s, w), ob.reshape(b * s, w), pp, wts, i, g_fin, ROW_TILE, s, kv_all)
        out_s = _layer_call(xs, sa.reshape(bs * t, w), sob.reshape(bs * t, w), ps, wts, i, g_fin, bs * t, None)
        if i == depth - 1:
            xp, xs = out_p, out_s
        else:
            xp, qkv_p = out_p[0], out_p[1:]
            xs, qkv_s = out_s[0], out_s[1:]

    def positions_major(kv_t):
        return jnp.transpose(kv_t, (0, 1, 4, 2, 3))

    def stack(items):
        return jnp.stack(items).reshape(depth, bs, t, heads, HEAD_DIM)

    return (xp.reshape(b, s, d), xs.reshape(bs, t, d),
            positions_major(kat), positions_major(vat),
            positions_major(kbt[..., s - BAND:]), positions_major(vbt[..., s - BAND:]),
            stack([e[0] for e in new_s]), stack([e[1] for e in new_s]),
            stack([e[2] for e in new_s]), stack([e[3] for e in new_s]))
```

```python
import functools

import jax
import jax.numpy as jnp
from jax import lax
from jax.experimental import pallas as pl
from jax.experimental.pallas import tpu as pltpu

F32 = jnp.float32
BF16 = jnp.bfloat16

HEAD_DIM = 64
CHUNK = 64
BAND_CHUNKS = 8
BAND = BAND_CHUNKS * CHUNK
REL_MAX = 128
EPS = 1e-6
Q_SCALE = HEAD_DIM ** -0.5

LANES = 128
NEG = -1e30

ROW_TILE = 256
QKV_TILE = 512
SB_TQ = 256
SB_TK = 128
SB_CHAINS = 4
SB_DEAD = -120.0
CB_TQ = 256
CB_TK = CB_TQ + BAND
CB_GROUPS = 8
VMEM_LIMIT = 56 * 1024 * 1024


def _params(*semantics):
    return pltpu.CompilerParams(dimension_semantics=semantics, vmem_limit_bytes=VMEM_LIMIT)


def _resident(shape, layer=None):
    zeros = (0,) * len(shape)
    if layer is None:
        return pl.BlockSpec(shape, lambda *_: zeros, pipeline_mode=pl.Buffered(1))
    return pl.BlockSpec((None,) + tuple(shape), lambda *_: (layer,) + zeros, pipeline_mode=pl.Buffered(1))


def _rms_unit(x):
    return x * lax.rsqrt(jnp.mean(x * x, axis=-1, keepdims=True) + EPS)


def _dot(a, b):
    return jnp.dot(a, b, preferred_element_type=F32)


def _dot_nt(a, b):
    return lax.dot_general(a, b, (((1,), (1,)), ((), ())), preferred_element_type=F32)


def _softplus(z):
    return jnp.maximum(z, 0.0) + jnp.log(1.0 + jnp.exp(-jnp.abs(z)))


def _head_pair_rms(o, g):
    first = lax.broadcasted_iota(jnp.int32, o.shape, 1) < HEAD_DIM
    o2 = o * o
    s0 = jnp.sum(jnp.where(first, o2, 0.0), axis=-1, keepdims=True)
    s1 = jnp.sum(jnp.where(first, 0.0, o2), axis=-1, keepdims=True)
    ms = jnp.where(first, s0, s1) * (1.0 / HEAD_DIM)
    return o * lax.rsqrt(ms + EPS) * g


def _store_qkv(h, wq_ref, wkvt_ref, outs, per_head_t):
    qa_ref, qb_ref = outs[:2]
    w = qa_ref.shape[-1]
    q = _dot(h, wq_ref[...]) * Q_SCALE
    qa_ref[...] = q[:, :w].astype(BF16)
    qb_ref[...] = q[:, w:].astype(BF16)
    for c, ref in enumerate(outs[2:]):
        w_t = wkvt_ref[c * w:(c + 1) * w, :]
        if per_head_t:
            ref[0] = _dot_nt(w_t, h).reshape(ref.shape[1:])
        else:
            ref[...] = _dot_nt(h, w_t)


def _qkv_kernel(x_ref, g_ref, wq_ref, wkvt_ref, *outs, per_head_t):
    h = (_rms_unit(x_ref[...]) * g_ref[...]).astype(BF16)
    _store_qkv(h, wq_ref, wkvt_ref, outs, per_head_t)


def _layer_tail(x_ref, a_ref, ob_ref, p_ref, wo_ref, gf_ref, wg_ref, wu_ref, wd_ref, gp_ref, wpg_ref, wpp_ref):
    w = a_ref.shape[-1]
    x = x_ref[...] + _dot(a_ref[...], wo_ref[:w, :]) + _dot(ob_ref[...], wo_ref[w:, :])
    h = (_rms_unit(x) * gf_ref[...]).astype(BF16)
    gate = _dot(h, wg_ref[...])
    up = _dot(h, wu_ref[...])
    act = (gate * jax.nn.sigmoid(gate) * up).astype(BF16)
    x = x + _dot(act, wd_ref[...])
    hp = (_rms_unit(x) * gp_ref[...]).astype(BF16)
    ple_gate = jax.nn.sigmoid(_dot(hp, wpg_ref[...]))
    return x + ple_gate * _dot(p_ref[...].astype(BF16), wpp_ref[...])


N_TAIL = 12


def _mid_layer_kernel(*refs, per_head_t, n_alias):
    tail, (gn_ref, wq_ref, wkvt_ref) = refs[:N_TAIL], refs[N_TAIL:N_TAIL + 3]
    xo_ref, outs = refs[N_TAIL + 3 + n_alias], refs[N_TAIL + 4 + n_alias:]
    x = _layer_tail(*tail)
    xo_ref[...] = x
    _store_qkv((_rms_unit(x) * gn_ref[...]).astype(BF16), wq_ref, wkvt_ref, outs, per_head_t)


def _last_layer_kernel(*refs):
    tail, (gn_ref, y_ref) = refs[:N_TAIL], refs[N_TAIL:]
    y_ref[...] = _rms_unit(_layer_tail(*tail)) * gn_ref[...]


def _row_spec(tm, width):
    return pl.BlockSpec((tm, width), lambda i: (i, 0))


def _qkv_out(rows, w, tm, seq, layer, depth):
    q_shapes = [jax.ShapeDtypeStruct((rows, w), BF16)] * 2
    q_specs = [_row_spec(tm, w)] * 2
    if seq is None:
        return q_shapes + [jax.ShapeDtypeStruct((rows, w), F32)] * 4, q_specs + [_row_spec(tm, w)] * 4
    heads, tiles = w // HEAD_DIM, seq // tm
    kv_shape = jax.ShapeDtypeStruct((depth, rows // seq, heads, HEAD_DIM, seq), F32)
    kv_spec = pl.BlockSpec((None, 1, heads, HEAD_DIM, tm), lambda i: (layer, i // tiles, 0, 0, i % tiles))
    return q_shapes + [kv_shape] * 4, q_specs + [kv_spec] * 4


def _qkv_call(x, wts, layer, tm, seq):
    rows, d = x.shape
    wq, wkvt, g = wts["wq"], wts["wkvt"], wts["g_mix"]
    out_shape, out_specs = _qkv_out(rows, wq.shape[2] // 2, tm, seq, layer, wq.shape[0])
    return pl.pallas_call(
        functools.partial(_qkv_kernel, per_head_t=seq is not None), grid=(rows // tm,),
        in_specs=[_row_spec(tm, d), _resident(g.shape[1:], layer), _resident(wq.shape[1:], layer),
                  _resident(wkvt.shape[1:], layer)],
        out_specs=out_specs, out_shape=out_shape,
        compiler_params=_params("parallel"), name="qkv_proj",
    )(x, g, wq, wkvt)


TAIL_WEIGHTS = ("w_out", "g_ffn", "w_gate", "w_up", "w_down", "g_ple", "w_ple_gate", "w_ple_proj")


def _layer_call(x, a, ob, p, wts, layer, g_final, tm, seq, kv_all=()):
    rows, d = x.shape
    last = layer == wts["w_out"].shape[0] - 1
    tail_in = [x, a, ob, p] + [wts[k] for k in TAIL_WEIGHTS]
    tail_specs = ([_row_spec(tm, d), _row_spec(tm, a.shape[1]), _row_spec(tm, ob.shape[1]),
                   pl.BlockSpec((None, tm, p.shape[2]), lambda i: (layer, i, 0))]
                  + [_resident(wts[k].shape[1:], layer) for k in TAIL_WEIGHTS])
    if last:
        return pl.pallas_call(
            _last_layer_kernel, grid=(rows // tm,), in_specs=tail_specs + [_resident(g_final.shape)],
            out_specs=_row_spec(tm, d), out_shape=jax.ShapeDtypeStruct((rows, d), F32),
            compiler_params=_params("parallel"), name="layer_tail_final",
        )(*tail_in, g_final)
    nxt = [wts["g_mix"], wts["wq"], wts["wkvt"]]
    out_shape, out_specs = _qkv_out(rows, wts["wq"].shape[2] // 2, tm, seq, layer + 1, wts["wq"].shape[0])
    n_in = len(tail_in) + len(nxt)
    return pl.pallas_call(
        functools.partial(_mid_layer_kernel, per_head_t=seq is not None, n_alias=len(kv_all)), grid=(rows // tm,),
        in_specs=(tail_specs + [_resident(t.shape[1:], layer + 1) for t in nxt]
                  + [pl.BlockSpec(memory_space=pl.ANY)] * len(kv_all)),
        out_specs=[_row_spec(tm, d)] + out_specs,
        out_shape=[jax.ShapeDtypeStruct((rows, d), F32)] + out_shape,
        input_output_aliases={n_in + k: 3 + k for k in range(len(kv_all))},
        compiler_params=_params("parallel"), name="layer_tail_qkv",
    )(*tail_in, *nxt, *kv_all)


def _suffix_matrix(tk):
    j = lax.broadcasted_iota(jnp.int32, (tk, 2 * tk), 0)
    c = lax.broadcasted_iota(jnp.int32, (tk, 2 * tk), 1)
    return jnp.where((c >= tk) | (j > c), -1.0, 0.0).astype(BF16)


def _sb_weights(z, suffix, carry_ref, causal, row0=0):
    tk = suffix.shape[0]
    sp = _softplus(z)
    w_parts = []
    for h in range(z.shape[1] // tk):
        cols = slice(h * tk, (h + 1) * tk)
        z_h, sp_h = z[:, cols], sp[:, cols]
        drop = sp_h if causal is None else jnp.where(causal, sp_h, 0.0)
        sums = _dot(drop.astype(BF16), suffix)
        carry = carry_ref[row0:, cols]
        log_w = (z_h - sp_h) + (sums[:, :tk] + carry)
        w = jnp.exp(log_w)
        if causal is not None:
            w = jnp.where(causal, w, 0.0)
        carry_ref[row0:, cols] = carry + sums[:, tk:]
        w_parts.append(w.astype(BF16))
    return jnp.concatenate(w_parts, axis=1)


def _sb_kernel(q_ref, kt_ref, vt_ref, suffix_ref, g_ref, o_ref, kk_ref, vv_ref, carry_ref, acc_ref, z_ref, w_ref,
               more_ref):
    tq, tk = SB_TQ, SB_TK
    per_q = tq // tk
    i = pl.program_id(2)

    @pl.when(i == 0)
    def _():
        kk_ref[...] = jnp.zeros_like(kk_ref)
        vv_ref[...] = jnp.zeros_like(vv_ref)
        for j in range(kk_ref.shape[0]):
            for h in range(2):
                rows, cols = slice(h * HEAD_DIM, (h + 1) * HEAD_DIM), slice(h * tk, (h + 1) * tk)
                kk_ref[j, rows, cols] = kt_ref[0, h, :, j * tk:(j + 1) * tk].astype(BF16)
                vv_ref[j, rows, cols] = vt_ref[0, h, :, j * tk:(j + 1) * tk].astype(BF16)

    carry_ref[...] = jnp.zeros_like(carry_ref)
    acc_ref[...] = jnp.zeros_like(acc_ref)
    suffix = suffix_ref[...]
    k_off = lax.broadcasted_iota(jnp.int32, (tq, tk), 1)
    row = lax.broadcasted_iota(jnp.int32, (tq, tk), 0)
    last_slot = (per_q - 1) % 2

    def row0(n):
        return (per_q - 1 - n) * tk if isinstance(n, int) and 0 <= n < per_q else 0

    class Chain:
        def __init__(self, c):
            self.c = c
            self.qi = i * SB_CHAINS + c
            self.rows = slice(c * tq, (c + 1) * tq)
            self.q = q_ref[0, self.rows, :]
            self.newest = (self.qi + 1) * per_q - 1
            self.carry, self.acc, self.z, self.w = carry_ref.at[c], acc_ref.at[c], z_ref.at[c], w_ref.at[c]

        def step(self, n, slot, diagonal, with_next=True):
            j = self.newest - n
            r_prev, r_cur, r_next = row0(n - 1) if diagonal else 0, row0(n), row0(n + 1)
            z = self.z[slot, r_cur:, :]
            if with_next:
                self.z[1 - slot, r_next:, :] = _dot(self.q[r_next:], kk_ref[jnp.maximum(j - 1, 0)])
            if not (diagonal and n == 0):
                self.acc[r_prev:, :] += _dot_nt(self.w[1 - slot, r_prev:, :], vv_ref[j + 1])
            causal = ((j * tk + k_off) < self.qi * tq + row)[r_cur:] if diagonal else None
            self.w[slot, r_cur:, :] = _sb_weights(z, suffix, self.carry, causal, r_cur)

        def emit(self, oldest):
            o = self.acc[...] + _dot_nt(self.w[last_slot], vv_ref[oldest])
            o_ref[0, self.rows, :] = _head_pair_rms(o, g_ref[...]).astype(o_ref.dtype)

        def first_steps(self, n_steps):
            self.z[0, row0(0):, :] = _dot(self.q[row0(0):], kk_ref[self.newest])
            for n in range(n_steps):
                self.step(n, n % 2, n < per_q, with_next=n + 1 < n_steps)
            self.emit(self.newest - (n_steps - 1))
            more = jnp.logical_and(jnp.max(self.carry[...]) > SB_DEAD, self.newest >= n_steps)
            more_ref[self.c] = more.astype(jnp.int32)

        def older_steps(self):
            @pl.when(more_ref[self.c] != 0)
            def _():
                self.z[0] = _dot(self.q, kk_ref[self.newest - 2 * per_q])

                def more(state):
                    t, live = state
                    return jnp.logical_and(t < self.qi, live)

                def trip(state):
                    t, _ = state
                    for d in range(per_q):
                        self.step(per_q * (t + 1) + d, d % 2, False)
                    return t + 1, jnp.max(self.carry[...]) > SB_DEAD
                trips, _ = lax.while_loop(more, trip, (jnp.int32(1), self.qi > 0))
                self.emit(per_q * (self.qi - trips))

    chains = [Chain(c) for c in range(SB_CHAINS)]

    @pl.when(i == 0)
    def _():
        chains[0].first_steps(per_q)
        for ch in chains[1:]:
            ch.first_steps(2 * per_q)

    @pl.when(i > 0)
    def _():
        for ch in chains:
            ch.first_steps(2 * per_q)

    for ch in chains:
        ch.older_steps()


def _pair_kv_spec(s, layer):
    return pl.BlockSpec((None, 1, 2, HEAD_DIM, s), lambda bi, pi, i: (layer, bi, pi, 0, 0))


def _sb_call(q, kt, vt, suffix, g, layer):
    b, s, w = q.shape
    nkb = s // SB_TK
    step_rows = SB_CHAINS * SB_TQ
    assert (SB_TQ // SB_TK) % 2 == 0 and SB_CHAINS >= 2
    return pl.pallas_call(
        _sb_kernel, grid=(b, w // LANES, s // step_rows),
        in_specs=[pl.BlockSpec((1, step_rows, LANES), lambda bi, pi, i: (bi, i, pi)),
                  _pair_kv_spec(s, layer), _pair_kv_spec(s, layer),
                  _resident(suffix.shape), pl.BlockSpec((None, 1, LANES), lambda bi, pi, i: (layer, 0, pi))],
        out_specs=pl.BlockSpec((1, step_rows, LANES), lambda bi, pi, i: (bi, i, pi)),
        out_shape=jax.ShapeDtypeStruct((b, s, w), BF16),
        scratch_shapes=[pltpu.VMEM((nkb, LANES, 2 * SB_TK), BF16), pltpu.VMEM((nkb, LANES, 2 * SB_TK), BF16),
                        pltpu.VMEM((SB_CHAINS, SB_TQ, 2 * SB_TK), F32), pltpu.VMEM((SB_CHAINS, SB_TQ, LANES), F32),
                        pltpu.VMEM((SB_CHAINS, 2, SB_TQ, 2 * SB_TK), F32),
                        pltpu.VMEM((SB_CHAINS, 2, SB_TQ, 2 * SB_TK), BF16), pltpu.SMEM((SB_CHAINS,), jnp.int32)],
        compiler_params=_params("parallel", "parallel", "arbitrary"), name="sb_attention",
    )(q, kt, vt, suffix, g)


def _softmax_unnormalised(s):
    e = jnp.exp(s - jnp.max(s, axis=-1, keepdims=True))
    return e, jnp.sum(e, axis=-1, keepdims=True)


def _cb_kernel(q_ref, qn_ref, kt_ref, vt_ref, bias_ref, g_ref, o_ref, kk_ref, vv_ref, s_ref):
    i = pl.program_id(2)
    lead = BAND // CB_TQ
    n_window = CB_TK // CB_TQ
    last_block = kk_ref.shape[1] - n_window

    def window(ref, h, block):
        return jnp.concatenate([ref[h, block + c] for c in range(n_window)], axis=1)

    def put_scores(slot, q, block):
        for h in range(2):
            s_ref[slot, h] = _dot(q, window(kk_ref, h, block))

    @pl.when(i == 0)
    def _():
        kk_ref[...] = jnp.zeros_like(kk_ref)
        vv_ref[...] = jnp.zeros_like(vv_ref)
        for h in range(2):
            rows = slice(h * HEAD_DIM, (h + 1) * HEAD_DIM)
            for c in range(kk_ref.shape[1] - lead):
                kk_ref[h, lead + c, rows, :] = kt_ref[0, h, :, c * CB_TQ:(c + 1) * CB_TQ].astype(BF16)
                vv_ref[h, lead + c, rows, :] = vt_ref[0, h, :, c * CB_TQ:(c + 1) * CB_TQ].astype(BF16)
        put_scores(0, q_ref[0, :CB_TQ, :], 0)

    key = lax.broadcasted_iota(jnp.int32, (CB_TQ, CB_TK), 1)
    first = lax.broadcasted_iota(jnp.int32, (CB_TQ, LANES), 1) < HEAD_DIM
    for g in range(CB_GROUPS):
        block = i * CB_GROUPS + g
        rows = slice(g * CB_TQ, (g + 1) * CB_TQ)
        if g + 1 < CB_GROUPS:
            put_scores((g + 1) % 2, q_ref[0, (g + 1) * CB_TQ:(g + 2) * CB_TQ, :], block + 1)
        else:
            put_scores((g + 1) % 2, qn_ref[0], jnp.minimum(block + 1, last_block))
        o = None
        for h in range(2):
            s = s_ref[g % 2, h] + bias_ref[h]
            if g < lead:
                s = jnp.where(key >= BAND - block * CB_TQ, s, NEG)
            e, l = _softmax_unnormalised(s)
            o_h = _dot_nt(e.astype(BF16), window(vv_ref, h, block)) / l
            o = o_h if h == 0 else jnp.where(first, o, o_h)
        o_ref[0, rows, :] = _head_pair_rms(o, g_ref[...]).astype(o_ref.dtype)


def _cb_call(q, kt, vt, bias, g, layer):
    b, s, w = q.shape
    blocks = (BAND + s) // CB_TQ
    step_rows = CB_GROUPS * CB_TQ
    n_q = s // CB_TQ
    return pl.pallas_call(
        _cb_kernel, grid=(b, w // LANES, s // step_rows),
        in_specs=[pl.BlockSpec((1, step_rows, LANES), lambda bi, pi, i: (bi, i, pi)),
                  pl.BlockSpec((1, CB_TQ, LANES), lambda bi, pi, i: (bi, jnp.minimum((i + 1) * CB_GROUPS, n_q - 1), pi)),
                  _pair_kv_spec(s, layer), _pair_kv_spec(s, layer),
                  pl.BlockSpec((2, CB_TQ, CB_TK), lambda bi, pi, i: (pi, 0, 0)),
                  pl.BlockSpec((None, 1, LANES), lambda bi, pi, i: (layer, 0, pi))],
        out_specs=pl.BlockSpec((1, step_rows, LANES), lambda bi, pi, i: (bi, i, pi)),
        out_shape=jax.ShapeDtypeStruct((b, s, w), BF16),
        scratch_shapes=[pltpu.VMEM((2, blocks, LANES, CB_TQ), BF16), pltpu.VMEM((2, blocks, LANES, CB_TQ), BF16),
                        pltpu.VMEM((2, 2, CB_TQ, CB_TK), F32)],
        compiler_params=_params("parallel", "parallel", "arbitrary"), name="cb_attention",
    )(q, q, kt, vt, bias, g)


def _bias_tiles_kernel(table_ref, o_ref, *, heads):
    layer, head = pl.program_id(0), pl.program_id(1)
    n_rel = 2 * REL_MAX + 1
    shape = o_ref.shape[2:]
    dist = (2 * CHUNK + lax.broadcasted_iota(jnp.int32, shape, 0) - lax.broadcasted_iota(jnp.int32, shape, 1))
    idx = jnp.clip(dist, -REL_MAX, REL_MAX) + REL_MAX

    def pick(r, acc):
        return jnp.where(idx == r, table_ref[(layer * n_rel + r) * heads + head], acc)
    lowest = REL_MAX + 2 * CHUNK - (shape[1] - 1)
    o_ref[0, 0] = lax.fori_loop(max(lowest, 0), n_rel, pick, jnp.zeros(shape, F32), unroll=8)


def _bias_tiles(rel_table):
    n_layers, _, heads = rel_table.shape
    return pl.pallas_call(
        functools.partial(_bias_tiles_kernel, heads=heads), grid=(n_layers, heads),
        in_specs=[pl.BlockSpec(memory_space=pltpu.SMEM)],
        out_specs=pl.BlockSpec((1, 1, CHUNK, 3 * CHUNK), lambda l, h: (l, h, 0, 0)),
        out_shape=jax.ShapeDtypeStruct((n_layers, heads, CHUNK, 3 * CHUNK), F32),
        compiler_params=_params("parallel", "parallel"), name="rel_bias_tiles",
    )(rel_table.reshape(-1))


def _prompt_bias(tiles, far):
    heads = tiles.shape[0]
    qc, kc = CB_TQ // CHUNK, CB_TK // CHUNK
    far_tile = jnp.broadcast_to(far[:, None, None], (heads, CHUNK, CHUNK))
    hidden = jnp.full((heads, CHUNK, CHUNK), NEG, F32)
    rows = []
    for c in range(qc):
        row = []
        for k in range(kc):
            m = k - c
            if m < 0 or m > BAND_CHUNKS:
                row.append(hidden)
            elif m < BAND_CHUNKS - 2:
                row.append(far_tile)
            else:
                t = m - (BAND_CHUNKS - 2)
                row.append(tiles[:, :, t * CHUNK:(t + 1) * CHUNK])
        rows.append(jnp.concatenate(row, axis=2))
    return jnp.concatenate(rows, axis=1)


def _head_rows(x, heads):
    t, w = x.shape
    tiled = jnp.concatenate([x] * heads, axis=0)
    row_head = lax.div(lax.broadcasted_iota(jnp.int32, tiled.shape, 0), t)
    col_head = lax.div(lax.broadcasted_iota(jnp.int32, tiled.shape, 1), HEAD_DIM)
    return jnp.where(row_head == col_head, tiled, jnp.zeros_like(tiled))


def _fold_head_rows(o, heads):
    t = o.shape[0] // heads
    col_head = lax.div(lax.broadcasted_iota(jnp.int32, (t, o.shape[1]), 1), HEAD_DIM)
    out = jnp.zeros((t, o.shape[1]), F32)
    for h in range(heads):
        out = jnp.where(col_head == h, o[h * t:(h + 1) * t], out)
    return out


def _all_heads_rms(o, g):
    parts = [_head_pair_rms(o[:, c:c + LANES], g[:, c:c + LANES]) for c in range(0, o.shape[1], LANES)]
    return jnp.concatenate(parts, axis=1)


def _pad_rows(x, rows):
    return jnp.concatenate([x, jnp.zeros((rows - x.shape[0], x.shape[1]), x.dtype)], axis=0)


def _sample_kernel(qa_ref, ka_ref, va_ref, ckt_ref, cvt_ref, qb_ref, kb_ref, vb_ref, ckbt_ref, cvbt_ref,
                   suffix_ref, bias_ref, ga_ref, gb_ref, oa_ref, ob_ref, carry_ref, acc_ref, live_ref):
    t, w = qa_ref.shape[1:]
    heads = w // HEAD_DIM
    tk = SB_TK
    past = ckt_ref.shape[2]

    carry_ref[...] = jnp.zeros_like(carry_ref)
    q = _head_rows(qa_ref[0], heads)
    suffix = suffix_ref[...]
    frame = lax.rem(lax.broadcasted_iota(jnp.int32, (heads * t, tk), 0), t)
    key = lax.broadcasted_iota(jnp.int32, (heads * t, tk), 1)
    wts = _sb_weights(_dot_nt(q, _pad_rows(ka_ref[0], tk).astype(BF16)), suffix, carry_ref, key < frame)
    acc_ref[...] = _dot(wts, _pad_rows(va_ref[0], tk).astype(BF16))
    def cached(blk):
        cols = slice(blk * tk, (blk + 1) * tk)
        wts = _sb_weights(_dot(q, ckt_ref[0, :, cols].astype(BF16)), suffix, carry_ref, None)
        acc_ref[...] += _dot_nt(wts, cvt_ref[0, :, cols].astype(BF16))

    live_ref[0] = jnp.int32(1)
    newest_first = list(reversed(range(past // tk)))
    for first in range(0, len(newest_first), 2):
        @pl.when(live_ref[0] != 0)
        def _(first=first):
            for blk in newest_first[first:first + 2]:
                cached(blk)
            live_ref[0] = (jnp.max(carry_ref[...]) > SB_DEAD).astype(jnp.int32)
    oa_ref[0] = _all_heads_rms(_fold_head_rows(acc_ref[...], heads), ga_ref[...]).astype(oa_ref.dtype)

    qb = _head_rows(qb_ref[0], heads)
    s = jnp.concatenate([_dot(qb, ckbt_ref[0].astype(BF16)),
                         _dot_nt(qb, _pad_rows(kb_ref[0], LANES).astype(BF16))], axis=1)
    e, l = _softmax_unnormalised(s + bias_ref[...])
    e = e.astype(BF16)
    band = ckbt_ref.shape[2]
    o = _dot_nt(e[:, :band], cvbt_ref[0].astype(BF16)) + _dot(e[:, band:], _pad_rows(vb_ref[0], LANES).astype(BF16))
    ob_ref[0] = _all_heads_rms(_fold_head_rows(o / l, heads), gb_ref[...]).astype(ob_ref.dtype)


def _sample_call(qa, ka, va, ckt, cvt, qb, kb, vb, ckbt, cvbt, suffix, bias, ga, gb, layer):
    b, t, w = qa.shape
    heads = w // HEAD_DIM
    new = pl.BlockSpec((1, t, w), lambda bi: (bi, 0, 0))

    def cache(c):
        return pl.BlockSpec((None, 1) + c.shape[2:], lambda bi: (layer, bi, 0, 0))
    return pl.pallas_call(
        _sample_kernel, grid=(b,),
        in_specs=[new, new, new, cache(ckt), cache(cvt), new, new, new, cache(ckbt), cache(cvbt),
                  _resident(suffix.shape), _resident(bias.shape), _resident(ga.shape[1:], layer),
                  _resident(gb.shape[1:], layer)],
        out_specs=[new, new], out_shape=[jax.ShapeDtypeStruct((b, t, w), BF16)] * 2,
        scratch_shapes=[pltpu.VMEM((heads * t, SB_TK), F32), pltpu.VMEM((heads * t, w), F32),
                        pltpu.SMEM((1,), jnp.int32)],
        compiler_params=_params("parallel"), name="sample_attention",
    )(qa, ka, va, ckt, cvt, qb, kb, vb, ckbt, cvbt, suffix, bias, ga, gb)


def _sample_bias(tiles, far, t):
    heads = tiles.shape[0]
    near = BAND - 2 * CHUNK
    parts = [jnp.broadcast_to(far[:, None, None], (heads, t, near)), tiles[:, :t, :2 * CHUNK + t],
             jnp.full((heads, t, LANES - t), NEG, F32)]
    return jnp.concatenate(parts, axis=2).reshape(heads * t, BAND + LANES)


def _per_head_t(cache):
    n_l, n_b, n_p, n_h, n_d = cache.shape
    return jnp.transpose(cache, (0, 1, 3, 4, 2)).reshape(n_l, n_b, n_h * n_d, n_p)


def kernel(x_prompt, x_sample, p_prompt, p_sample, cache_sb_k, cache_sb_v, cache_cb_k, cache_cb_v, g_mix, w_in,
           rel_table, g_out_sb, g_out_cb, w_out, g_ffn, w_gate, w_up, w_down, g_ple, w_ple_gate, w_ple_proj,
           g_final):
    depth = w_in.shape[0]
    b, s, d = x_prompt.shape
    bs, t, _ = x_sample.shape
    w = g_out_sb.shape[1]
    heads = w // HEAD_DIM
    past, band = cache_sb_k.shape[2], cache_cb_k.shape[2]
    assert g_out_cb.shape[1] == w and w_in.shape[2] == 6 * w and w % LANES == 0
    assert s % (SB_CHAINS * SB_TQ) == 0 and s % (CB_GROUPS * CB_TQ) == 0 and s >= BAND and s % ROW_TILE == 0 and s % QKV_TILE == 0
    assert band == BAND and past % SB_TK == 0 and t <= CHUNK and t % 8 == 0

    gain = lambda g: g[:, None, :]
    wts = dict(
        g_mix=gain(g_mix), g_ffn=gain(g_ffn), g_ple=gain(g_ple),
        wq=jnp.concatenate([w_in[:, :, :w], w_in[:, :, 3 * w:4 * w]], axis=2).astype(BF16),
        wkvt=jnp.swapaxes(jnp.concatenate([w_in[:, :, w:3 * w], w_in[:, :, 4 * w:]], axis=2), 1, 2).astype(BF16),
        w_out=w_out.astype(BF16), w_gate=w_gate.astype(BF16), w_up=w_up.astype(BF16), w_down=w_down.astype(BF16),
        w_ple_gate=w_ple_gate.astype(BF16), w_ple_proj=w_ple_proj.astype(BF16))
    g_sb, g_cb = gain(g_out_sb), gain(g_out_cb)
    g_fin = g_final.reshape(1, d)
    suffix = _suffix_matrix(SB_TK)
    tiles = _bias_tiles(rel_table)
    far = rel_table[:, 2 * REL_MAX, :]
    caches = [_per_head_t(c) for c in (cache_sb_k, cache_sb_v, cache_cb_k, cache_cb_v)]
    pp = p_prompt.reshape(depth, b * s, -1)
    ps = p_sample.reshape(depth, bs * t, -1)

    xp = x_prompt.reshape(b * s, d)
    xs = x_sample.reshape(bs * t, d)
    qkv_p = _qkv_call(xp, wts, 0, QKV_TILE, s)
    qkv_s = _qkv_call(xs, wts, 0, bs * t, None)
    new_s = []
    for i in range(depth):
        qa, qb = [u.reshape(b, s, w) for u in qkv_p[:2]]
        kat, vat, kbt, vbt = kv_all = qkv_p[2:]
        a = _sb_call(qa, kat, vat, suffix, g_sb, i)
        ob = _cb_call(qb, kbt, vbt, _prompt_bias(tiles[i], far[i]), g_cb, i)

        sqa, sqb, ska, sva, skb, svb = [u.reshape(bs, t, w) for u in qkv_s]
        new_s.append((ska, sva, skb, svb))
        sa, sob = _sample_call(sqa, ska, sva, caches[0], caches[1], sqb, skb, svb, caches[2], caches[3],
                               suffix, _sample_bias(tiles[i], far[i], t), g_sb, g_cb, i)

        out_p = _layer_call(xp, a.reshape(b * s, w), ob.reshape(b * s, w), pp, wts, i, g_fin, ROW_TILE, s, kv_all)
        out_s = _layer_call(xs, sa.reshape(bs * t, w), sob.reshape(bs * t, w), ps, wts, i, g_fin, bs * t, None)
        if i == depth - 1:
            xp, xs = out_p, out_s
        else:
            xp, qkv_p = out_p[0], out_p[1:]
            xs, qkv_s = out_s[0], out_s[1:]

    def positions_major(kv_t):
        return jnp.transpose(kv_t, (0, 1, 4, 2, 3))

    def stack(items):
        return jnp.stack(items).reshape(depth, bs, t, heads, HEAD_DIM)

    return (xp.reshape(b, s, d), xs.reshape(bs, t, d),
            positions_major(kat), positions_major(vat),
            positions_major(kbt[..., s - BAND:]), positions_major(vbt[..., s - BAND:]),
            stack([e[0] for e in new_s]), stack([e[1] for e in new_s]),
            stack([e[2] for e in new_s]), stack([e[3] for e in new_s]))
```

```python
import functools

import jax
import jax.numpy as jnp
from jax import lax
from jax.experimental import pallas as pl
from jax.experimental.pallas import tpu as pltpu

F32 = jnp.float32
BF16 = jnp.bfloat16

HEAD_DIM = 64
CHUNK = 64
BAND_CHUNKS = 8
BAND = BAND_CHUNKS * CHUNK
REL_MAX = 128
EPS = 1e-6
Q_SCALE = HEAD_DIM ** -0.5

LANES = 128
NEG = -1e30

ROW_TILE = 256
QKV_TILE = 512
SB_TQ = 256
SB_TK = 128
SB_CHAINS = 4
SB_DEAD = -120.0
CB_TQ = 256
CB_TK = CB_TQ + BAND
CB_GROUPS = 8
VMEM_LIMIT = 58 * 1024 * 1024


def _params(*semantics):
    return pltpu.CompilerParams(dimension_semantics=semantics, vmem_limit_bytes=VMEM_LIMIT)


def _resident(shape, layer=None):
    zeros = (0,) * len(shape)
    if layer is None:
        return pl.BlockSpec(shape, lambda *_: zeros, pipeline_mode=pl.Buffered(1))
    return pl.BlockSpec((None,) + tuple(shape), lambda *_: (layer,) + zeros, pipeline_mode=pl.Buffered(1))


def _rms_unit(x):
    return x * lax.rsqrt(jnp.mean(x * x, axis=-1, keepdims=True) + EPS)


def _dot(a, b):
    return jnp.dot(a, b, preferred_element_type=F32)


def _dot_nt(a, b):
    return lax.dot_general(a, b, (((1,), (1,)), ((), ())), preferred_element_type=F32)


def _softplus(z):
    return jnp.maximum(z, 0.0) + jnp.log(1.0 + jnp.exp(-jnp.abs(z)))


def _head_pair_rms(o, g):
    first = lax.broadcasted_iota(jnp.int32, o.shape, 1) < HEAD_DIM
    o2 = o * o
    s0 = jnp.sum(jnp.where(first, o2, 0.0), axis=-1, keepdims=True)
    s1 = jnp.sum(jnp.where(first, 0.0, o2), axis=-1, keepdims=True)
    ms = jnp.where(first, s0, s1) * (1.0 / HEAD_DIM)
    return o * lax.rsqrt(ms + EPS) * g


def _store_qkv(h, wq_ref, wkvt_ref, outs, per_head_t, prev=()):
    qa_ref, qb_ref = outs[:2]
    w = qa_ref.shape[-1]
    q = _dot(h, wq_ref[...]) * Q_SCALE
    qa_ref[...] = q[:, :w].astype(BF16)
    qb_ref[...] = q[:, w:].astype(BF16)
    band_tails = outs[6:]
    for c, ref in enumerate(outs[2:6]):
        w_t = wkvt_ref[c * w:(c + 1) * w, :]
        if per_head_t:
            n_prev = ref.shape[0] - 1
            new = _dot_nt(w_t, h).reshape(ref.shape[2:])
            for target in [ref] + ([band_tails[c - 2]] if band_tails and c >= 2 else []):
                if n_prev:
                    target[:n_prev] = prev[c][...]
                target[n_prev, 0] = new
        else:
            ref[...] = _dot_nt(h, w_t)


def _qkv_kernel(x_ref, g_ref, wq_ref, wkvt_ref, *outs, per_head_t):
    h = (_rms_unit(x_ref[...]) * g_ref[...]).astype(BF16)
    _store_qkv(h, wq_ref, wkvt_ref, outs, per_head_t)


def _layer_tail(x_ref, a_ref, ob_ref, p_ref, wo_ref, gf_ref, wg_ref, wu_ref, wd_ref, gp_ref, wpg_ref, wpp_ref):
    w = a_ref.shape[-1]
    x = x_ref[...] + _dot(a_ref[...], wo_ref[:w, :]) + _dot(ob_ref[...], wo_ref[w:, :])
    h = (_rms_unit(x) * gf_ref[...]).astype(BF16)
    gate = _dot(h, wg_ref[...])
    up = _dot(h, wu_ref[...])
    act = (gate * jax.nn.sigmoid(gate) * up).astype(BF16)
    x = x + _dot(act, wd_ref[...])
    hp = (_rms_unit(x) * gp_ref[...]).astype(BF16)
    ple_gate = jax.nn.sigmoid(_dot(hp, wpg_ref[...]))
    return x + ple_gate * _dot(p_ref[...].astype(BF16), wpp_ref[...])


N_TAIL = 12


def _mid_layer_kernel(*refs, per_head_t, n_prev):
    tail, (gn_ref, wq_ref, wkvt_ref) = refs[:N_TAIL], refs[N_TAIL:N_TAIL + 3]
    prev, xo_ref, outs = refs[N_TAIL + 3:N_TAIL + 3 + n_prev], refs[N_TAIL + 3 + n_prev], refs[N_TAIL + 4 + n_prev:]
    x = _layer_tail(*tail)
    xo_ref[...] = x
    _store_qkv((_rms_unit(x) * gn_ref[...]).astype(BF16), wq_ref, wkvt_ref, outs, per_head_t, prev)


def _last_layer_kernel(*refs):
    tail, (gn_ref, y_ref) = refs[:N_TAIL], refs[N_TAIL:]
    y_ref[...] = _rms_unit(_layer_tail(*tail)) * gn_ref[...]


def _row_spec(tm, width):
    return pl.BlockSpec((tm, width), lambda i: (i, 0))


def _kv_spec(layers, w, tm, seq):
    heads, tiles = w // HEAD_DIM, seq // tm
    return pl.BlockSpec((layers, 1, heads, HEAD_DIM, tm), lambda i: (0, i // tiles, 0, 0, i % tiles))


def _qkv_out(rows, w, tm, seq, layers):
    q_shapes = [jax.ShapeDtypeStruct((rows, w), BF16)] * 2
    q_specs = [_row_spec(tm, w)] * 2
    if seq is None:
        return q_shapes + [jax.ShapeDtypeStruct((rows, w), F32)] * 4, q_specs + [_row_spec(tm, w)] * 4
    kv_shape = jax.ShapeDtypeStruct((layers, rows // seq, w // HEAD_DIM, HEAD_DIM, seq), F32)
    return q_shapes + [kv_shape] * 4, q_specs + [_kv_spec(layers, w, tm, seq)] * 4


def _qkv_call(x, wts, layer, tm, seq):
    rows, d = x.shape
    wq, wkvt, g = wts["wq"], wts["wkvt"], wts["g_mix"]
    out_shape, out_specs = _qkv_out(rows, wq.shape[2] // 2, tm, seq, layer + 1)
    return pl.pallas_call(
        functools.partial(_qkv_kernel, per_head_t=seq is not None), grid=(rows // tm,),
        in_specs=[_row_spec(tm, d), _resident(g.shape[1:], layer), _resident(wq.shape[1:], layer),
                  _resident(wkvt.shape[1:], layer)],
        out_specs=out_specs, out_shape=out_shape,
        compiler_params=_params("parallel"), name="qkv_proj",
    )(x, g, wq, wkvt)


TAIL_WEIGHTS = ("w_out", "g_ffn", "w_gate", "w_up", "w_down", "g_ple", "w_ple_gate", "w_ple_proj")


def _layer_call(x, a, ob, p, wts, layer, g_final, tm, seq, kv_prev=(), band_tail=False):
    rows, d = x.shape
    last = layer == wts["w_out"].shape[0] - 1
    tail_in = [x, a, ob, p] + [wts[k] for k in TAIL_WEIGHTS]
    tail_specs = ([_row_spec(tm, d), _row_spec(tm, a.shape[1]), _row_spec(tm, ob.shape[1]),
                   pl.BlockSpec((None, tm, p.shape[2]), lambda i: (layer, i, 0))]
                  + [_resident(wts[k].shape[1:], layer) for k in TAIL_WEIGHTS])
    if last:
        return pl.pallas_call(
            _last_layer_kernel, grid=(rows // tm,), in_specs=tail_specs + [_resident(g_final.shape)],
            out_specs=_row_spec(tm, d), out_shape=jax.ShapeDtypeStruct((rows, d), F32),
            compiler_params=_params("parallel"), name="layer_tail_final",
        )(*tail_in, g_final)
    nxt = [wts["g_mix"], wts["wq"], wts["wkvt"]]
    w = wts["wq"].shape[2] // 2
    out_shape, out_specs = _qkv_out(rows, w, tm, seq, layer + 2)
    if band_tail:
        tiles, first_tail = seq // tm, (seq - BAND) // tm
        out_shape += [jax.ShapeDtypeStruct(out_shape[-1].shape[:-1] + (BAND,), F32)] * 2
        out_specs += [pl.BlockSpec(out_specs[-1].block_shape,
                                   lambda i: (0, i // tiles, 0, 0, jnp.maximum(i % tiles - first_tail, 0)))] * 2
    return pl.pallas_call(
        functools.partial(_mid_layer_kernel, per_head_t=seq is not None, n_prev=len(kv_prev)), grid=(rows // tm,),
        in_specs=(tail_specs + [_resident(t.shape[1:], layer + 1) for t in nxt]
                  + [_kv_spec(layer + 1, w, tm, seq) for _ in kv_prev]),
        out_specs=[_row_spec(tm, d)] + out_specs,
        out_shape=[jax.ShapeDtypeStruct((rows, d), F32)] + out_shape,
        compiler_params=_params("arbitrary" if band_tail else "parallel"), name="layer_tail_qkv",
    )(*tail_in, *nxt, *kv_prev)


def _suffix_matrix(tk):
    j = lax.broadcasted_iota(jnp.int32, (tk, 2 * tk), 0)
    c = lax.broadcasted_iota(jnp.int32, (tk, 2 * tk), 1)
    return jnp.where((c >= tk) | (j > c), -1.0, 0.0).astype(BF16)


def _sb_weights(z, suffix, carry_ref, causal, row0=0):
    tk = suffix.shape[0]
    sp = _softplus(z)
    w_parts = []
    for h in range(z.shape[1] // tk):
        cols = slice(h * tk, (h + 1) * tk)
        z_h, sp_h = z[:, cols], sp[:, cols]
        drop = sp_h if causal is None else jnp.where(causal, sp_h, 0.0)
        sums = _dot(drop.astype(BF16), suffix)
        carry = carry_ref[row0:, cols]
        log_w = (z_h - sp_h) + (sums[:, :tk] + carry)
        w = jnp.exp(log_w)
        if causal is not None:
            w = jnp.where(causal, w, 0.0)
        carry_ref[row0:, cols] = carry + sums[:, tk:]
        w_parts.append(w.astype(BF16))
    return jnp.concatenate(w_parts, axis=1)


def _sb_kernel(q_ref, kt_ref, vt_ref, suffix_ref, g_ref, o_ref, kk_ref, vv_ref, carry_ref, acc_ref, z_ref, w_ref,
               more_ref):
    tq, tk = SB_TQ, SB_TK
    per_q = tq // tk
    i = pl.program_id(2)

    @pl.when(i == 0)
    def _():
        kk_ref[...] = jnp.zeros_like(kk_ref)
        vv_ref[...] = jnp.zeros_like(vv_ref)
        for j in range(kk_ref.shape[0]):
            for h in range(2):
                rows, cols = slice(h * HEAD_DIM, (h + 1) * HEAD_DIM), slice(h * tk, (h + 1) * tk)
                kk_ref[j, rows, cols] = kt_ref[0, h, :, j * tk:(j + 1) * tk].astype(BF16)
                vv_ref[j, rows, cols] = vt_ref[0, h, :, j * tk:(j + 1) * tk].astype(BF16)

    carry_ref[...] = jnp.zeros_like(carry_ref)
    acc_ref[...] = jnp.zeros_like(acc_ref)
    suffix = suffix_ref[...]
    k_off = lax.broadcasted_iota(jnp.int32, (tq, tk), 1)
    row = lax.broadcasted_iota(jnp.int32, (tq, tk), 0)
    last_slot = (per_q - 1) % 2

    def row0(n):
        return (per_q - 1 - n) * tk if isinstance(n, int) and 0 <= n < per_q else 0

    class Chain:
        def __init__(self, c):
            self.c = c
            self.qi = i * SB_CHAINS + c
            self.rows = slice(c * tq, (c + 1) * tq)
            self.q = q_ref[0, self.rows, :]
            self.newest = (self.qi + 1) * per_q - 1
            self.carry, self.acc, self.z, self.w = carry_ref.at[c], acc_ref.at[c], z_ref.at[c], w_ref.at[c]

        def step(self, n, slot, diagonal, with_next=True):
            j = self.newest - n
            r_prev, r_cur, r_next = row0(n - 1) if diagonal else 0, row0(n), row0(n + 1)
            z = self.z[slot, r_cur:, :]
            if with_next:
                self.z[1 - slot, r_next:, :] = _dot(self.q[r_next:], kk_ref[jnp.maximum(j - 1, 0)])
            if not (diagonal and n == 0):
                self.acc[r_prev:, :] += _dot_nt(self.w[1 - slot, r_prev:, :], vv_ref[j + 1])
            causal = ((j * tk + k_off) < self.qi * tq + row)[r_cur:] if diagonal else None
            self.w[slot, r_cur:, :] = _sb_weights(z, suffix, self.carry, causal, r_cur)

        def emit(self, oldest):
            o = self.acc[...] + _dot_nt(self.w[last_slot], vv_ref[oldest])
            o_ref[0, self.rows, :] = _head_pair_rms(o, g_ref[...]).astype(o_ref.dtype)

        def first_steps(self, n_steps):
            self.z[0, row0(0):, :] = _dot(self.q[row0(0):], kk_ref[self.newest])
            for n in range(n_steps):
                self.step(n, n % 2, n < per_q, with_next=n + 1 < n_steps)
            self.emit(self.newest - (n_steps - 1))
            more = jnp.logical_and(jnp.max(self.carry[...]) > SB_DEAD, self.newest >= n_steps)
            more_ref[self.c] = more.astype(jnp.int32)

        def older_steps(self):
            @pl.when(more_ref[self.c] != 0)
            def _():
                self.z[0] = _dot(self.q, kk_ref[self.newest - 2 * per_q])

                def more(state):
                    t, live = state
                    return jnp.logical_and(t < self.qi, live)

                def trip(state):
                    t, _ = state
                    for d in range(per_q):
                        self.step(per_q * (t + 1) + d, d % 2, False)
                    return t + 1, jnp.max(self.carry[...]) > SB_DEAD
                trips, _ = lax.while_loop(more, trip, (jnp.int32(1), self.qi > 0))
                self.emit(per_q * (self.qi - trips))

    chains = [Chain(c) for c in range(SB_CHAINS)]

    @pl.when(i == 0)
    def _():
        chains[0].first_steps(per_q)
        for ch in chains[1:]:
            ch.first_steps(2 * per_q)

    @pl.when(i > 0)
    def _():
        for ch in chains:
            ch.first_steps(2 * per_q)

    for ch in chains:
        ch.older_steps()


def _pair_kv_spec(s, layer):
    return pl.BlockSpec((None, 1, 2, HEAD_DIM, s), lambda bi, pi, i: (layer, bi, pi, 0, 0))


def _sb_call(q, kt, vt, suffix, g, layer):
    b, s, w = q.shape
    nkb = s // SB_TK
    step_rows = SB_CHAINS * SB_TQ
    assert (SB_TQ // SB_TK) % 2 == 0 and SB_CHAINS >= 2
    return pl.pallas_call(
        _sb_kernel, grid=(b, w // LANES, s // step_rows),
        in_specs=[pl.BlockSpec((1, step_rows, LANES), lambda bi, pi, i: (bi, i, pi)),
                  _pair_kv_spec(s, layer), _pair_kv_spec(s, layer),
                  _resident(suffix.shape), pl.BlockSpec((None, 1, LANES), lambda bi, pi, i: (layer, 0, pi))],
        out_specs=pl.BlockSpec((1, step_rows, LANES), lambda bi, pi, i: (bi, i, pi)),
        out_shape=jax.ShapeDtypeStruct((b, s, w), BF16),
        scratch_shapes=[pltpu.VMEM((nkb, LANES, 2 * SB_TK), BF16), pltpu.VMEM((nkb, LANES, 2 * SB_TK), BF16),
                        pltpu.VMEM((SB_CHAINS, SB_TQ, 2 * SB_TK), F32), pltpu.VMEM((SB_CHAINS, SB_TQ, LANES), F32),
                        pltpu.VMEM((SB_CHAINS, 2, SB_TQ, 2 * SB_TK), F32),
                        pltpu.VMEM((SB_CHAINS, 2, SB_TQ, 2 * SB_TK), BF16), pltpu.SMEM((SB_CHAINS,), jnp.int32)],
        compiler_params=_params("parallel", "parallel", "arbitrary"), name="sb_attention",
    )(q, kt, vt, suffix, g)


def _softmax_unnormalised(s):
    e = jnp.exp(s - jnp.max(s, axis=-1, keepdims=True))
    return e, jnp.sum(e, axis=-1, keepdims=True)


def _cb_kernel(q_ref, qn_ref, kt_ref, vt_ref, bias_ref, g_ref, o_ref, kk_ref, vv_ref, s_ref):
    i = pl.program_id(2)
    lead = BAND // CB_TQ
    n_window = CB_TK // CB_TQ
    last_block = kk_ref.shape[1] - n_window

    def window(ref, h, block):
        return jnp.concatenate([ref[h, block + c] for c in range(n_window)], axis=1)

    def put_scores(slot, q, block):
        for h in range(2):
            s_ref[slot, h] = _dot(q, window(kk_ref, h, block))

    @pl.when(i == 0)
    def _():
        kk_ref[...] = jnp.zeros_like(kk_ref)
        vv_ref[...] = jnp.zeros_like(vv_ref)
        for h in range(2):
            rows = slice(h * HEAD_DIM, (h + 1) * HEAD_DIM)
            for c in range(kk_ref.shape[1] - lead):
                kk_ref[h, lead + c, rows, :] = kt_ref[0, h, :, c * CB_TQ:(c + 1) * CB_TQ].astype(BF16)
                vv_ref[h, lead + c, rows, :] = vt_ref[0, h, :, c * CB_TQ:(c + 1) * CB_TQ].astype(BF16)
        put_scores(0, q_ref[0, :CB_TQ, :], 0)

    key = lax.broadcasted_iota(jnp.int32, (CB_TQ, CB_TK), 1)
    first = lax.broadcasted_iota(jnp.int32, (CB_TQ, LANES), 1) < HEAD_DIM
    for g in range(CB_GROUPS):
        block = i * CB_GROUPS + g
        rows = slice(g * CB_TQ, (g + 1) * CB_TQ)
        if g + 1 < CB_GROUPS:
            put_scores((g + 1) % 2, q_ref[0, (g + 1) * CB_TQ:(g + 2) * CB_TQ, :], block + 1)
        else:
            put_scores((g + 1) % 2, qn_ref[0], jnp.minimum(block + 1, last_block))
        o = None
        for h in range(2):
            s = s_ref[g % 2, h] + bias_ref[h]
            if g < lead:
                s = jnp.where(key >= BAND - block * CB_TQ, s, NEG)
            e, l = _softmax_unnormalised(s)
            o_h = _dot_nt(e.astype(BF16), window(vv_ref, h, block)) / l
            o = o_h if h == 0 else jnp.where(first, o, o_h)
        o_ref[0, rows, :] = _head_pair_rms(o, g_ref[...]).astype(o_ref.dtype)


def _cb_call(q, kt, vt, bias, g, layer):
    b, s, w = q.shape
    blocks = (BAND + s) // CB_TQ
    step_rows = CB_GROUPS * CB_TQ
    n_q = s // CB_TQ
    return pl.pallas_call(
        _cb_kernel, grid=(b, w // LANES, s // step_rows),
        in_specs=[pl.BlockSpec((1, step_rows, LANES), lambda bi, pi, i: (bi, i, pi)),
                  pl.BlockSpec((1, CB_TQ, LANES), lambda bi, pi, i: (bi, jnp.minimum((i + 1) * CB_GROUPS, n_q - 1), pi)),
                  _pair_kv_spec(s, layer), _pair_kv_spec(s, layer),
                  pl.BlockSpec((2, CB_TQ, CB_TK), lambda bi, pi, i: (pi, 0, 0)),
                  pl.BlockSpec((None, 1, LANES), lambda bi, pi, i: (layer, 0, pi))],
        out_specs=pl.BlockSpec((1, step_rows, LANES), lambda bi, pi, i: (bi, i, pi)),
        out_shape=jax.ShapeDtypeStruct((b, s, w), BF16),
        scratch_shapes=[pltpu.VMEM((2, blocks, LANES, CB_TQ), BF16), pltpu.VMEM((2, blocks, LANES, CB_TQ), BF16),
                        pltpu.VMEM((2, 2, CB_TQ, CB_TK), F32)],
        compiler_params=_params("parallel", "parallel", "arbitrary"), name="cb_attention",
    )(q, q, kt, vt, bias, g)


def _bias_tiles_kernel(table_ref, o_ref, *, heads):
    layer, head = pl.program_id(0), pl.program_id(1)
    n_rel = 2 * REL_MAX + 1
    shape = o_ref.shape[2:]
    dist = (2 * CHUNK + lax.broadcasted_iota(jnp.int32, shape, 0) - lax.broadcasted_iota(jnp.int32, shape, 1))
    idx = jnp.clip(dist, -REL_MAX, REL_MAX) + REL_MAX

    def pick(r, acc):
        return jnp.where(idx == r, table_ref[(layer * n_rel + r) * heads + head], acc)
    lowest = REL_MAX + 2 * CHUNK - (shape[1] - 1)
    o_ref[0, 0] = lax.fori_loop(max(lowest, 0), n_rel, pick, jnp.zeros(shape, F32), unroll=8)


def _bias_tiles(rel_table):
    n_layers, _, heads = rel_table.shape
    return pl.pallas_call(
        functools.partial(_bias_tiles_kernel, heads=heads), grid=(n_layers, heads),
        in_specs=[pl.BlockSpec(memory_space=pltpu.SMEM)],
        out_specs=pl.BlockSpec((1, 1, CHUNK, 3 * CHUNK), lambda l, h: (l, h, 0, 0)),
        out_shape=jax.ShapeDtypeStruct((n_layers, heads, CHUNK, 3 * CHUNK), F32),
        compiler_params=_params("parallel", "parallel"), name="rel_bias_tiles",
    )(rel_table.reshape(-1))


def _prompt_bias(tiles, far):
    heads = tiles.shape[0]
    qc, kc = CB_TQ // CHUNK, CB_TK // CHUNK
    far_tile = jnp.broadcast_to(far[:, None, None], (heads, CHUNK, CHUNK))
    hidden = jnp.full((heads, CHUNK, CHUNK), NEG, F32)
    rows = []
    for c in range(qc):
        row = []
        for k in range(kc):
            m = k - c
            if m < 0 or m > BAND_CHUNKS:
                row.append(hidden)
            elif m < BAND_CHUNKS - 2:
                row.append(far_tile)
            else:
                t = m - (BAND_CHUNKS - 2)
                row.append(tiles[:, :, t * CHUNK:(t + 1) * CHUNK])
        rows.append(jnp.concatenate(row, axis=2))
    return jnp.concatenate(rows, axis=1)


def _head_rows(x, heads):
    t, w = x.shape
    tiled = jnp.concatenate([x] * heads, axis=0)
    row_head = lax.div(lax.broadcasted_iota(jnp.int32, tiled.shape, 0), t)
    col_head = lax.div(lax.broadcasted_iota(jnp.int32, tiled.shape, 1), HEAD_DIM)
    return jnp.where(row_head == col_head, tiled, jnp.zeros_like(tiled))


def _fold_head_rows(o, heads):
    t = o.shape[0] // heads
    col_head = lax.div(lax.broadcasted_iota(jnp.int32, (t, o.shape[1]), 1), HEAD_DIM)
    out = jnp.zeros((t, o.shape[1]), F32)
    for h in range(heads):
        out = jnp.where(col_head == h, o[h * t:(h + 1) * t], out)
    return out


def _all_heads_rms(o, g):
    parts = [_head_pair_rms(o[:, c:c + LANES], g[:, c:c + LANES]) for c in range(0, o.shape[1], LANES)]
    return jnp.concatenate(parts, axis=1)


def _pad_rows(x, rows):
    return jnp.concatenate([x, jnp.zeros((rows - x.shape[0], x.shape[1]), x.dtype)], axis=0)


def _sample_kernel(qa_ref, ka_ref, va_ref, ckt_ref, cvt_ref, qb_ref, kb_ref, vb_ref, ckbt_ref, cvbt_ref,
                   suffix_ref, bias_ref, ga_ref, gb_ref, oa_ref, ob_ref, carry_ref, acc_ref, live_ref):
    t, w = qa_ref.shape[1:]
    heads = w // HEAD_DIM
    tk = SB_TK
    past = ckt_ref.shape[2]

    carry_ref[...] = jnp.zeros_like(carry_ref)
    q = _head_rows(qa_ref[0], heads)
    suffix = suffix_ref[...]
    frame = lax.rem(lax.broadcasted_iota(jnp.int32, (heads * t, tk), 0), t)
    key = lax.broadcasted_iota(jnp.int32, (heads * t, tk), 1)
    wts = _sb_weights(_dot_nt(q, _pad_rows(ka_ref[0], tk).astype(BF16)), suffix, carry_ref, key < frame)
    acc_ref[...] = _dot(wts, _pad_rows(va_ref[0], tk).astype(BF16))
    def cached(blk):
        cols = slice(blk * tk, (blk + 1) * tk)
        wts = _sb_weights(_dot(q, ckt_ref[0, :, cols].astype(BF16)), suffix, carry_ref, None)
        acc_ref[...] += _dot_nt(wts, cvt_ref[0, :, cols].astype(BF16))

    live_ref[0] = jnp.int32(1)
    newest_first = list(reversed(range(past // tk)))
    for first in range(0, len(newest_first), 2):
        @pl.when(live_ref[0] != 0)
        def _(first=first):
            for blk in newest_first[first:first + 2]:
                cached(blk)
            live_ref[0] = (jnp.max(carry_ref[...]) > SB_DEAD).astype(jnp.int32)
    oa_ref[0] = _all_heads_rms(_fold_head_rows(acc_ref[...], heads), ga_ref[...]).astype(oa_ref.dtype)

    qb = _head_rows(qb_ref[0], heads)
    s = jnp.concatenate([_dot(qb, ckbt_ref[0].astype(BF16)),
                         _dot_nt(qb, _pad_rows(kb_ref[0], LANES).astype(BF16))], axis=1)
    e, l = _softmax_unnormalised(s + bias_ref[...])
    e = e.astype(BF16)
    band = ckbt_ref.shape[2]
    o = _dot_nt(e[:, :band], cvbt_ref[0].astype(BF16)) + _dot(e[:, band:], _pad_rows(vb_ref[0], LANES).astype(BF16))
    ob_ref[0] = _all_heads_rms(_fold_head_rows(o / l, heads), gb_ref[...]).astype(ob_ref.dtype)


def _sample_call(qa, ka, va, ckt, cvt, qb, kb, vb, ckbt, cvbt, suffix, bias, ga, gb, layer):
    b, t, w = qa.shape
    heads = w // HEAD_DIM
    new = pl.BlockSpec((1, t, w), lambda bi: (bi, 0, 0))

    def cache(c):
        return pl.BlockSpec((None, 1) + c.shape[2:], lambda bi: (layer, bi, 0, 0))
    return pl.pallas_call(
        _sample_kernel, grid=(b,),
        in_specs=[new, new, new, cache(ckt), cache(cvt), new, new, new, cache(ckbt), cache(cvbt),
                  _resident(suffix.shape), _resident(bias.shape), _resident(ga.shape[1:], layer),
                  _resident(gb.shape[1:], layer)],
        out_specs=[new, new], out_shape=[jax.ShapeDtypeStruct((b, t, w), BF16)] * 2,
        scratch_shapes=[pltpu.VMEM((heads * t, SB_TK), F32), pltpu.VMEM((heads * t, w), F32),
                        pltpu.SMEM((1,), jnp.int32)],
        compiler_params=_params("parallel"), name="sample_attention",
    )(qa, ka, va, ckt, cvt, qb, kb, vb, ckbt, cvbt, suffix, bias, ga, gb)


def _sample_bias(tiles, far, t):
    heads = tiles.shape[0]
    near = BAND - 2 * CHUNK
    parts = [jnp.broadcast_to(far[:, None, None], (heads, t, near)), tiles[:, :t, :2 * CHUNK + t],
             jnp.full((heads, t, LANES - t), NEG, F32)]
    return jnp.concatenate(parts, axis=2).reshape(heads * t, BAND + LANES)


def _per_head_t(cache):
    n_l, n_b, n_p, n_h, n_d = cache.shape
    return jnp.transpose(cache, (0, 1, 3, 4, 2)).reshape(n_l, n_b, n_h * n_d, n_p)


def kernel(x_prompt, x_sample, p_prompt, p_sample, cache_sb_k, cache_sb_v, cache_cb_k, cache_cb_v, g_mix, w_in,
           rel_table, g_out_sb, g_out_cb, w_out, g_ffn, w_gate, w_up, w_down, g_ple, w_ple_gate, w_ple_proj,
           g_final):
    depth = w_in.shape[0]
    b, s, d = x_prompt.shape
    bs, t, _ = x_sample.shape
    w = g_out_sb.shape[1]
    heads = w // HEAD_DIM
    past, band = cache_sb_k.shape[2], cache_cb_k.shape[2]
    assert g_out_cb.shape[1] == w and w_in.shape[2] == 6 * w and w % LANES == 0
    assert s % (SB_CHAINS * SB_TQ) == 0 and s % (CB_GROUPS * CB_TQ) == 0 and s >= BAND and s % ROW_TILE == 0 and s % QKV_TILE == 0
    assert band == BAND and past % SB_TK == 0 and t <= CHUNK and t % 8 == 0

    gain = lambda g: g[:, None, :]
    wts = dict(
        g_mix=gain(g_mix), g_ffn=gain(g_ffn), g_ple=gain(g_ple),
        wq=jnp.concatenate([w_in[:, :, :w], w_in[:, :, 3 * w:4 * w]], axis=2).astype(BF16),
        wkvt=jnp.swapaxes(jnp.concatenate([w_in[:, :, w:3 * w], w_in[:, :, 4 * w:]], axis=2), 1, 2).astype(BF16),
        w_out=w_out.astype(BF16), w_gate=w_gate.astype(BF16), w_up=w_up.astype(BF16), w_down=w_down.astype(BF16),
        w_ple_gate=w_ple_gate.astype(BF16), w_ple_proj=w_ple_proj.astype(BF16))
    g_sb, g_cb = gain(g_out_sb), gain(g_out_cb)
    g_fin = g_final.reshape(1, d)
    suffix = _suffix_matrix(SB_TK)
    tiles = _bias_tiles(rel_table)
    far = rel_table[:, 2 * REL_MAX, :]
    caches = [_per_head_t(c) for c in (cache_sb_k, cache_sb_v, cache_cb_k, cache_cb_v)]
    pp = p_prompt.reshape(depth, b * s, -1)
    ps = p_sample.reshape(depth, bs * t, -1)

    xp = x_prompt.reshape(b * s, d)
    xs = x_sample.reshape(bs * t, d)
    qkv_p = _qkv_call(xp, wts, 0, QKV_TILE, s)
    qkv_s = _qkv_call(xs, wts, 0, bs * t, None)
    new_s = []
    for i in range(depth):
        qa, qb = [u.reshape(b, s, w) for u in qkv_p[:2]]
        kat, vat, kbt, vbt = kv_all = qkv_p[2:]
        a = _sb_call(qa, kat, vat, suffix, g_sb, i)
        ob = _cb_call(qb, kbt, vbt, _prompt_bias(tiles[i], far[i]), g_cb, i)

        sqa, sqb, ska, sva, skb, svb = [u.reshape(bs, t, w) for u in qkv_s]
        new_s.append((ska, sva, skb, svb))
        sa, sob = _sample_call(sqa, ska, sva, caches[0], caches[1], sqb, skb, svb, caches[2], caches[3],
                               suffix, _sample_bias(tiles[i], far[i], t), g_sb, g_cb, i)

        out_p = _layer_call(xp, a.reshape(b * s, w), ob.reshape(b * s, w), pp, wts, i, g_fin, ROW_TILE, s, kv_all,
                            band_tail=i + 2 == depth)
        out_s = _layer_call(xs, sa.reshape(bs * t, w), sob.reshape(bs * t, w), ps, wts, i, g_fin, bs * t, None)
        if i == depth - 1:
            xp, xs = out_p, out_s
        else:
            xp, qkv_p, band_tails = out_p[0], out_p[1:7], out_p[7:]
            xs, qkv_s = out_s[0], out_s[1:]
    if depth == 1:
        band_tails = [kbt[..., s - BAND:], vbt[..., s - BAND:]]

    def positions_major(kv_t):
        return jnp.transpose(kv_t, (0, 1, 4, 2, 3))

    def stack(items):
        return jnp.stack(items).reshape(depth, bs, t, heads, HEAD_DIM)

    return (xp.reshape(b, s, d), xs.reshape(bs, t, d),
            positions_major(kat), positions_major(vat),
            positions_major(band_tails[0]), positions_major(band_tails[1]),
            stack([e[0] for e in new_s]), stack([e[1] for e in new_s]),
            stack([e[2] for e in new_s]), stack([e[3] for e in new_s]))
```

```python
import functools

import jax
import jax.numpy as jnp
from jax import lax
from jax.experimental import pallas as pl
from jax.experimental.pallas import tpu as pltpu

F32 = jnp.float32
BF16 = jnp.bfloat16

HEAD_DIM = 64
CHUNK = 64
BAND_CHUNKS = 8
BAND = BAND_CHUNKS * CHUNK
REL_MAX = 128
EPS = 1e-6
Q_SCALE = HEAD_DIM ** -0.5

LANES = 128
NEG = -1e30

ROW_TILE = 256
QKV_TILE = 512
SB_TQ = 256
SB_TK = 128
SB_CHAINS = 4
SB_DEAD = -120.0
CB_TQ = 256
CB_TK = CB_TQ + BAND
CB_GROUPS = 8
VMEM_LIMIT = 58 * 1024 * 1024


def _params(*semantics):
    return pltpu.CompilerParams(dimension_semantics=semantics, vmem_limit_bytes=VMEM_LIMIT)


def _resident(shape, layer=None):
    zeros = (0,) * len(shape)
    if layer is None:
        return pl.BlockSpec(shape, lambda *_: zeros, pipeline_mode=pl.Buffered(1))
    return pl.BlockSpec((None,) + tuple(shape), lambda *_: (layer,) + zeros, pipeline_mode=pl.Buffered(1))


def _rms_unit(x):
    return x * lax.rsqrt(jnp.mean(x * x, axis=-1, keepdims=True) + EPS)


def _dot(a, b):
    return jnp.dot(a, b, preferred_element_type=F32)


def _dot_nt(a, b):
    return lax.dot_general(a, b, (((1,), (1,)), ((), ())), preferred_element_type=F32)


def _softplus(z):
    return jnp.maximum(z, 0.0) + jnp.log(1.0 + jnp.exp(-jnp.abs(z)))


def _head_pair_rms(o, g):
    first = lax.broadcasted_iota(jnp.int32, o.shape, 1) < HEAD_DIM
    o2 = o * o
    s0 = jnp.sum(jnp.where(first, o2, 0.0), axis=-1, keepdims=True)
    s1 = jnp.sum(jnp.where(first, 0.0, o2), axis=-1, keepdims=True)
    ms = jnp.where(first, s0, s1) * (1.0 / HEAD_DIM)
    return o * lax.rsqrt(ms + EPS) * g


def _store_qkv(h, wt_ref, outs, per_head_t, prev=()):
    qa_ref, qb_ref = outs[:2]
    w = qa_ref.shape[-1]
    qa_ref[...] = (_dot_nt(h, wt_ref[0:w, :]) * Q_SCALE).astype(BF16)
    qb_ref[...] = (_dot_nt(h, wt_ref[3 * w:4 * w, :]) * Q_SCALE).astype(BF16)
    band_tails = outs[6:]
    for c, ref in enumerate(outs[2:6]):
        first_row = (1 + c + c // 2) * w
        w_t = wt_ref[first_row:first_row + w, :]
        if per_head_t:
            n_prev = ref.shape[0] - 1
            new = _dot_nt(w_t, h).reshape(ref.shape[2:])
            for target in [ref] + ([band_tails[c - 2]] if band_tails and c >= 2 else []):
                if n_prev:
                    target[:n_prev] = prev[c][...]
                target[n_prev, 0] = new
        else:
            ref[...] = _dot_nt(h, w_t)


def _qkv_kernel(x_ref, g_ref, wt_ref, *outs, per_head_t):
    h = (_rms_unit(x_ref[...]) * g_ref[...]).astype(BF16)
    _store_qkv(h, wt_ref, outs, per_head_t)


def _layer_tail(x_ref, a_ref, ob_ref, p_ref, wo_ref, gf_ref, wg_ref, wu_ref, wd_ref, gp_ref, wpg_ref, wpp_ref):
    w = a_ref.shape[-1]
    x = x_ref[...] + _dot(a_ref[...], wo_ref[:w, :]) + _dot(ob_ref[...], wo_ref[w:, :])
    h = (_rms_unit(x) * gf_ref[...]).astype(BF16)
    gate = _dot(h, wg_ref[...])
    up = _dot(h, wu_ref[...])
    act = (gate * jax.nn.sigmoid(gate) * up).astype(BF16)
    x = x + _dot(act, wd_ref[...])
    hp = (_rms_unit(x) * gp_ref[...]).astype(BF16)
    ple_gate = jax.nn.sigmoid(_dot(hp, wpg_ref[...]))
    return x + ple_gate * _dot(p_ref[...].astype(BF16), wpp_ref[...])


N_TAIL = 12


def _mid_layer_kernel(*refs, per_head_t, n_prev):
    tail, (gn_ref, wt_ref) = refs[:N_TAIL], refs[N_TAIL:N_TAIL + 2]
    prev, xo_ref, outs = refs[N_TAIL + 2:N_TAIL + 2 + n_prev], refs[N_TAIL + 2 + n_prev], refs[N_TAIL + 3 + n_prev:]
    x = _layer_tail(*tail)
    xo_ref[...] = x
    _store_qkv((_rms_unit(x) * gn_ref[...]).astype(BF16), wt_ref, outs, per_head_t, prev)


def _last_layer_kernel(*refs):
    tail, (gn_ref, y_ref) = refs[:N_TAIL], refs[N_TAIL:]
    y_ref[...] = _rms_unit(_layer_tail(*tail)) * gn_ref[...]


def _row_spec(tm, width):
    return pl.BlockSpec((tm, width), lambda i: (i, 0))


def _kv_spec(layers, w, tm, seq):
    heads, tiles = w // HEAD_DIM, seq // tm
    return pl.BlockSpec((layers, 1, heads, HEAD_DIM, tm), lambda i: (0, i // tiles, 0, 0, i % tiles))


def _qkv_out(rows, w, tm, seq, layers):
    q_shapes = [jax.ShapeDtypeStruct((rows, w), BF16)] * 2
    q_specs = [_row_spec(tm, w)] * 2
    if seq is None:
        return q_shapes + [jax.ShapeDtypeStruct((rows, w), F32)] * 4, q_specs + [_row_spec(tm, w)] * 4
    kv_shape = jax.ShapeDtypeStruct((layers, rows // seq, w // HEAD_DIM, HEAD_DIM, seq), F32)
    return q_shapes + [kv_shape] * 4, q_specs + [_kv_spec(layers, w, tm, seq)] * 4


def _qkv_call(x, wts, layer, tm, seq):
    rows, d = x.shape
    wt, g = wts["w_in_t"], wts["g_mix"]
    out_shape, out_specs = _qkv_out(rows, wt.shape[1] // 6, tm, seq, layer + 1)
    return pl.pallas_call(
        functools.partial(_qkv_kernel, per_head_t=seq is not None), grid=(rows // tm,),
        in_specs=[_row_spec(tm, d), _resident(g.shape[1:], layer), _resident(wt.shape[1:], layer)],
        out_specs=out_specs, out_shape=out_shape,
        compiler_params=_params("parallel"), name="qkv_proj",
    )(x, g, wt)


TAIL_WEIGHTS = ("w_out", "g_ffn", "w_gate", "w_up", "w_down", "g_ple", "w_ple_gate", "w_ple_proj")


def _layer_call(x, a, ob, p, wts, layer, g_final, tm, seq, kv_prev=(), band_tail=False):
    rows, d = x.shape
    last = layer == wts["w_out"].shape[0] - 1
    tail_in = [x, a, ob, p] + [wts[k] for k in TAIL_WEIGHTS]
    tail_specs = ([_row_spec(tm, d), _row_spec(tm, a.shape[1]), _row_spec(tm, ob.shape[1]),
                   pl.BlockSpec((None, tm, p.shape[2]), lambda i: (layer, i, 0))]
                  + [_resident(wts[k].shape[1:], layer) for k in TAIL_WEIGHTS])
    if last:
        return pl.pallas_call(
            _last_layer_kernel, grid=(rows // tm,), in_specs=tail_specs + [_resident(g_final.shape)],
            out_specs=_row_spec(tm, d), out_shape=jax.ShapeDtypeStruct((rows, d), F32),
            compiler_params=_params("parallel"), name="layer_tail_final",
        )(*tail_in, g_final)
    nxt = [wts["g_mix"], wts["w_in_t"]]
    w = wts["w_in_t"].shape[1] // 6
    out_shape, out_specs = _qkv_out(rows, w, tm, seq, layer + 2)
    if band_tail:
        tiles, first_tail = seq // tm, (seq - BAND) // tm
        out_shape += [jax.ShapeDtypeStruct(out_shape[-1].shape[:-1] + (BAND,), F32)] * 2
        out_specs += [pl.BlockSpec(out_specs[-1].block_shape,
                                   lambda i: (0, i // tiles, 0, 0, jnp.maximum(i % tiles - first_tail, 0)))] * 2
    return pl.pallas_call(
        functools.partial(_mid_layer_kernel, per_head_t=seq is not None, n_prev=len(kv_prev)), grid=(rows // tm,),
        in_specs=(tail_specs + [_resident(t.shape[1:], layer + 1) for t in nxt]
                  + [_kv_spec(layer + 1, w, tm, seq) for _ in kv_prev]),
        out_specs=[_row_spec(tm, d)] + out_specs,
        out_shape=[jax.ShapeDtypeStruct((rows, d), F32)] + out_shape,
        compiler_params=_params("arbitrary" if band_tail else "parallel"), name="layer_tail_qkv",
    )(*tail_in, *nxt, *kv_prev)


def _suffix_matrix(tk):
    j = lax.broadcasted_iota(jnp.int32, (tk, 2 * tk), 0)
    c = lax.broadcasted_iota(jnp.int32, (tk, 2 * tk), 1)
    return jnp.where((c >= tk) | (j > c), -1.0, 0.0).astype(BF16)


def _sb_weights(z, suffix, carry_ref, causal, row0=0):
    tk = suffix.shape[0]
    sp = _softplus(z)
    w_parts = []
    for h in range(z.shape[1] // tk):
        cols = slice(h * tk, (h + 1) * tk)
        z_h, sp_h = z[:, cols], sp[:, cols]
        drop = sp_h if causal is None else jnp.where(causal, sp_h, 0.0)
        sums = _dot(drop.astype(BF16), suffix)
        carry = carry_ref[row0:, cols]
        log_w = (z_h - sp_h) + (sums[:, :tk] + carry)
        w = jnp.exp(log_w)
        if causal is not None:
            w = jnp.where(causal, w, 0.0)
        carry_ref[row0:, cols] = carry + sums[:, tk:]
        w_parts.append(w.astype(BF16))
    return jnp.concatenate(w_parts, axis=1)


def _sb_kernel(q_ref, kt_ref, vt_ref, suffix_ref, g_ref, o_ref, kk_ref, vv_ref, carry_ref, acc_ref, z_ref, w_ref,
               more_ref):
    tq, tk = SB_TQ, SB_TK
    per_q = tq // tk
    i = pl.program_id(2)

    @pl.when(i == 0)
    def _():
        kk_ref[...] = jnp.zeros_like(kk_ref)
        vv_ref[...] = jnp.zeros_like(vv_ref)
        for j in range(kk_ref.shape[0]):
            for h in range(2):
                rows, cols = slice(h * HEAD_DIM, (h + 1) * HEAD_DIM), slice(h * tk, (h + 1) * tk)
                kk_ref[j, rows, cols] = kt_ref[0, h, :, j * tk:(j + 1) * tk].astype(BF16)
                vv_ref[j, rows, cols] = vt_ref[0, h, :, j * tk:(j + 1) * tk].astype(BF16)

    carry_ref[...] = jnp.zeros_like(carry_ref)
    acc_ref[...] = jnp.zeros_like(acc_ref)
    suffix = suffix_ref[...]
    k_off = lax.broadcasted_iota(jnp.int32, (tq, tk), 1)
    row = lax.broadcasted_iota(jnp.int32, (tq, tk), 0)
    last_slot = (per_q - 1) % 2

    def row0(n):
        return (per_q - 1 - n) * tk if isinstance(n, int) and 0 <= n < per_q else 0

    class Chain:
        def __init__(self, c):
            self.c = c
            self.qi = i * SB_CHAINS + c
            self.rows = slice(c * tq, (c + 1) * tq)
            self.q = q_ref[0, self.rows, :]
            self.newest = (self.qi + 1) * per_q - 1
            self.carry, self.acc, self.z, self.w = carry_ref.at[c], acc_ref.at[c], z_ref.at[c], w_ref.at[c]

        def step(self, n, slot, diagonal, with_next=True):
            j = self.newest - n
            r_prev, r_cur, r_next = row0(n - 1) if diagonal else 0, row0(n), row0(n + 1)
            z = self.z[slot, r_cur:, :]
            if with_next:
                self.z[1 - slot, r_next:, :] = _dot(self.q[r_next:], kk_ref[jnp.maximum(j - 1, 0)])
            if not (diagonal and n == 0):
                self.acc[r_prev:, :] += _dot_nt(self.w[1 - slot, r_prev:, :], vv_ref[j + 1])
            causal = ((j * tk + k_off) < self.qi * tq + row)[r_cur:] if diagonal else None
            self.w[slot, r_cur:, :] = _sb_weights(z, suffix, self.carry, causal, r_cur)

        def emit(self, oldest):
            o = self.acc[...] + _dot_nt(self.w[last_slot], vv_ref[oldest])
            o_ref[0, self.rows, :] = _head_pair_rms(o, g_ref[...]).astype(o_ref.dtype)

        def first_steps(self, n_steps):
            self.z[0, row0(0):, :] = _dot(self.q[row0(0):], kk_ref[self.newest])
            for n in range(n_steps):
                self.step(n, n % 2, n < per_q, with_next=n + 1 < n_steps)
            self.emit(self.newest - (n_steps - 1))
            more = jnp.logical_and(jnp.max(self.carry[...]) > SB_DEAD, self.newest >= n_steps)
            more_ref[self.c] = more.astype(jnp.int32)

        def older_steps(self):
            @pl.when(more_ref[self.c] != 0)
            def _():
                self.z[0] = _dot(self.q, kk_ref[self.newest - 2 * per_q])

                def more(state):
                    t, live = state
                    return jnp.logical_and(t < self.qi, live)

                def trip(state):
                    t, _ = state
                    for d in range(per_q):
                        self.step(per_q * (t + 1) + d, d % 2, False)
                    return t + 1, jnp.max(self.carry[...]) > SB_DEAD
                trips, _ = lax.while_loop(more, trip, (jnp.int32(1), self.qi > 0))
                self.emit(per_q * (self.qi - trips))

    chains = [Chain(c) for c in range(SB_CHAINS)]

    @pl.when(i == 0)
    def _():
        chains[0].first_steps(per_q)
        for ch in chains[1:]:
            ch.first_steps(2 * per_q)

    @pl.when(i > 0)
    def _():
        for ch in chains:
            ch.first_steps(2 * per_q)

    for ch in chains:
        ch.older_steps()


def _pair_kv_spec(s, layer):
    return pl.BlockSpec((None, 1, 2, HEAD_DIM, s), lambda bi, pi, i: (layer, bi, pi, 0, 0))


def _sb_call(q, kt, vt, suffix, g, layer):
    b, s, w = q.shape
    nkb = s // SB_TK
    step_rows = SB_CHAINS * SB_TQ
    assert (SB_TQ // SB_TK) % 2 == 0 and SB_CHAINS >= 2
    return pl.pallas_call(
        _sb_kernel, grid=(b, w // LANES, s // step_rows),
        in_specs=[pl.BlockSpec((1, step_rows, LANES), lambda bi, pi, i: (bi, i, pi)),
                  _pair_kv_spec(s, layer), _pair_kv_spec(s, layer),
                  _resident(suffix.shape), pl.BlockSpec((None, 1, LANES), lambda bi, pi, i: (layer, 0, pi))],
        out_specs=pl.BlockSpec((1, step_rows, LANES), lambda bi, pi, i: (bi, i, pi)),
        out_shape=jax.ShapeDtypeStruct((b, s, w), BF16),
        scratch_shapes=[pltpu.VMEM((nkb, LANES, 2 * SB_TK), BF16), pltpu.VMEM((nkb, LANES, 2 * SB_TK), BF16),
                        pltpu.VMEM((SB_CHAINS, SB_TQ, 2 * SB_TK), F32), pltpu.VMEM((SB_CHAINS, SB_TQ, LANES), F32),
                        pltpu.VMEM((SB_CHAINS, 2, SB_TQ, 2 * SB_TK), F32),
                        pltpu.VMEM((SB_CHAINS, 2, SB_TQ, 2 * SB_TK), BF16), pltpu.SMEM((SB_CHAINS,), jnp.int32)],
        compiler_params=_params("parallel", "parallel", "arbitrary"), name="sb_attention",
    )(q, kt, vt, suffix, g)


def _softmax_unnormalised(s):
    e = jnp.exp(s - jnp.max(s, axis=-1, keepdims=True))
    return e, jnp.sum(e, axis=-1, keepdims=True)


def _cb_kernel(q_ref, qn_ref, kt_ref, vt_ref, bias_ref, g_ref, o_ref, kk_ref, vv_ref, s_ref):
    i = pl.program_id(2)
    lead = BAND // CB_TQ
    n_window = CB_TK // CB_TQ
    last_block = kk_ref.shape[1] - n_window

    def ones_row(h):
        return (1 - h) * HEAD_DIM

    def window(ref, h, block):
        return jnp.concatenate([ref[h, block + c] for c in range(n_window)], axis=1)

    def put_scores(slot, q, block):
        for h in range(2):
            s_ref[slot, h] = _dot(q, window(kk_ref, h, block))

    @pl.when(i == 0)
    def _():
        kk_ref[...] = jnp.zeros_like(kk_ref)
        row = lax.broadcasted_iota(jnp.int32, vv_ref.shape[1:], 1)
        for h in range(2):
            vv_ref[h] = jnp.where(row == ones_row(h), 1.0, 0.0).astype(BF16)
            rows = slice(h * HEAD_DIM, (h + 1) * HEAD_DIM)
            for c in range(kk_ref.shape[1] - lead):
                kk_ref[h, lead + c, rows, :] = kt_ref[0, h, :, c * CB_TQ:(c + 1) * CB_TQ].astype(BF16)
                vv_ref[h, lead + c, rows, :] = vt_ref[0, h, :, c * CB_TQ:(c + 1) * CB_TQ].astype(BF16)
        put_scores(0, q_ref[0, :CB_TQ, :], 0)

    key = lax.broadcasted_iota(jnp.int32, (CB_TQ, CB_TK), 1)
    first = lax.broadcasted_iota(jnp.int32, (CB_TQ, LANES), 1) < HEAD_DIM
    for g in range(CB_GROUPS):
        block = i * CB_GROUPS + g
        rows = slice(g * CB_TQ, (g + 1) * CB_TQ)
        if g + 1 < CB_GROUPS:
            put_scores((g + 1) % 2, q_ref[0, (g + 1) * CB_TQ:(g + 2) * CB_TQ, :], block + 1)
        else:
            put_scores((g + 1) % 2, qn_ref[0], jnp.minimum(block + 1, last_block))
        o = None
        for h in range(2):
            s = s_ref[g % 2, h] + bias_ref[h]
            if g < lead:
                s = jnp.where(key >= BAND - block * CB_TQ, s, NEG)
            e = jnp.exp(s - jnp.max(s, axis=-1, keepdims=True)).astype(BF16)
            o_h = _dot_nt(e, window(vv_ref, h, block))
            o_h = o_h / o_h[:, ones_row(h):ones_row(h) + 1]
            o = o_h if h == 0 else jnp.where(first, o, o_h)
        o_ref[0, rows, :] = _head_pair_rms(o, g_ref[...]).astype(o_ref.dtype)


def _cb_call(q, kt, vt, bias, g, layer):
    b, s, w = q.shape
    blocks = (BAND + s) // CB_TQ
    step_rows = CB_GROUPS * CB_TQ
    n_q = s // CB_TQ
    return pl.pallas_call(
        _cb_kernel, grid=(b, w // LANES, s // step_rows),
        in_specs=[pl.BlockSpec((1, step_rows, LANES), lambda bi, pi, i: (bi, i, pi)),
                  pl.BlockSpec((1, CB_TQ, LANES), lambda bi, pi, i: (bi, jnp.minimum((i + 1) * CB_GROUPS, n_q - 1), pi)),
                  _pair_kv_spec(s, layer), _pair_kv_spec(s, layer),
                  pl.BlockSpec((2, CB_TQ, CB_TK), lambda bi, pi, i: (pi, 0, 0)),
                  pl.BlockSpec((None, 1, LANES), lambda bi, pi, i: (layer, 0, pi))],
        out_specs=pl.BlockSpec((1, step_rows, LANES), lambda bi, pi, i: (bi, i, pi)),
        out_shape=jax.ShapeDtypeStruct((b, s, w), BF16),
        scratch_shapes=[pltpu.VMEM((2, blocks, LANES, CB_TQ), BF16), pltpu.VMEM((2, blocks, LANES, CB_TQ), BF16),
                        pltpu.VMEM((2, 2, CB_TQ, CB_TK), F32)],
        compiler_params=_params("parallel", "parallel", "arbitrary"), name="cb_attention",
    )(q, q, kt, vt, bias, g)


def _bias_tiles_kernel(table_ref, o_ref, *, heads):
    layer, head = pl.program_id(0), pl.program_id(1)
    n_rel = 2 * REL_MAX + 1
    shape = o_ref.shape[2:]
    dist = (2 * CHUNK + lax.broadcasted_iota(jnp.int32, shape, 0) - lax.broadcasted_iota(jnp.int32, shape, 1))
    idx = jnp.clip(dist, -REL_MAX, REL_MAX) + REL_MAX

    def pick(r, acc):
        return jnp.where(idx == r, table_ref[(layer * n_rel + r) * heads + head], acc)
    lowest = REL_MAX + 2 * CHUNK - (shape[1] - 1)
    o_ref[0, 0] = lax.fori_loop(max(lowest, 0), n_rel, pick, jnp.zeros(shape, F32), unroll=8)


def _bias_tiles(rel_table):
    n_layers, _, heads = rel_table.shape
    return pl.pallas_call(
        functools.partial(_bias_tiles_kernel, heads=heads), grid=(n_layers, heads),
        in_specs=[pl.BlockSpec(memory_space=pltpu.SMEM)],
        out_specs=pl.BlockSpec((1, 1, CHUNK, 3 * CHUNK), lambda l, h: (l, h, 0, 0)),
        out_shape=jax.ShapeDtypeStruct((n_layers, heads, CHUNK, 3 * CHUNK), F32),
        compiler_params=_params("parallel", "parallel"), name="rel_bias_tiles",
    )(rel_table.reshape(-1))


def _prompt_bias(tiles, far):
    heads = tiles.shape[0]
    qc, kc = CB_TQ // CHUNK, CB_TK // CHUNK
    far_tile = jnp.broadcast_to(far[:, None, None], (heads, CHUNK, CHUNK))
    hidden = jnp.full((heads, CHUNK, CHUNK), NEG, F32)
    rows = []
    for c in range(qc):
        row = []
        for k in range(kc):
            m = k - c
            if m < 0 or m > BAND_CHUNKS:
                row.append(hidden)
            elif m < BAND_CHUNKS - 2:
                row.append(far_tile)
            else:
                t = m - (BAND_CHUNKS - 2)
                row.append(tiles[:, :, t * CHUNK:(t + 1) * CHUNK])
        rows.append(jnp.concatenate(row, axis=2))
    return jnp.concatenate(rows, axis=1)


def _head_rows(x, heads):
    t, w = x.shape
    tiled = jnp.concatenate([x] * heads, axis=0)
    row_head = lax.div(lax.broadcasted_iota(jnp.int32, tiled.shape, 0), t)
    col_head = lax.div(lax.broadcasted_iota(jnp.int32, tiled.shape, 1), HEAD_DIM)
    return jnp.where(row_head == col_head, tiled, jnp.zeros_like(tiled))


def _fold_head_rows(o, heads):
    t = o.shape[0] // heads
    col_head = lax.div(lax.broadcasted_iota(jnp.int32, (t, o.shape[1]), 1), HEAD_DIM)
    out = jnp.zeros((t, o.shape[1]), F32)
    for h in range(heads):
        out = jnp.where(col_head == h, o[h * t:(h + 1) * t], out)
    return out


def _all_heads_rms(o, g):
    parts = [_head_pair_rms(o[:, c:c + LANES], g[:, c:c + LANES]) for c in range(0, o.shape[1], LANES)]
    return jnp.concatenate(parts, axis=1)


def _pad_rows(x, rows):
    return jnp.concatenate([x, jnp.zeros((rows - x.shape[0], x.shape[1]), x.dtype)], axis=0)


def _sample_kernel(qa_ref, ka_ref, va_ref, ckt_ref, cvt_ref, qb_ref, kb_ref, vb_ref, ckbt_ref, cvbt_ref,
                   suffix_ref, bias_ref, ga_ref, gb_ref, oa_ref, ob_ref, carry_ref, acc_ref, live_ref):
    t, w = qa_ref.shape[1:]
    heads = w // HEAD_DIM
    tk = SB_TK
    past = ckt_ref.shape[2]

    carry_ref[...] = jnp.zeros_like(carry_ref)
    q = _head_rows(qa_ref[0], heads)
    suffix = suffix_ref[...]
    frame = lax.rem(lax.broadcasted_iota(jnp.int32, (heads * t, tk), 0), t)
    key = lax.broadcasted_iota(jnp.int32, (heads * t, tk), 1)
    wts = _sb_weights(_dot_nt(q, _pad_rows(ka_ref[0], tk).astype(BF16)), suffix, carry_ref, key < frame)
    acc_ref[...] = _dot(wts, _pad_rows(va_ref[0], tk).astype(BF16))
    def cached(blk):
        cols = slice(blk * tk, (blk + 1) * tk)
        wts = _sb_weights(_dot(q, ckt_ref[0, :, cols].astype(BF16)), suffix, carry_ref, None)
        acc_ref[...] += _dot_nt(wts, cvt_ref[0, :, cols].astype(BF16))

    live_ref[0] = jnp.int32(1)
    newest_first = list(reversed(range(past // tk)))
    for first in range(0, len(newest_first), 2):
        @pl.when(live_ref[0] != 0)
        def _(first=first):
            for blk in newest_first[first:first + 2]:
                cached(blk)
            live_ref[0] = (jnp.max(carry_ref[...]) > SB_DEAD).astype(jnp.int32)
    oa_ref[0] = _all_heads_rms(_fold_head_rows(acc_ref[...], heads), ga_ref[...]).astype(oa_ref.dtype)

    qb = _head_rows(qb_ref[0], heads)
    s = jnp.concatenate([_dot(qb, ckbt_ref[0].astype(BF16)),
                         _dot_nt(qb, _pad_rows(kb_ref[0], LANES).astype(BF16))], axis=1)
    e, l = _softmax_unnormalised(s + bias_ref[...])
    e = e.astype(BF16)
    band = ckbt_ref.shape[2]
    o = _dot_nt(e[:, :band], cvbt_ref[0].astype(BF16)) + _dot(e[:, band:], _pad_rows(vb_ref[0], LANES).astype(BF16))
    ob_ref[0] = _all_heads_rms(_fold_head_rows(o / l, heads), gb_ref[...]).astype(ob_ref.dtype)


def _sample_call(qa, ka, va, ckt, cvt, qb, kb, vb, ckbt, cvbt, suffix, bias, ga, gb, layer):
    b, t, w = qa.shape
    heads = w // HEAD_DIM
    new = pl.BlockSpec((1, t, w), lambda bi: (bi, 0, 0))

    def cache(c):
        return pl.BlockSpec((None, 1) + c.shape[2:], lambda bi: (layer, bi, 0, 0))
    return pl.pallas_call(
        _sample_kernel, grid=(b,),
        in_specs=[new, new, new, cache(ckt), cache(cvt), new, new, new, cache(ckbt), cache(cvbt),
                  _resident(suffix.shape), _resident(bias.shape), _resident(ga.shape[1:], layer),
                  _resident(gb.shape[1:], layer)],
        out_specs=[new, new], out_shape=[jax.ShapeDtypeStruct((b, t, w), BF16)] * 2,
        scratch_shapes=[pltpu.VMEM((heads * t, SB_TK), F32), pltpu.VMEM((heads * t, w), F32),
                        pltpu.SMEM((1,), jnp.int32)],
        compiler_params=_params("parallel"), name="sample_attention",
    )(qa, ka, va, ckt, cvt, qb, kb, vb, ckbt, cvbt, suffix, bias, ga, gb)


def _sample_bias(tiles, far, t):
    heads = tiles.shape[0]
    near = BAND - 2 * CHUNK
    parts = [jnp.broadcast_to(far[:, None, None], (heads, t, near)), tiles[:, :t, :2 * CHUNK + t],
             jnp.full((heads, t, LANES - t), NEG, F32)]
    return jnp.concatenate(parts, axis=2).reshape(heads * t, BAND + LANES)


def _per_head_t(cache):
    n_l, n_b, n_p, n_h, n_d = cache.shape
    return jnp.transpose(cache, (0, 1, 3, 4, 2)).reshape(n_l, n_b, n_h * n_d, n_p)


def kernel(x_prompt, x_sample, p_prompt, p_sample, cache_sb_k, cache_sb_v, cache_cb_k, cache_cb_v, g_mix, w_in,
           rel_table, g_out_sb, g_out_cb, w_out, g_ffn, w_gate, w_up, w_down, g_ple, w_ple_gate, w_ple_proj,
           g_final):
    depth = w_in.shape[0]
    b, s, d = x_prompt.shape
    bs, t, _ = x_sample.shape
    w = g_out_sb.shape[1]
    heads = w // HEAD_DIM
    past, band = cache_sb_k.shape[2], cache_cb_k.shape[2]
    assert g_out_cb.shape[1] == w and w_in.shape[2] == 6 * w and w % LANES == 0
    assert s % (SB_CHAINS * SB_TQ) == 0 and s % (CB_GROUPS * CB_TQ) == 0 and s >= BAND and s % ROW_TILE == 0 and s % QKV_TILE == 0
    assert band == BAND and past % SB_TK == 0 and t <= CHUNK and t % 8 == 0

    gain = lambda g: g[:, None, :]
    wts = dict(
        g_mix=gain(g_mix), g_ffn=gain(g_ffn), g_ple=gain(g_ple),
        w_in_t=jnp.swapaxes(w_in, 1, 2).astype(BF16),
        w_out=w_out.astype(BF16), w_gate=w_gate.astype(BF16), w_up=w_up.astype(BF16), w_down=w_down.astype(BF16),
        w_ple_gate=w_ple_gate.astype(BF16), w_ple_proj=w_ple_proj.astype(BF16))
    g_sb, g_cb = gain(g_out_sb), gain(g_out_cb)
    g_fin = g_final.reshape(1, d)
    suffix = _suffix_matrix(SB_TK)
    tiles = _bias_tiles(rel_table)
    far = rel_table[:, 2 * REL_MAX, :]
    caches = [_per_head_t(c) for c in (cache_sb_k, cache_sb_v, cache_cb_k, cache_cb_v)]
    pp = p_prompt.reshape(depth, b * s, -1)
    ps = p_sample.reshape(depth, bs * t, -1)

    xp = x_prompt.reshape(b * s, d)
    xs = x_sample.reshape(bs * t, d)
    qkv_p = _qkv_call(xp, wts, 0, QKV_TILE, s)
    qkv_s = _qkv_call(xs, wts, 0, bs * t, None)
    new_s = []
    for i in range(depth):
        qa, qb = [u.reshape(b, s, w) for u in qkv_p[:2]]
        kat, vat, kbt, vbt = kv_all = qkv_p[2:]
        a = _sb_call(qa, kat, vat, suffix, g_sb, i)
        ob = _cb_call(qb, kbt, vbt, _prompt_bias(tiles[i], far[i]), g_cb, i)

        sqa, sqb, ska, sva, skb, svb = [u.reshape(bs, t, w) for u in qkv_s]
        new_s.append((ska, sva, skb, svb))
        sa, sob = _sample_call(sqa, ska, sva, caches[0], caches[1], sqb, skb, svb, caches[2], caches[3],
                               suffix, _sample_bias(tiles[i], far[i], t), g_sb, g_cb, i)

        out_p = _layer_call(xp, a.reshape(b * s, w), ob.reshape(b * s, w), pp, wts, i, g_fin, ROW_TILE, s, kv_all,
                            band_tail=i + 2 == depth)
        out_s = _layer_call(xs, sa.reshape(bs * t, w), sob.reshape(bs * t, w), ps, wts, i, g_fin, bs * t, None)
        if i == depth - 1:
            xp, xs = out_p, out_s
        else:
            xp, qkv_p, band_tails = out_p[0], out_p[1:7], out_p[7:]
            xs, qkv_s = out_s[0], out_s[1:]
    if depth == 1:
        band_tails = [kbt[..., s - BAND:], vbt[..., s - BAND:]]

    def positions_major(kv_t):
        return jnp.transpose(kv_t, (0, 1, 4, 2, 3))

    def stack(items):
        return jnp.stack(items).reshape(depth, bs, t, heads, HEAD_DIM)

    return (xp.reshape(b, s, d), xs.reshape(bs, t, d),
            positions_major(kat), positions_major(vat),
            positions_major(band_tails[0]), positions_major(band_tails[1]),
            stack([e[0] for e in new_s]), stack([e[1] for e in new_s]),
            stack([e[2] for e in new_s]), stack([e[3] for e in new_s]))
```

```python
import functools

import jax
import jax.numpy as jnp
from jax import lax
from jax.experimental import pallas as pl
from jax.experimental.pallas import tpu as pltpu

F32 = jnp.float32
BF16 = jnp.bfloat16

HEAD_DIM = 64
CHUNK = 64
BAND_CHUNKS = 8
BAND = BAND_CHUNKS * CHUNK
REL_MAX = 128
EPS = 1e-6
Q_SCALE = HEAD_DIM ** -0.5

LANES = 128
NEG = -1e30

ROW_TILE = 256
QKV_TILE = 512
SB_TQ = 256
SB_TK = 128
SB_CHAINS = 8
SB_DEAD = -120.0
CB_TQ = 256
CB_TK = CB_TQ + BAND
CB_GROUPS = 8
VMEM_LIMIT = 58 * 1024 * 1024


def _params(*semantics):
    return pltpu.CompilerParams(dimension_semantics=semantics, vmem_limit_bytes=VMEM_LIMIT)


def _resident(shape, layer=None):
    zeros = (0,) * len(shape)
    if layer is None:
        return pl.BlockSpec(shape, lambda *_: zeros, pipeline_mode=pl.Buffered(1))
    return pl.BlockSpec((None,) + tuple(shape), lambda *_: (layer,) + zeros, pipeline_mode=pl.Buffered(1))


def _rms_unit(x):
    return x * lax.rsqrt(jnp.mean(x * x, axis=-1, keepdims=True) + EPS)


def _dot(a, b):
    return jnp.dot(a, b, preferred_element_type=F32)


def _dot_nt(a, b):
    return lax.dot_general(a, b, (((1,), (1,)), ((), ())), preferred_element_type=F32)


def _softplus(z):
    return jnp.maximum(z, 0.0) + jnp.log(1.0 + jnp.exp(-jnp.abs(z)))


def _head_pair_rms(o, g):
    first = lax.broadcasted_iota(jnp.int32, o.shape, 1) < HEAD_DIM
    o2 = o * o
    s0 = jnp.sum(jnp.where(first, o2, 0.0), axis=-1, keepdims=True)
    s1 = jnp.sum(jnp.where(first, 0.0, o2), axis=-1, keepdims=True)
    ms = jnp.where(first, s0, s1) * (1.0 / HEAD_DIM)
    return o * lax.rsqrt(ms + EPS) * g


def _store_qkv(h, wt_ref, outs, per_head_t, prev=()):
    qa_ref, qb_ref = outs[:2]
    w = qa_ref.shape[-1]
    qa_ref[...] = (_dot_nt(h, wt_ref[0:w, :]) * Q_SCALE).astype(BF16)
    qb_ref[...] = (_dot_nt(h, wt_ref[3 * w:4 * w, :]) * Q_SCALE).astype(BF16)
    band_tails = outs[6:]
    for c, ref in enumerate(outs[2:6]):
        first_row = (1 + c + c // 2) * w
        w_t = wt_ref[first_row:first_row + w, :]
        if per_head_t:
            n_prev = ref.shape[0] - 1
            new = _dot_nt(w_t, h).reshape(ref.shape[2:])
            for target in [ref] + ([band_tails[c - 2]] if band_tails and c >= 2 else []):
                if n_prev:
                    target[:n_prev] = prev[c][...]
                target[n_prev, 0] = new
        else:
            ref[...] = _dot_nt(h, w_t)


def _qkv_kernel(x_ref, g_ref, wt_ref, *outs, per_head_t):
    h = (_rms_unit(x_ref[...]) * g_ref[...]).astype(BF16)
    _store_qkv(h, wt_ref, outs, per_head_t)


def _layer_tail(x_ref, a_ref, ob_ref, p_ref, wo_ref, gf_ref, wg_ref, wu_ref, wd_ref, gp_ref, wpg_ref, wpp_ref):
    w = a_ref.shape[-1]
    x = x_ref[...] + _dot(a_ref[...], wo_ref[:w, :]) + _dot(ob_ref[...], wo_ref[w:, :])
    h = (_rms_unit(x) * gf_ref[...]).astype(BF16)
    gate = _dot(h, wg_ref[...])
    up = _dot(h, wu_ref[...])
    act = (gate * jax.nn.sigmoid(gate) * up).astype(BF16)
    x = x + _dot(act, wd_ref[...])
    hp = (_rms_unit(x) * gp_ref[...]).astype(BF16)
    ple_gate = jax.nn.sigmoid(_dot(hp, wpg_ref[...]))
    return x + ple_gate * _dot(p_ref[...].astype(BF16), wpp_ref[...])


N_TAIL = 12


def _mid_layer_kernel(*refs, per_head_t, n_prev):
    tail, (gn_ref, wt_ref) = refs[:N_TAIL], refs[N_TAIL:N_TAIL + 2]
    prev, xo_ref, outs = refs[N_TAIL + 2:N_TAIL + 2 + n_prev], refs[N_TAIL + 2 + n_prev], refs[N_TAIL + 3 + n_prev:]
    x = _layer_tail(*tail)
    xo_ref[...] = x
    _store_qkv((_rms_unit(x) * gn_ref[...]).astype(BF16), wt_ref, outs, per_head_t, prev)


def _last_layer_kernel(*refs):
    tail, (gn_ref, y_ref) = refs[:N_TAIL], refs[N_TAIL:]
    y_ref[...] = _rms_unit(_layer_tail(*tail)) * gn_ref[...]


def _row_spec(tm, width):
    return pl.BlockSpec((tm, width), lambda i: (i, 0))


def _kv_spec(layers, w, tm, seq):
    heads, tiles = w // HEAD_DIM, seq // tm
    return pl.BlockSpec((layers, 1, heads, HEAD_DIM, tm), lambda i: (0, i // tiles, 0, 0, i % tiles))


def _qkv_out(rows, w, tm, seq, layers):
    q_shapes = [jax.ShapeDtypeStruct((rows, w), BF16)] * 2
    q_specs = [_row_spec(tm, w)] * 2
    if seq is None:
        return q_shapes + [jax.ShapeDtypeStruct((rows, w), F32)] * 4, q_specs + [_row_spec(tm, w)] * 4
    kv_shape = jax.ShapeDtypeStruct((layers, rows // seq, w // HEAD_DIM, HEAD_DIM, seq), F32)
    return q_shapes + [kv_shape] * 4, q_specs + [_kv_spec(layers, w, tm, seq)] * 4


def _qkv_call(x, wts, layer, tm, seq):
    rows, d = x.shape
    wt, g = wts["w_in_t"], wts["g_mix"]
    out_shape, out_specs = _qkv_out(rows, wt.shape[1] // 6, tm, seq, layer + 1)
    return pl.pallas_call(
        functools.partial(_qkv_kernel, per_head_t=seq is not None), grid=(rows // tm,),
        in_specs=[_row_spec(tm, d), _resident(g.shape[1:], layer), _resident(wt.shape[1:], layer)],
        out_specs=out_specs, out_shape=out_shape,
        compiler_params=_params("parallel"), name="qkv_proj",
    )(x, g, wt)


TAIL_WEIGHTS = ("w_out", "g_ffn", "w_gate", "w_up", "w_down", "g_ple", "w_ple_gate", "w_ple_proj")


def _layer_call(x, a, ob, p, wts, layer, g_final, tm, seq, kv_prev=(), band_tail=False):
    rows, d = x.shape
    last = layer == wts["w_out"].shape[0] - 1
    tail_in = [x, a, ob, p] + [wts[k] for k in TAIL_WEIGHTS]
    tail_specs = ([_row_spec(tm, d), _row_spec(tm, a.shape[1]), _row_spec(tm, ob.shape[1]),
                   pl.BlockSpec((None, tm, p.shape[2]), lambda i: (layer, i, 0))]
                  + [_resident(wts[k].shape[1:], layer) for k in TAIL_WEIGHTS])
    if last:
        return pl.pallas_call(
            _last_layer_kernel, grid=(rows // tm,), in_specs=tail_specs + [_resident(g_final.shape)],
            out_specs=_row_spec(tm, d), out_shape=jax.ShapeDtypeStruct((rows, d), F32),
            compiler_params=_params("parallel"), name="layer_tail_final",
        )(*tail_in, g_final)
    nxt = [wts["g_mix"], wts["w_in_t"]]
    w = wts["w_in_t"].shape[1] // 6
    out_shape, out_specs = _qkv_out(rows, w, tm, seq, layer + 2)
    if band_tail:
        tiles, first_tail = seq // tm, (seq - BAND) // tm
        out_shape += [jax.ShapeDtypeStruct(out_shape[-1].shape[:-1] + (BAND,), F32)] * 2
        out_specs += [pl.BlockSpec(out_specs[-1].block_shape,
                                   lambda i: (0, i // tiles, 0, 0, jnp.maximum(i % tiles - first_tail, 0)))] * 2
    return pl.pallas_call(
        functools.partial(_mid_layer_kernel, per_head_t=seq is not None, n_prev=len(kv_prev)), grid=(rows // tm,),
        in_specs=(tail_specs + [_resident(t.shape[1:], layer + 1) for t in nxt]
                  + [_kv_spec(layer + 1, w, tm, seq) for _ in kv_prev]),
        out_specs=[_row_spec(tm, d)] + out_specs,
        out_shape=[jax.ShapeDtypeStruct((rows, d), F32)] + out_shape,
        compiler_params=_params("arbitrary" if band_tail else "parallel"), name="layer_tail_qkv",
    )(*tail_in, *nxt, *kv_prev)


def _suffix_matrix(tk):
    j = lax.broadcasted_iota(jnp.int32, (tk, 2 * tk), 0)
    c = lax.broadcasted_iota(jnp.int32, (tk, 2 * tk), 1)
    return jnp.where((c >= tk) | (j > c), -1.0, 0.0).astype(BF16)


def _sb_weights(z, suffix, carry_ref, causal, row0=0):
    tk = suffix.shape[0]
    sp = _softplus(z)
    w_parts = []
    for h in range(z.shape[1] // tk):
        cols = slice(h * tk, (h + 1) * tk)
        z_h, sp_h = z[:, cols], sp[:, cols]
        drop = sp_h if causal is None else jnp.where(causal, sp_h, 0.0)
        sums = _dot(drop.astype(BF16), suffix)
        carry = carry_ref[row0:, cols]
        log_w = (z_h - sp_h) + (sums[:, :tk] + carry)
        w = jnp.exp(log_w)
        if causal is not None:
            w = jnp.where(causal, w, 0.0)
        carry_ref[row0:, cols] = carry + sums[:, tk:]
        w_parts.append(w.astype(BF16))
    return jnp.concatenate(w_parts, axis=1)


def _sb_kernel(q_ref, kt_ref, vt_ref, suffix_ref, g_ref, o_ref, kk_ref, vv_ref, carry_ref, acc_ref, z_ref, w_ref,
               more_ref):
    tq, tk = SB_TQ, SB_TK
    per_q = tq // tk
    i = pl.program_id(2)

    @pl.when(i == 0)
    def _():
        kk_ref[...] = jnp.zeros_like(kk_ref)
        vv_ref[...] = jnp.zeros_like(vv_ref)
        for j in range(kk_ref.shape[0]):
            for h in range(2):
                rows, cols = slice(h * HEAD_DIM, (h + 1) * HEAD_DIM), slice(h * tk, (h + 1) * tk)
                kk_ref[j, rows, cols] = kt_ref[0, h, :, j * tk:(j + 1) * tk].astype(BF16)
                vv_ref[j, rows, cols] = vt_ref[0, h, :, j * tk:(j + 1) * tk].astype(BF16)

    carry_ref[...] = jnp.zeros_like(carry_ref)
    acc_ref[...] = jnp.zeros_like(acc_ref)
    suffix = suffix_ref[...]
    k_off = lax.broadcasted_iota(jnp.int32, (tq, tk), 1)
    row = lax.broadcasted_iota(jnp.int32, (tq, tk), 0)
    last_slot = (per_q - 1) % 2

    def row0(n):
        return (per_q - 1 - n) * tk if isinstance(n, int) and 0 <= n < per_q else 0

    class Chain:
        def __init__(self, c):
            self.c = c
            self.qi = i * SB_CHAINS + c
            self.rows = slice(c * tq, (c + 1) * tq)
            self.q = q_ref[0, self.rows, :]
            self.newest = (self.qi + 1) * per_q - 1
            self.carry, self.acc, self.z, self.w = carry_ref.at[c], acc_ref.at[c], z_ref.at[c], w_ref.at[c]

        def step(self, n, slot, diagonal, with_next=True):
            j = self.newest - n
            r_prev, r_cur, r_next = row0(n - 1) if diagonal else 0, row0(n), row0(n + 1)
            z = self.z[slot, r_cur:, :]
            if with_next:
                self.z[1 - slot, r_next:, :] = _dot(self.q[r_next:], kk_ref[jnp.maximum(j - 1, 0)])
            if not (diagonal and n == 0):
                self.acc[r_prev:, :] += _dot_nt(self.w[1 - slot, r_prev:, :], vv_ref[j + 1])
            causal = ((j * tk + k_off) < self.qi * tq + row)[r_cur:] if diagonal else None
            self.w[slot, r_cur:, :] = _sb_weights(z, suffix, self.carry, causal, r_cur)

        def emit(self, oldest):
            o = self.acc[...] + _dot_nt(self.w[last_slot], vv_ref[oldest])
            o_ref[0, self.rows, :] = _head_pair_rms(o, g_ref[...]).astype(o_ref.dtype)

        def first_steps(self, n_steps):
            self.z[0, row0(0):, :] = _dot(self.q[row0(0):], kk_ref[self.newest])
            yield
            for n in range(n_steps):
                self.step(n, n % 2, n < per_q, with_next=n + 1 < n_steps)
                yield
            self.emit(self.newest - (n_steps - 1))
            more = jnp.logical_and(jnp.max(self.carry[...]) > SB_DEAD, self.newest >= n_steps)
            more_ref[self.c] = more.astype(jnp.int32)

        def older_steps(self):
            @pl.when(more_ref[self.c] != 0)
            def _():
                self.z[0] = _dot(self.q, kk_ref[self.newest - 2 * per_q])

                def more(state):
                    t, live = state
                    return jnp.logical_and(t < self.qi, live)

                def trip(state):
                    t, _ = state
                    for d in range(per_q):
                        self.step(per_q * (t + 1) + d, d % 2, False)
                    return t + 1, jnp.max(self.carry[...]) > SB_DEAD
                trips, _ = lax.while_loop(more, trip, (jnp.int32(1), self.qi > 0))
                self.emit(per_q * (self.qi - trips))

    chains = [Chain(c) for c in range(SB_CHAINS)]

    def interleave(walks):
        while walks:
            walks = [w for w in walks if next(w, "done") != "done"]

    @pl.when(i == 0)
    def _():
        interleave([chains[0].first_steps(per_q)] + [ch.first_steps(2 * per_q) for ch in chains[1:]])

    @pl.when(i > 0)
    def _():
        interleave([ch.first_steps(2 * per_q) for ch in chains])

    for ch in chains:
        ch.older_steps()


def _pair_kv_spec(s, layer):
    return pl.BlockSpec((None, 1, 2, HEAD_DIM, s), lambda bi, pi, i: (layer, bi, pi, 0, 0))


def _sb_call(q, kt, vt, suffix, g, layer):
    b, s, w = q.shape
    nkb = s // SB_TK
    step_rows = SB_CHAINS * SB_TQ
    assert (SB_TQ // SB_TK) % 2 == 0 and SB_CHAINS >= 2
    return pl.pallas_call(
        _sb_kernel, grid=(b, w // LANES, s // step_rows),
        in_specs=[pl.BlockSpec((1, step_rows, LANES), lambda bi, pi, i: (bi, i, pi)),
                  _pair_kv_spec(s, layer), _pair_kv_spec(s, layer),
                  _resident(suffix.shape), pl.BlockSpec((None, 1, LANES), lambda bi, pi, i: (layer, 0, pi))],
        out_specs=pl.BlockSpec((1, step_rows, LANES), lambda bi, pi, i: (bi, i, pi)),
        out_shape=jax.ShapeDtypeStruct((b, s, w), BF16),
        scratch_shapes=[pltpu.VMEM((nkb, LANES, 2 * SB_TK), BF16), pltpu.VMEM((nkb, LANES, 2 * SB_TK), BF16),
                        pltpu.VMEM((SB_CHAINS, SB_TQ, 2 * SB_TK), F32), pltpu.VMEM((SB_CHAINS, SB_TQ, LANES), F32),
                        pltpu.VMEM((SB_CHAINS, 2, SB_TQ, 2 * SB_TK), F32),
                        pltpu.VMEM((SB_CHAINS, 2, SB_TQ, 2 * SB_TK), BF16), pltpu.SMEM((SB_CHAINS,), jnp.int32)],
        compiler_params=_params("parallel", "parallel", "arbitrary"), name="sb_attention",
    )(q, kt, vt, suffix, g)


def _softmax_unnormalised(s):
    e = jnp.exp(s - jnp.max(s, axis=-1, keepdims=True))
    return e, jnp.sum(e, axis=-1, keepdims=True)


def _cb_kernel(q_ref, qn_ref, kt_ref, vt_ref, bias_ref, g_ref, o_ref, kk_ref, vv_ref, s_ref, *, steps):
    i = pl.program_id(2)
    lead = BAND // CB_TQ
    n_window = CB_TK // CB_TQ
    last_block = kk_ref.shape[1] - n_window

    def ones_row(h):
        return (1 - h) * HEAD_DIM

    def window(ref, h, block):
        return jnp.concatenate([ref[h, block + c] for c in range(n_window)], axis=1)

    def put_scores(slot, q, block):
        for h in range(2):
            s_ref[slot, h] = _dot(q, window(kk_ref, h, block))

    @pl.when(i == 0)
    def _():
        kk_ref[...] = jnp.zeros_like(kk_ref)
        row = lax.broadcasted_iota(jnp.int32, vv_ref.shape[1:], 1)
        for h in range(2):
            vv_ref[h] = jnp.where(row == ones_row(h), 1.0, 0.0).astype(BF16)
            rows = slice(h * HEAD_DIM, (h + 1) * HEAD_DIM)
            for c in range(kk_ref.shape[1] - lead):
                kk_ref[h, lead + c, rows, :] = kt_ref[0, h, :, c * CB_TQ:(c + 1) * CB_TQ].astype(BF16)
                vv_ref[h, lead + c, rows, :] = vt_ref[0, h, :, c * CB_TQ:(c + 1) * CB_TQ].astype(BF16)
        put_scores(0, q_ref[0, :CB_TQ, :], 0)

    key = lax.broadcasted_iota(jnp.int32, (CB_TQ, CB_TK), 1)
    first = lax.broadcasted_iota(jnp.int32, (CB_TQ, LANES), 1) < HEAD_DIM
    def numerators(g):
        block, es = i * CB_GROUPS + g, []
        for h in range(2):
            s = s_ref[g % 2, h] + bias_ref[h]
            if g < lead:
                s = jnp.where(key >= BAND - block * CB_TQ, s, NEG)
            es.append(jnp.exp(s - jnp.max(s, axis=-1, keepdims=True)).astype(BF16))
        return es

    def finish(g, es):
        block, o = i * CB_GROUPS + g, None
        for h in range(2):
            o_h = _dot_nt(es[h], window(vv_ref, h, block))
            o_h = o_h / o_h[:, ones_row(h):ones_row(h) + 1]
            o = o_h if h == 0 else jnp.where(first, o, o_h)
        o_ref[0, g * CB_TQ:(g + 1) * CB_TQ, :] = _head_pair_rms(o, g_ref[...]).astype(o_ref.dtype)

    pending = None
    for g in range(CB_GROUPS):
        block = i * CB_GROUPS + g
        es = numerators(g)
        if g + 1 < CB_GROUPS:
            put_scores((g + 1) % 2, q_ref[0, (g + 1) * CB_TQ:(g + 2) * CB_TQ, :], block + 1)
        elif steps > 1:
            put_scores((g + 1) % 2, qn_ref[0], jnp.minimum(block + 1, last_block))
        if pending is not None:
            finish(*pending)
        pending = (g, es)
    finish(*pending)


def _cb_call(q, kt, vt, bias, g, layer):
    b, s, w = q.shape
    blocks = (BAND + s) // CB_TQ
    step_rows = CB_GROUPS * CB_TQ
    n_q = s // CB_TQ
    return pl.pallas_call(
        functools.partial(_cb_kernel, steps=s // step_rows), grid=(b, w // LANES, s // step_rows),
        in_specs=[pl.BlockSpec((1, step_rows, LANES), lambda bi, pi, i: (bi, i, pi)),
                  pl.BlockSpec((1, CB_TQ, LANES), lambda bi, pi, i: (bi, jnp.minimum((i + 1) * CB_GROUPS, n_q - 1), pi)),
                  _pair_kv_spec(s, layer), _pair_kv_spec(s, layer),
                  pl.BlockSpec((2, CB_TQ, CB_TK), lambda bi, pi, i: (pi, 0, 0)),
                  pl.BlockSpec((None, 1, LANES), lambda bi, pi, i: (layer, 0, pi))],
        out_specs=pl.BlockSpec((1, step_rows, LANES), lambda bi, pi, i: (bi, i, pi)),
        out_shape=jax.ShapeDtypeStruct((b, s, w), BF16),
        scratch_shapes=[pltpu.VMEM((2, blocks, LANES, CB_TQ), BF16), pltpu.VMEM((2, blocks, LANES, CB_TQ), BF16),
                        pltpu.VMEM((2, 2, CB_TQ, CB_TK), F32)],
        compiler_params=_params("parallel", "parallel", "arbitrary"), name="cb_attention",
    )(q, q, kt, vt, bias, g)


def _bias_tiles_kernel(table_ref, o_ref, *, heads):
    layer, head = pl.program_id(0), pl.program_id(1)
    n_rel = 2 * REL_MAX + 1
    shape = o_ref.shape[2:]
    dist = (2 * CHUNK + lax.broadcasted_iota(jnp.int32, shape, 0) - lax.broadcasted_iota(jnp.int32, shape, 1))
    idx = jnp.clip(dist, -REL_MAX, REL_MAX) + REL_MAX

    def pick(r, acc):
        return jnp.where(idx == r, table_ref[(layer * n_rel + r) * heads + head], acc)
    lowest = REL_MAX + 2 * CHUNK - (shape[1] - 1)
    o_ref[0, 0] = lax.fori_loop(max(lowest, 0), n_rel, pick, jnp.zeros(shape, F32), unroll=8)


def _bias_tiles(rel_table):
    n_layers, _, heads = rel_table.shape
    return pl.pallas_call(
        functools.partial(_bias_tiles_kernel, heads=heads), grid=(n_layers, heads),
        in_specs=[pl.BlockSpec(memory_space=pltpu.SMEM)],
        out_specs=pl.BlockSpec((1, 1, CHUNK, 3 * CHUNK), lambda l, h: (l, h, 0, 0)),
        out_shape=jax.ShapeDtypeStruct((n_layers, heads, CHUNK, 3 * CHUNK), F32),
        compiler_params=_params("parallel", "parallel"), name="rel_bias_tiles",
    )(rel_table.reshape(-1))


def _prompt_bias(tiles, far):
    heads = tiles.shape[0]
    qc, kc = CB_TQ // CHUNK, CB_TK // CHUNK
    far_tile = jnp.broadcast_to(far[:, None, None], (heads, CHUNK, CHUNK))
    hidden = jnp.full((heads, CHUNK, CHUNK), NEG, F32)
    rows = []
    for c in range(qc):
        row = []
        for k in range(kc):
            m = k - c
            if m < 0 or m > BAND_CHUNKS:
                row.append(hidden)
            elif m < BAND_CHUNKS - 2:
                row.append(far_tile)
            else:
                t = m - (BAND_CHUNKS - 2)
                row.append(tiles[:, :, t * CHUNK:(t + 1) * CHUNK])
        rows.append(jnp.concatenate(row, axis=2))
    return jnp.concatenate(rows, axis=1)


def _head_rows(x, heads):
    t, w = x.shape
    tiled = jnp.concatenate([x] * heads, axis=0)
    row_head = lax.div(lax.broadcasted_iota(jnp.int32, tiled.shape, 0), t)
    col_head = lax.div(lax.broadcasted_iota(jnp.int32, tiled.shape, 1), HEAD_DIM)
    return jnp.where(row_head == col_head, tiled, jnp.zeros_like(tiled))


def _fold_head_rows(o, heads):
    t = o.shape[0] // heads
    col_head = lax.div(lax.broadcasted_iota(jnp.int32, (t, o.shape[1]), 1), HEAD_DIM)
    out = jnp.zeros((t, o.shape[1]), F32)
    for h in range(heads):
        out = jnp.where(col_head == h, o[h * t:(h + 1) * t], out)
    return out


def _all_heads_rms(o, g):
    parts = [_head_pair_rms(o[:, c:c + LANES], g[:, c:c + LANES]) for c in range(0, o.shape[1], LANES)]
    return jnp.concatenate(parts, axis=1)


def _pad_rows(x, rows):
    return jnp.concatenate([x, jnp.zeros((rows - x.shape[0], x.shape[1]), x.dtype)], axis=0)


def _sample_kernel(qa_ref, ka_ref, va_ref, ckt_ref, cvt_ref, qb_ref, kb_ref, vb_ref, ckbt_ref, cvbt_ref,
                   suffix_ref, bias_ref, ga_ref, gb_ref, oa_ref, ob_ref, carry_ref, acc_ref, live_ref):
    t, w = qa_ref.shape[1:]
    heads = w // HEAD_DIM
    tk = SB_TK
    past = ckt_ref.shape[2]

    carry_ref[...] = jnp.zeros_like(carry_ref)
    q = _head_rows(qa_ref[0], heads)
    suffix = suffix_ref[...]
    frame = lax.rem(lax.broadcasted_iota(jnp.int32, (heads * t, tk), 0), t)
    key = lax.broadcasted_iota(jnp.int32, (heads * t, tk), 1)
    wts = _sb_weights(_dot_nt(q, _pad_rows(ka_ref[0], tk).astype(BF16)), suffix, carry_ref, key < frame)
    acc_ref[...] = _dot(wts, _pad_rows(va_ref[0], tk).astype(BF16))
    def cached(blk):
        cols = slice(blk * tk, (blk + 1) * tk)
        wts = _sb_weights(_dot(q, ckt_ref[0, :, cols].astype(BF16)), suffix, carry_ref, None)
        acc_ref[...] += _dot_nt(wts, cvt_ref[0, :, cols].astype(BF16))

    live_ref[0] = jnp.int32(1)
    newest_first = list(reversed(range(past // tk)))
    for first in range(0, len(newest_first), 2):
        @pl.when(live_ref[0] != 0)
        def _(first=first):
            for blk in newest_first[first:first + 2]:
                cached(blk)
            live_ref[0] = (jnp.max(carry_ref[...]) > SB_DEAD).astype(jnp.int32)
    oa_ref[0] = _all_heads_rms(_fold_head_rows(acc_ref[...], heads), ga_ref[...]).astype(oa_ref.dtype)

    qb = _head_rows(qb_ref[0], heads)
    s = jnp.concatenate([_dot(qb, ckbt_ref[0].astype(BF16)),
                         _dot_nt(qb, _pad_rows(kb_ref[0], LANES).astype(BF16))], axis=1)
    e, l = _softmax_unnormalised(s + bias_ref[...])
    e = e.astype(BF16)
    band = ckbt_ref.shape[2]
    o = _dot_nt(e[:, :band], cvbt_ref[0].astype(BF16)) + _dot(e[:, band:], _pad_rows(vb_ref[0], LANES).astype(BF16))
    ob_ref[0] = _all_heads_rms(_fold_head_rows(o / l, heads), gb_ref[...]).astype(ob_ref.dtype)


def _sample_call(qa, ka, va, ckt, cvt, qb, kb, vb, ckbt, cvbt, suffix, bias, ga, gb, layer):
    b, t, w = qa.shape
    heads = w // HEAD_DIM
    new = pl.BlockSpec((1, t, w), lambda bi: (bi, 0, 0))

    def cache(c):
        return pl.BlockSpec((None, 1) + c.shape[2:], lambda bi: (layer, bi, 0, 0))
    return pl.pallas_call(
        _sample_kernel, grid=(b,),
        in_specs=[new, new, new, cache(ckt), cache(cvt), new, new, new, cache(ckbt), cache(cvbt),
                  _resident(suffix.shape), _resident(bias.shape), _resident(ga.shape[1:], layer),
                  _resident(gb.shape[1:], layer)],
        out_specs=[new, new], out_shape=[jax.ShapeDtypeStruct((b, t, w), BF16)] * 2,
        scratch_shapes=[pltpu.VMEM((heads * t, SB_TK), F32), pltpu.VMEM((heads * t, w), F32),
                        pltpu.SMEM((1,), jnp.int32)],
        compiler_params=_params("parallel"), name="sample_attention",
    )(qa, ka, va, ckt, cvt, qb, kb, vb, ckbt, cvbt, suffix, bias, ga, gb)


def _sample_bias(tiles, far, t):
    heads = tiles.shape[0]
    near = BAND - 2 * CHUNK
    parts = [jnp.broadcast_to(far[:, None, None], (heads, t, near)), tiles[:, :t, :2 * CHUNK + t],
             jnp.full((heads, t, LANES - t), NEG, F32)]
    return jnp.concatenate(parts, axis=2).reshape(heads * t, BAND + LANES)


def _per_head_t(cache):
    n_l, n_b, n_p, n_h, n_d = cache.shape
    return jnp.transpose(cache, (0, 1, 3, 4, 2)).reshape(n_l, n_b, n_h * n_d, n_p)


def kernel(x_prompt, x_sample, p_prompt, p_sample, cache_sb_k, cache_sb_v, cache_cb_k, cache_cb_v, g_mix, w_in,
           rel_table, g_out_sb, g_out_cb, w_out, g_ffn, w_gate, w_up, w_down, g_ple, w_ple_gate, w_ple_proj,
           g_final):
    depth = w_in.shape[0]
    b, s, d = x_prompt.shape
    bs, t, _ = x_sample.shape
    w = g_out_sb.shape[1]
    heads = w // HEAD_DIM
    past, band = cache_sb_k.shape[2], cache_cb_k.shape[2]
    assert g_out_cb.shape[1] == w and w_in.shape[2] == 6 * w and w % LANES == 0
    assert s % (SB_CHAINS * SB_TQ) == 0 and s % (CB_GROUPS * CB_TQ) == 0 and s >= BAND and s % ROW_TILE == 0 and s % QKV_TILE == 0
    assert band == BAND and past % SB_TK == 0 and t <= CHUNK and t % 8 == 0

    gain = lambda g: g[:, None, :]
    wts = dict(
        g_mix=gain(g_mix), g_ffn=gain(g_ffn), g_ple=gain(g_ple),
        w_in_t=jnp.swapaxes(w_in, 1, 2).astype(BF16),
        w_out=w_out.astype(BF16), w_gate=w_gate.astype(BF16), w_up=w_up.astype(BF16), w_down=w_down.astype(BF16),
        w_ple_gate=w_ple_gate.astype(BF16), w_ple_proj=w_ple_proj.astype(BF16))
    g_sb, g_cb = gain(g_out_sb), gain(g_out_cb)
    g_fin = g_final.reshape(1, d)
    suffix = _suffix_matrix(SB_TK)
    tiles = _bias_tiles(rel_table)
    far = rel_table[:, 2 * REL_MAX, :]
    caches = [_per_head_t(c) for c in (cache_sb_k, cache_sb_v, cache_cb_k, cache_cb_v)]
    pp = p_prompt.reshape(depth, b * s, -1)
    ps = p_sample.reshape(depth, bs * t, -1)

    xp = x_prompt.reshape(b * s, d)
    xs = x_sample.reshape(bs * t, d)
    qkv_p = _qkv_call(xp, wts, 0, QKV_TILE, s)
    qkv_s = _qkv_call(xs, wts, 0, bs * t, None)
    new_s = []
    for i in range(depth):
        qa, qb = [u.reshape(b, s, w) for u in qkv_p[:2]]
        kat, vat, kbt, vbt = kv_all = qkv_p[2:]
        a = _sb_call(qa, kat, vat, suffix, g_sb, i)
        ob = _cb_call(qb, kbt, vbt, _prompt_bias(tiles[i], far[i]), g_cb, i)

        sqa, sqb, ska, sva, skb, svb = [u.reshape(bs, t, w) for u in qkv_s]
        new_s.append((ska, sva, skb, svb))
        sa, sob = _sample_call(sqa, ska, sva, caches[0], caches[1], sqb, skb, svb, caches[2], caches[3],
                               suffix, _sample_bias(tiles[i], far[i], t), g_sb, g_cb, i)

        out_p = _layer_call(xp, a.reshape(b * s, w), ob.reshape(b * s, w), pp, wts, i, g_fin, ROW_TILE, s, kv_all,
                            band_tail=i + 2 == depth)
        out_s = _layer_call(xs, sa.reshape(bs * t, w), sob.reshape(bs * t, w), ps, wts, i, g_fin, bs * t, None)
        if i == depth - 1:
            xp, xs = out_p, out_s
        else:
            xp, qkv_p, band_tails = out_p[0], out_p[1:7], out_p[7:]
            xs, qkv_s = out_s[0], out_s[1:]
    if depth == 1:
        band_tails = [kbt[..., s - BAND:], vbt[..., s - BAND:]]

    def positions_major(kv_t):
        return jnp.transpose(kv_t, (0, 1, 4, 2, 3))

    def stack(items):
        return jnp.stack(items).reshape(depth, bs, t, heads, HEAD_DIM)

    return (xp.reshape(b, s, d), xs.reshape(bs, t, d),
            positions_major(kat), positions_major(vat),
            positions_major(band_tails[0]), positions_major(band_tails[1]),
            stack([e[0] for e in new_s]), stack([e[1] for e in new_s]),
            stack([e[2] for e in new_s]), stack([e[3] for e in new_s]))
```

```python
import functools

import jax
import jax.numpy as jnp
from jax import lax
from jax.experimental import pallas as pl
from jax.experimental.pallas import tpu as pltpu

F32 = jnp.float32
BF16 = jnp.bfloat16

HEAD_DIM = 64
CHUNK = 64
BAND_CHUNKS = 8
BAND = BAND_CHUNKS * CHUNK
REL_MAX = 128
EPS = 1e-6
Q_SCALE = HEAD_DIM ** -0.5

LANES = 128
NEG = -1e30

ROW_TILE = 256
TAIL_CHAINS = 2
MIN_CHAIN_ROWS = 128
QKV_TILE = 512
SB_TQ = 256
SB_TK = 128
SB_CHAINS = 8
SB_DEAD = -120.0
CB_TQ = 256
CB_TK = CB_TQ + BAND
CB_GROUPS = 8
VMEM_LIMIT = 58 * 1024 * 1024


def _params(*semantics):
    return pltpu.CompilerParams(dimension_semantics=semantics, vmem_limit_bytes=VMEM_LIMIT)


def _resident(shape, layer=None):
    zeros = (0,) * len(shape)
    if layer is None:
        return pl.BlockSpec(shape, lambda *_: zeros, pipeline_mode=pl.Buffered(1))
    return pl.BlockSpec((None,) + tuple(shape), lambda *_: (layer,) + zeros, pipeline_mode=pl.Buffered(1))


def _rms_unit(x):
    return x * lax.rsqrt(jnp.mean(x * x, axis=-1, keepdims=True) + EPS)


def _dot(a, b):
    return jnp.dot(a, b, preferred_element_type=F32)


def _dot_nt(a, b):
    return lax.dot_general(a, b, (((1,), (1,)), ((), ())), preferred_element_type=F32)


def _softplus(z):
    return jnp.maximum(z, 0.0) + jnp.log(1.0 + jnp.exp(-jnp.abs(z)))


def _head_pair_rms(o, g):
    first = lax.broadcasted_iota(jnp.int32, o.shape, 1) < HEAD_DIM
    o2 = o * o
    s0 = jnp.sum(jnp.where(first, o2, 0.0), axis=-1, keepdims=True)
    s1 = jnp.sum(jnp.where(first, 0.0, o2), axis=-1, keepdims=True)
    ms = jnp.where(first, s0, s1) * (1.0 / HEAD_DIM)
    return o * lax.rsqrt(ms + EPS) * g


def _store_qkv(h, wt_ref, outs, per_head_t, prev=()):
    qa_ref, qb_ref = outs[:2]
    w = qa_ref.shape[-1]
    qa_ref[...] = (_dot_nt(h, wt_ref[0:w, :]) * Q_SCALE).astype(BF16)
    qb_ref[...] = (_dot_nt(h, wt_ref[3 * w:4 * w, :]) * Q_SCALE).astype(BF16)
    band_tails = outs[6:]
    for c, ref in enumerate(outs[2:6]):
        first_row = (1 + c + c // 2) * w
        w_t = wt_ref[first_row:first_row + w, :]
        if per_head_t:
            n_prev = ref.shape[0] - 1
            new = _dot_nt(w_t, h).reshape(ref.shape[2:])
            for target in [ref] + ([band_tails[c - 2]] if band_tails and c >= 2 else []):
                if n_prev:
                    target[:n_prev] = prev[c][...]
                target[n_prev, 0] = new
        else:
            ref[...] = _dot_nt(h, w_t)


def _qkv_kernel(x_ref, g_ref, wt_ref, *outs, per_head_t):
    h = (_rms_unit(x_ref[...]) * g_ref[...]).astype(BF16)
    _store_qkv(h, wt_ref, outs, per_head_t)


def _layer_tail(x_ref, a_ref, ob_ref, p_ref, wo_ref, gf_ref, wg_ref, wu_ref, wd_ref, gp_ref, wpg_ref, wpp_ref):
    w = a_ref.shape[-1]
    out = {}

    def rows_chain(c, rows):
        x = x_ref[rows, :] + _dot(a_ref[rows, :], wo_ref[:w, :]) + _dot(ob_ref[rows, :], wo_ref[w:, :])
        yield
        h = (_rms_unit(x) * gf_ref[...]).astype(BF16)
        yield
        gate = _dot(h, wg_ref[...])
        up = _dot(h, wu_ref[...])
        yield
        act = (gate * jax.nn.sigmoid(gate) * up).astype(BF16)
        x = x + _dot(act, wd_ref[...])
        yield
        hp = (_rms_unit(x) * gp_ref[...]).astype(BF16)
        yield
        ple_gate = jax.nn.sigmoid(_dot(hp, wpg_ref[...]))
        out[c] = x + ple_gate * _dot(p_ref[rows, :].astype(BF16), wpp_ref[...])

    n = x_ref.shape[0]
    chains = TAIL_CHAINS if n % (TAIL_CHAINS * MIN_CHAIN_ROWS) == 0 else 1
    step = n // chains
    walks = [rows_chain(c, slice(c * step, (c + 1) * step)) for c in range(chains)]
    while walks:
        walks = [wk for wk in walks if next(wk, "done") != "done"]
    return jnp.concatenate([out[c] for c in range(chains)], axis=0)


N_TAIL = 12


def _mid_layer_kernel(*refs, per_head_t, n_prev):
    tail, (gn_ref, wt_ref) = refs[:N_TAIL], refs[N_TAIL:N_TAIL + 2]
    prev, xo_ref, outs = refs[N_TAIL + 2:N_TAIL + 2 + n_prev], refs[N_TAIL + 2 + n_prev], refs[N_TAIL + 3 + n_prev:]
    x = _layer_tail(*tail)
    xo_ref[...] = x
    _store_qkv((_rms_unit(x) * gn_ref[...]).astype(BF16), wt_ref, outs, per_head_t, prev)


def _last_layer_kernel(*refs):
    tail, (gn_ref, y_ref) = refs[:N_TAIL], refs[N_TAIL:]
    y_ref[...] = _rms_unit(_layer_tail(*tail)) * gn_ref[...]


def _row_spec(tm, width):
    return pl.BlockSpec((tm, width), lambda i: (i, 0))


def _kv_spec(layers, w, tm, seq):
    heads, tiles = w // HEAD_DIM, seq // tm
    return pl.BlockSpec((layers, 1, heads, HEAD_DIM, tm), lambda i: (0, i // tiles, 0, 0, i % tiles))


def _qkv_out(rows, w, tm, seq, layers):
    q_shapes = [jax.ShapeDtypeStruct((rows, w), BF16)] * 2
    q_specs = [_row_spec(tm, w)] * 2
    if seq is None:
        return q_shapes + [jax.ShapeDtypeStruct((rows, w), F32)] * 4, q_specs + [_row_spec(tm, w)] * 4
    kv_shape = jax.ShapeDtypeStruct((layers, rows // seq, w // HEAD_DIM, HEAD_DIM, seq), F32)
    return q_shapes + [kv_shape] * 4, q_specs + [_kv_spec(layers, w, tm, seq)] * 4


def _qkv_call(x, wts, layer, tm, seq):
    rows, d = x.shape
    wt, g = wts["w_in_t"], wts["g_mix"]
    out_shape, out_specs = _qkv_out(rows, wt.shape[1] // 6, tm, seq, layer + 1)
    return pl.pallas_call(
        functools.partial(_qkv_kernel, per_head_t=seq is not None), grid=(rows // tm,),
        in_specs=[_row_spec(tm, d), _resident(g.shape[1:], layer), _resident(wt.shape[1:], layer)],
        out_specs=out_specs, out_shape=out_shape,
        compiler_params=_params("parallel"), name="qkv_proj",
    )(x, g, wt)


TAIL_WEIGHTS = ("w_out", "g_ffn", "w_gate", "w_up", "w_down", "g_ple", "w_ple_gate", "w_ple_proj")


def _layer_call(x, a, ob, p, wts, layer, g_final, tm, seq, kv_prev=(), band_tail=False):
    rows, d = x.shape
    last = layer == wts["w_out"].shape[0] - 1
    tail_in = [x, a, ob, p] + [wts[k] for k in TAIL_WEIGHTS]
    tail_specs = ([_row_spec(tm, d), _row_spec(tm, a.shape[1]), _row_spec(tm, ob.shape[1]),
                   pl.BlockSpec((None, tm, p.shape[2]), lambda i: (layer, i, 0))]
                  + [_resident(wts[k].shape[1:], layer) for k in TAIL_WEIGHTS])
    if last:
        return pl.pallas_call(
            _last_layer_kernel, grid=(rows // tm,), in_specs=tail_specs + [_resident(g_final.shape)],
            out_specs=_row_spec(tm, d), out_shape=jax.ShapeDtypeStruct((rows, d), F32),
            compiler_params=_params("parallel"), name="layer_tail_final",
        )(*tail_in, g_final)
    nxt = [wts["g_mix"], wts["w_in_t"]]
    w = wts["w_in_t"].shape[1] // 6
    out_shape, out_specs = _qkv_out(rows, w, tm, seq, layer + 2)
    if band_tail:
        tiles, first_tail = seq // tm, (seq - BAND) // tm
        out_shape += [jax.ShapeDtypeStruct(out_shape[-1].shape[:-1] + (BAND,), F32)] * 2
        out_specs += [pl.BlockSpec(out_specs[-1].block_shape,
                                   lambda i: (0, i // tiles, 0, 0, jnp.maximum(i % tiles - first_tail, 0)))] * 2
    return pl.pallas_call(
        functools.partial(_mid_layer_kernel, per_head_t=seq is not None, n_prev=len(kv_prev)), grid=(rows // tm,),
        in_specs=(tail_specs + [_resident(t.shape[1:], layer + 1) for t in nxt]
                  + [_kv_spec(layer + 1, w, tm, seq) for _ in kv_prev]),
        out_specs=[_row_spec(tm, d)] + out_specs,
        out_shape=[jax.ShapeDtypeStruct((rows, d), F32)] + out_shape,
        compiler_params=_params("arbitrary" if band_tail else "parallel"), name="layer_tail_qkv",
    )(*tail_in, *nxt, *kv_prev)


def _suffix_matrix(tk):
    j = lax.broadcasted_iota(jnp.int32, (tk, 2 * tk), 0)
    c = lax.broadcasted_iota(jnp.int32, (tk, 2 * tk), 1)
    return jnp.where((c >= tk) | (j > c), -1.0, 0.0).astype(BF16)


def _sb_weights(z, suffix, carry_ref, causal, row0=0):
    tk = suffix.shape[0]
    sp = _softplus(z)
    w_parts = []
    for h in range(z.shape[1] // tk):
        cols = slice(h * tk, (h + 1) * tk)
        z_h, sp_h = z[:, cols], sp[:, cols]
        drop = sp_h if causal is None else jnp.where(causal, sp_h, 0.0)
        sums = _dot(drop.astype(BF16), suffix)
        carry = carry_ref[row0:, cols]
        log_w = (z_h - sp_h) + (sums[:, :tk] + carry)
        w = jnp.exp(log_w)
        if causal is not None:
            w = jnp.where(causal, w, 0.0)
        carry_ref[row0:, cols] = carry + sums[:, tk:]
        w_parts.append(w.astype(BF16))
    return jnp.concatenate(w_parts, axis=1)


def _sb_kernel(q_ref, kt_ref, vt_ref, suffix_ref, g_ref, o_ref, kk_ref, vv_ref, carry_ref, acc_ref, z_ref, w_ref,
               more_ref):
    tq, tk = SB_TQ, SB_TK
    per_q = tq // tk
    i = pl.program_id(2)

    @pl.when(i == 0)
    def _():
        kk_ref[...] = jnp.zeros_like(kk_ref)
        vv_ref[...] = jnp.zeros_like(vv_ref)
        for j in range(kk_ref.shape[0]):
            for h in range(2):
                rows, cols = slice(h * HEAD_DIM, (h + 1) * HEAD_DIM), slice(h * tk, (h + 1) * tk)
                kk_ref[j, rows, cols] = kt_ref[0, h, :, j * tk:(j + 1) * tk].astype(BF16)
                vv_ref[j, rows, cols] = vt_ref[0, h, :, j * tk:(j + 1) * tk].astype(BF16)

    carry_ref[...] = jnp.zeros_like(carry_ref)
    acc_ref[...] = jnp.zeros_like(acc_ref)
    suffix = suffix_ref[...]
    k_off = lax.broadcasted_iota(jnp.int32, (tq, tk), 1)
    row = lax.broadcasted_iota(jnp.int32, (tq, tk), 0)
    last_slot = (per_q - 1) % 2

    def row0(n):
        return (per_q - 1 - n) * tk if isinstance(n, int) and 0 <= n < per_q else 0

    class Chain:
        def __init__(self, c):
            self.c = c
            self.qi = i * SB_CHAINS + c
            self.rows = slice(c * tq, (c + 1) * tq)
            self.q = q_ref[0, self.rows, :]
            self.newest = (self.qi + 1) * per_q - 1
            self.carry, self.acc, self.z, self.w = carry_ref.at[c], acc_ref.at[c], z_ref.at[c], w_ref.at[c]

        def step(self, n, slot, diagonal, with_next=True):
            j = self.newest - n
            r_prev, r_cur, r_next = row0(n - 1) if diagonal else 0, row0(n), row0(n + 1)
            z = self.z[slot, r_cur:, :]
            if with_next:
                self.z[1 - slot, r_next:, :] = _dot(self.q[r_next:], kk_ref[jnp.maximum(j - 1, 0)])
            if not (diagonal and n == 0):
                self.acc[r_prev:, :] += _dot_nt(self.w[1 - slot, r_prev:, :], vv_ref[j + 1])
            causal = ((j * tk + k_off) < self.qi * tq + row)[r_cur:] if diagonal else None
            self.w[slot, r_cur:, :] = _sb_weights(z, suffix, self.carry, causal, r_cur)

        def emit(self, oldest):
            o = self.acc[...] + _dot_nt(self.w[last_slot], vv_ref[oldest])
            o_ref[0, self.rows, :] = _head_pair_rms(o, g_ref[...]).astype(o_ref.dtype)

        def first_steps(self, n_steps):
            self.z[0, row0(0):, :] = _dot(self.q[row0(0):], kk_ref[self.newest])
            yield
            for n in range(n_steps):
                self.step(n, n % 2, n < per_q, with_next=n + 1 < n_steps)
                yield
            self.emit(self.newest - (n_steps - 1))
            more = jnp.logical_and(jnp.max(self.carry[...]) > SB_DEAD, self.newest >= n_steps)
            more_ref[self.c] = more.astype(jnp.int32)

        def older_steps(self):
            @pl.when(more_ref[self.c] != 0)
            def _():
                self.z[0] = _dot(self.q, kk_ref[self.newest - 2 * per_q])

                def more(state):
                    t, live = state
                    return jnp.logical_and(t < self.qi, live)

                def trip(state):
                    t, _ = state
                    for d in range(per_q):
                        self.step(per_q * (t + 1) + d, d % 2, False)
                    return t + 1, jnp.max(self.carry[...]) > SB_DEAD
                trips, _ = lax.while_loop(more, trip, (jnp.int32(1), self.qi > 0))
                self.emit(per_q * (self.qi - trips))

    chains = [Chain(c) for c in range(SB_CHAINS)]

    def interleave(walks):
        while walks:
            walks = [w for w in walks if next(w, "done") != "done"]

    @pl.when(i == 0)
    def _():
        interleave([chains[0].first_steps(per_q)] + [ch.first_steps(2 * per_q) for ch in chains[1:]])

    @pl.when(i > 0)
    def _():
        interleave([ch.first_steps(2 * per_q) for ch in chains])

    for ch in chains:
        ch.older_steps()


def _pair_kv_spec(s, layer):
    return pl.BlockSpec((None, 1, 2, HEAD_DIM, s), lambda bi, pi, i: (layer, bi, pi, 0, 0))


def _sb_call(q, kt, vt, suffix, g, layer):
    b, s, w = q.shape
    nkb = s // SB_TK
    step_rows = SB_CHAINS * SB_TQ
    assert (SB_TQ // SB_TK) % 2 == 0 and SB_CHAINS >= 2
    return pl.pallas_call(
        _sb_kernel, grid=(b, w // LANES, s // step_rows),
        in_specs=[pl.BlockSpec((1, step_rows, LANES), lambda bi, pi, i: (bi, i, pi)),
                  _pair_kv_spec(s, layer), _pair_kv_spec(s, layer),
                  _resident(suffix.shape), pl.BlockSpec((None, 1, LANES), lambda bi, pi, i: (layer, 0, pi))],
        out_specs=pl.BlockSpec((1, step_rows, LANES), lambda bi, pi, i: (bi, i, pi)),
        out_shape=jax.ShapeDtypeStruct((b, s, w), BF16),
        scratch_shapes=[pltpu.VMEM((nkb, LANES, 2 * SB_TK), BF16), pltpu.VMEM((nkb, LANES, 2 * SB_TK), BF16),
                        pltpu.VMEM((SB_CHAINS, SB_TQ, 2 * SB_TK), F32), pltpu.VMEM((SB_CHAINS, SB_TQ, LANES), F32),
                        pltpu.VMEM((SB_CHAINS, 2, SB_TQ, 2 * SB_TK), F32),
                        pltpu.VMEM((SB_CHAINS, 2, SB_TQ, 2 * SB_TK), BF16), pltpu.SMEM((SB_CHAINS,), jnp.int32)],
        compiler_params=_params("parallel", "parallel", "arbitrary"), name="sb_attention",
    )(q, kt, vt, suffix, g)


def _softmax_unnormalised(s):
    e = jnp.exp(s - jnp.max(s, axis=-1, keepdims=True))
    return e, jnp.sum(e, axis=-1, keepdims=True)


def _cb_kernel(q_ref, qn_ref, kt_ref, vt_ref, bias_ref, g_ref, o_ref, kk_ref, vv_ref, s_ref, *, steps):
    i = pl.program_id(2)
    lead = BAND // CB_TQ
    n_window = CB_TK // CB_TQ
    last_block = kk_ref.shape[1] - n_window

    def ones_row(h):
        return (1 - h) * HEAD_DIM

    def window(ref, h, block):
        return jnp.concatenate([ref[h, block + c] for c in range(n_window)], axis=1)

    def put_scores(slot, q, block):
        for h in range(2):
            s_ref[slot, h] = _dot(q, window(kk_ref, h, block))

    @pl.when(i == 0)
    def _():
        kk_ref[...] = jnp.zeros_like(kk_ref)
        row = lax.broadcasted_iota(jnp.int32, vv_ref.shape[1:], 1)
        for h in range(2):
            vv_ref[h] = jnp.where(row == ones_row(h), 1.0, 0.0).astype(BF16)
            rows = slice(h * HEAD_DIM, (h + 1) * HEAD_DIM)
            for c in range(kk_ref.shape[1] - lead):
                kk_ref[h, lead + c, rows, :] = kt_ref[0, h, :, c * CB_TQ:(c + 1) * CB_TQ].astype(BF16)
                vv_ref[h, lead + c, rows, :] = vt_ref[0, h, :, c * CB_TQ:(c + 1) * CB_TQ].astype(BF16)
        put_scores(0, q_ref[0, :CB_TQ, :], 0)

    key = lax.broadcasted_iota(jnp.int32, (CB_TQ, CB_TK), 1)
    first = lax.broadcasted_iota(jnp.int32, (CB_TQ, LANES), 1) < HEAD_DIM
    def numerators(g):
        block, es = i * CB_GROUPS + g, []
        for h in range(2):
            s = s_ref[g % 2, h] + bias_ref[h]
            if g < lead:
                s = jnp.where(key >= BAND - block * CB_TQ, s, NEG)
            es.append(jnp.exp(s - jnp.max(s, axis=-1, keepdims=True)).astype(BF16))
        return es

    def finish(g, es):
        block, o = i * CB_GROUPS + g, None
        for h in range(2):
            o_h = _dot_nt(es[h], window(vv_ref, h, block))
            o_h = o_h / o_h[:, ones_row(h):ones_row(h) + 1]
            o = o_h if h == 0 else jnp.where(first, o, o_h)
        o_ref[0, g * CB_TQ:(g + 1) * CB_TQ, :] = _head_pair_rms(o, g_ref[...]).astype(o_ref.dtype)

    pending = None
    for g in range(CB_GROUPS):
        block = i * CB_GROUPS + g
        es = numerators(g)
        if g + 1 < CB_GROUPS:
            put_scores((g + 1) % 2, q_ref[0, (g + 1) * CB_TQ:(g + 2) * CB_TQ, :], block + 1)
        elif steps > 1:
            put_scores((g + 1) % 2, qn_ref[0], jnp.minimum(block + 1, last_block))
        if pending is not None:
            finish(*pending)
        pending = (g, es)
    finish(*pending)


def _cb_call(q, kt, vt, bias, g, layer):
    b, s, w = q.shape
    blocks = (BAND + s) // CB_TQ
    step_rows = CB_GROUPS * CB_TQ
    n_q = s // CB_TQ
    return pl.pallas_call(
        functools.partial(_cb_kernel, steps=s // step_rows), grid=(b, w // LANES, s // step_rows),
        in_specs=[pl.BlockSpec((1, step_rows, LANES), lambda bi, pi, i: (bi, i, pi)),
                  pl.BlockSpec((1, CB_TQ, LANES), lambda bi, pi, i: (bi, jnp.minimum((i + 1) * CB_GROUPS, n_q - 1), pi)),
                  _pair_kv_spec(s, layer), _pair_kv_spec(s, layer),
                  pl.BlockSpec((2, CB_TQ, CB_TK), lambda bi, pi, i: (pi, 0, 0)),
                  pl.BlockSpec((None, 1, LANES), lambda bi, pi, i: (layer, 0, pi))],
        out_specs=pl.BlockSpec((1, step_rows, LANES), lambda bi, pi, i: (bi, i, pi)),
        out_shape=jax.ShapeDtypeStruct((b, s, w), BF16),
        scratch_shapes=[pltpu.VMEM((2, blocks, LANES, CB_TQ), BF16), pltpu.VMEM((2, blocks, LANES, CB_TQ), BF16),
                        pltpu.VMEM((2, 2, CB_TQ, CB_TK), F32)],
        compiler_params=_params("parallel", "parallel", "arbitrary"), name="cb_attention",
    )(q, q, kt, vt, bias, g)


def _bias_tiles_kernel(table_ref, o_ref, *, heads):
    layer, head = pl.program_id(0), pl.program_id(1)
    n_rel = 2 * REL_MAX + 1
    shape = o_ref.shape[2:]
    dist = (2 * CHUNK + lax.broadcasted_iota(jnp.int32, shape, 0) - lax.broadcasted_iota(jnp.int32, shape, 1))
    idx = jnp.clip(dist, -REL_MAX, REL_MAX) + REL_MAX

    def pick(r, acc):
        return jnp.where(idx == r, table_ref[(layer * n_rel + r) * heads + head], acc)
    lowest = REL_MAX + 2 * CHUNK - (shape[1] - 1)
    o_ref[0, 0] = lax.fori_loop(max(lowest, 0), n_rel, pick, jnp.zeros(shape, F32), unroll=8)


def _bias_tiles(rel_table):
    n_layers, _, heads = rel_table.shape
    return pl.pallas_call(
        functools.partial(_bias_tiles_kernel, heads=heads), grid=(n_layers, heads),
        in_specs=[pl.BlockSpec(memory_space=pltpu.SMEM)],
        out_specs=pl.BlockSpec((1, 1, CHUNK, 3 * CHUNK), lambda l, h: (l, h, 0, 0)),
        out_shape=jax.ShapeDtypeStruct((n_layers, heads, CHUNK, 3 * CHUNK), F32),
        compiler_params=_params("parallel", "parallel"), name="rel_bias_tiles",
    )(rel_table.reshape(-1))


def _prompt_bias(tiles, far):
    heads = tiles.shape[0]
    qc, kc = CB_TQ // CHUNK, CB_TK // CHUNK
    far_tile = jnp.broadcast_to(far[:, None, None], (heads, CHUNK, CHUNK))
    hidden = jnp.full((heads, CHUNK, CHUNK), NEG, F32)
    rows = []
    for c in range(qc):
        row = []
        for k in range(kc):
            m = k - c
            if m < 0 or m > BAND_CHUNKS:
                row.append(hidden)
            elif m < BAND_CHUNKS - 2:
                row.append(far_tile)
            else:
                t = m - (BAND_CHUNKS - 2)
                row.append(tiles[:, :, t * CHUNK:(t + 1) * CHUNK])
        rows.append(jnp.concatenate(row, axis=2))
    return jnp.concatenate(rows, axis=1)


def _head_rows(x, heads):
    t, w = x.shape
    tiled = jnp.concatenate([x] * heads, axis=0)
    row_head = lax.div(lax.broadcasted_iota(jnp.int32, tiled.shape, 0), t)
    col_head = lax.div(lax.broadcasted_iota(jnp.int32, tiled.shape, 1), HEAD_DIM)
    return jnp.where(row_head == col_head, tiled, jnp.zeros_like(tiled))


def _fold_head_rows(o, heads):
    t = o.shape[0] // heads
    col_head = lax.div(lax.broadcasted_iota(jnp.int32, (t, o.shape[1]), 1), HEAD_DIM)
    out = jnp.zeros((t, o.shape[1]), F32)
    for h in range(heads):
        out = jnp.where(col_head == h, o[h * t:(h + 1) * t], out)
    return out


def _all_heads_rms(o, g):
    parts = [_head_pair_rms(o[:, c:c + LANES], g[:, c:c + LANES]) for c in range(0, o.shape[1], LANES)]
    return jnp.concatenate(parts, axis=1)


def _pad_rows(x, rows):
    return jnp.concatenate([x, jnp.zeros((rows - x.shape[0], x.shape[1]), x.dtype)], axis=0)


def _sample_kernel(qa_ref, ka_ref, va_ref, ckt_ref, cvt_ref, qb_ref, kb_ref, vb_ref, ckbt_ref, cvbt_ref,
                   suffix_ref, bias_ref, ga_ref, gb_ref, oa_ref, ob_ref, carry_ref, acc_ref, live_ref):
    t, w = qa_ref.shape[1:]
    heads = w // HEAD_DIM
    tk = SB_TK
    past = ckt_ref.shape[2]

    carry_ref[...] = jnp.zeros_like(carry_ref)
    q = _head_rows(qa_ref[0], heads)
    suffix = suffix_ref[...]
    frame = lax.rem(lax.broadcasted_iota(jnp.int32, (heads * t, tk), 0), t)
    key = lax.broadcasted_iota(jnp.int32, (heads * t, tk), 1)
    wts = _sb_weights(_dot_nt(q, _pad_rows(ka_ref[0], tk).astype(BF16)), suffix, carry_ref, key < frame)
    acc_ref[...] = _dot(wts, _pad_rows(va_ref[0], tk).astype(BF16))
    def cached(blk):
        cols = slice(blk * tk, (blk + 1) * tk)
        wts = _sb_weights(_dot(q, ckt_ref[0, :, cols].astype(BF16)), suffix, carry_ref, None)
        acc_ref[...] += _dot_nt(wts, cvt_ref[0, :, cols].astype(BF16))

    live_ref[0] = jnp.int32(1)
    newest_first = list(reversed(range(past // tk)))
    for first in range(0, len(newest_first), 2):
        @pl.when(live_ref[0] != 0)
        def _(first=first):
            for blk in newest_first[first:first + 2]:
                cached(blk)
            live_ref[0] = (jnp.max(carry_ref[...]) > SB_DEAD).astype(jnp.int32)
    oa_ref[0] = _all_heads_rms(_fold_head_rows(acc_ref[...], heads), ga_ref[...]).astype(oa_ref.dtype)

    qb = _head_rows(qb_ref[0], heads)
    s = jnp.concatenate([_dot(qb, ckbt_ref[0].astype(BF16)),
                         _dot_nt(qb, _pad_rows(kb_ref[0], LANES).astype(BF16))], axis=1)
    e, l = _softmax_unnormalised(s + bias_ref[...])
    e = e.astype(BF16)
    band = ckbt_ref.shape[2]
    o = _dot_nt(e[:, :band], cvbt_ref[0].astype(BF16)) + _dot(e[:, band:], _pad_rows(vb_ref[0], LANES).astype(BF16))
    ob_ref[0] = _all_heads_rms(_fold_head_rows(o / l, heads), gb_ref[...]).astype(ob_ref.dtype)


def _sample_call(qa, ka, va, ckt, cvt, qb, kb, vb, ckbt, cvbt, suffix, bias, ga, gb, layer):
    b, t, w = qa.shape
    heads = w // HEAD_DIM
    new = pl.BlockSpec((1, t, w), lambda bi: (bi, 0, 0))

    def cache(c):
        return pl.BlockSpec((None, 1) + c.shape[2:], lambda bi: (layer, bi, 0, 0))
    return pl.pallas_call(
        _sample_kernel, grid=(b,),
        in_specs=[new, new, new, cache(ckt), cache(cvt), new, new, new, cache(ckbt), cache(cvbt),
                  _resident(suffix.shape), _resident(bias.shape), _resident(ga.shape[1:], layer),
                  _resident(gb.shape[1:], layer)],
        out_specs=[new, new], out_shape=[jax.ShapeDtypeStruct((b, t, w), BF16)] * 2,
        scratch_shapes=[pltpu.VMEM((heads * t, SB_TK), F32), pltpu.VMEM((heads * t, w), F32),
                        pltpu.SMEM((1,), jnp.int32)],
        compiler_params=_params("parallel"), name="sample_attention",
    )(qa, ka, va, ckt, cvt, qb, kb, vb, ckbt, cvbt, suffix, bias, ga, gb)


def _sample_bias(tiles, far, t):
    heads = tiles.shape[0]
    near = BAND - 2 * CHUNK
    parts = [jnp.broadcast_to(far[:, None, None], (heads, t, near)), tiles[:, :t, :2 * CHUNK + t],
             jnp.full((heads, t, LANES - t), NEG, F32)]
    return jnp.concatenate(parts, axis=2).reshape(heads * t, BAND + LANES)


def _per_head_t(cache):
    n_l, n_b, n_p, n_h, n_d = cache.shape
    return jnp.transpose(cache, (0, 1, 3, 4, 2)).reshape(n_l, n_b, n_h * n_d, n_p)


def kernel(x_prompt, x_sample, p_prompt, p_sample, cache_sb_k, cache_sb_v, cache_cb_k, cache_cb_v, g_mix, w_in,
           rel_table, g_out_sb, g_out_cb, w_out, g_ffn, w_gate, w_up, w_down, g_ple, w_ple_gate, w_ple_proj,
           g_final):
    depth = w_in.shape[0]
    b, s, d = x_prompt.shape
    bs, t, _ = x_sample.shape
    w = g_out_sb.shape[1]
    heads = w // HEAD_DIM
    past, band = cache_sb_k.shape[2], cache_cb_k.shape[2]
    assert g_out_cb.shape[1] == w and w_in.shape[2] == 6 * w and w % LANES == 0
    assert s % (SB_CHAINS * SB_TQ) == 0 and s % (CB_GROUPS * CB_TQ) == 0 and s >= BAND and s % ROW_TILE == 0 and s % QKV_TILE == 0
    assert band == BAND and past % SB_TK == 0 and t <= CHUNK and t % 8 == 0

    gain = lambda g: g[:, None, :]
    wts = dict(
        g_mix=gain(g_mix), g_ffn=gain(g_ffn), g_ple=gain(g_ple),
        w_in_t=jnp.swapaxes(w_in, 1, 2).astype(BF16),
        w_out=w_out.astype(BF16), w_gate=w_gate.astype(BF16), w_up=w_up.astype(BF16), w_down=w_down.astype(BF16),
        w_ple_gate=w_ple_gate.astype(BF16), w_ple_proj=w_ple_proj.astype(BF16))
    g_sb, g_cb = gain(g_out_sb), gain(g_out_cb)
    g_fin = g_final.reshape(1, d)
    suffix = _suffix_matrix(SB_TK)
    tiles = _bias_tiles(rel_table)
    far = rel_table[:, 2 * REL_MAX, :]
    caches = [_per_head_t(c) for c in (cache_sb_k, cache_sb_v, cache_cb_k, cache_cb_v)]
    pp = p_prompt.reshape(depth, b * s, -1)
    ps = p_sample.reshape(depth, bs * t, -1)

    xp = x_prompt.reshape(b * s, d)
    xs = x_sample.reshape(bs * t, d)
    qkv_p = _qkv_call(xp, wts, 0, QKV_TILE, s)
    qkv_s = _qkv_call(xs, wts, 0, bs * t, None)
    new_s = []
    for i in range(depth):
        qa, qb = [u.reshape(b, s, w) for u in qkv_p[:2]]
        kat, vat, kbt, vbt = kv_all = qkv_p[2:]
        a = _sb_call(qa, kat, vat, suffix, g_sb, i)
        ob = _cb_call(qb, kbt, vbt, _prompt_bias(tiles[i], far[i]), g_cb, i)

        sqa, sqb, ska, sva, skb, svb = [u.reshape(bs, t, w) for u in qkv_s]
        new_s.append((ska, sva, skb, svb))
        sa, sob = _sample_call(sqa, ska, sva, caches[0], caches[1], sqb, skb, svb, caches[2], caches[3],
                               suffix, _sample_bias(tiles[i], far[i], t), g_sb, g_cb, i)

        out_p = _layer_call(xp, a.reshape(b * s, w), ob.reshape(b * s, w), pp, wts, i, g_fin, ROW_TILE, s, kv_all,
                            band_tail=i + 2 == depth)
        out_s = _layer_call(xs, sa.reshape(bs * t, w), sob.reshape(bs * t, w), ps, wts, i, g_fin, bs * t, None)
        if i == depth - 1:
            xp, xs = out_p, out_s
        else:
            xp, qkv_p, band_tails = out_p[0], out_p[1:7], out_p[7:]
            xs, qkv_s = out_s[0], out_s[1:]
    if depth == 1:
        band_tails = [kbt[..., s - BAND:], vbt[..., s - BAND:]]

    def positions_major(kv_t):
        return jnp.transpose(kv_t, (0, 1, 4, 2, 3))

    def stack(items):
        return jnp.stack(items).reshape(depth, bs, t, heads, HEAD_DIM)

    return (xp.reshape(b, s, d), xs.reshape(bs, t, d),
            positions_major(kat), positions_major(vat),
            positions_major(band_tails[0]), positions_major(band_tails[1]),
            stack([e[0] for e in new_s]), stack([e[1] for e in new_s]),
            stack([e[2] for e in new_s]), stack([e[3] for e in new_s]))
```

```python
import functools

import jax
import jax.numpy as jnp
from jax import lax
from jax.experimental import pallas as pl
from jax.experimental.pallas import tpu as pltpu

F32 = jnp.float32
BF16 = jnp.bfloat16

HEAD_DIM = 64
CHUNK = 64
BAND_CHUNKS = 8
BAND = BAND_CHUNKS * CHUNK
REL_MAX = 128
EPS = 1e-6
Q_SCALE = HEAD_DIM ** -0.5

LANES = 128
NEG = -1e30

ROW_TILE = 256
LAST_ROW_TILE = 512
CHAIN_ROWS = 128
QKV_TILE = 512
SB_TQ = 256
SB_TK = 128
SB_CHAINS = 8
SB_DEAD = -120.0
CB_TQ = 256
CB_TK = CB_TQ + BAND
CB_GROUPS = 8
VMEM_LIMIT = 58 * 1024 * 1024


def _params(*semantics):
    return pltpu.CompilerParams(dimension_semantics=semantics, vmem_limit_bytes=VMEM_LIMIT)


def _resident(shape, layer=None):
    zeros = (0,) * len(shape)
    if layer is None:
        return pl.BlockSpec(shape, lambda *_: zeros, pipeline_mode=pl.Buffered(1))
    return pl.BlockSpec((None,) + tuple(shape), lambda *_: (layer,) + zeros, pipeline_mode=pl.Buffered(1))


def _rms_unit(x):
    return x * lax.rsqrt(jnp.mean(x * x, axis=-1, keepdims=True) + EPS)


def _dot(a, b):
    return jnp.dot(a, b, preferred_element_type=F32)


def _dot_nt(a, b):
    return lax.dot_general(a, b, (((1,), (1,)), ((), ())), preferred_element_type=F32)


def _softplus(z):
    return jnp.maximum(z, 0.0) + jnp.log(1.0 + jnp.exp(-jnp.abs(z)))


def _head_pair_rms(o, g):
    first = lax.broadcasted_iota(jnp.int32, o.shape, 1) < HEAD_DIM
    o2 = o * o
    s0 = jnp.sum(jnp.where(first, o2, 0.0), axis=-1, keepdims=True)
    s1 = jnp.sum(jnp.where(first, 0.0, o2), axis=-1, keepdims=True)
    ms = jnp.where(first, s0, s1) * (1.0 / HEAD_DIM)
    return o * lax.rsqrt(ms + EPS) * g


def _store_qkv(h, wt_ref, outs, per_head_t, prev=()):
    qa_ref, qb_ref = outs[:2]
    w = qa_ref.shape[-1]
    qa_ref[...] = (_dot_nt(h, wt_ref[0:w, :]) * Q_SCALE).astype(BF16)
    qb_ref[...] = (_dot_nt(h, wt_ref[3 * w:4 * w, :]) * Q_SCALE).astype(BF16)
    band_tails = outs[6:]
    for c, ref in enumerate(outs[2:6]):
        first_row = (1 + c + c // 2) * w
        w_t = wt_ref[first_row:first_row + w, :]
        if per_head_t:
            n_prev = ref.shape[0] - 1
            new = _dot_nt(w_t, h).reshape(ref.shape[2:])
            for target in [ref] + ([band_tails[c - 2]] if band_tails and c >= 2 else []):
                if n_prev:
                    target[:n_prev] = prev[c][...]
                target[n_prev, 0] = new
        else:
            ref[...] = _dot_nt(h, w_t)


def _qkv_kernel(x_ref, g_ref, wt_ref, *outs, per_head_t):
    h = (_rms_unit(x_ref[...]) * g_ref[...]).astype(BF16)
    _store_qkv(h, wt_ref, outs, per_head_t)


def _layer_tail(x_ref, a_ref, ob_ref, p_ref, wo_ref, gf_ref, wg_ref, wu_ref, wd_ref, gp_ref, wpg_ref, wpp_ref):
    w = a_ref.shape[-1]
    out = {}

    def rows_chain(c, rows):
        x = x_ref[rows, :] + _dot(a_ref[rows, :], wo_ref[:w, :]) + _dot(ob_ref[rows, :], wo_ref[w:, :])
        yield
        h = (_rms_unit(x) * gf_ref[...]).astype(BF16)
        yield
        gate = _dot(h, wg_ref[...])
        up = _dot(h, wu_ref[...])
        yield
        act = (gate * jax.nn.sigmoid(gate) * up).astype(BF16)
        x = x + _dot(act, wd_ref[...])
        yield
        hp = (_rms_unit(x) * gp_ref[...]).astype(BF16)
        yield
        ple_gate = jax.nn.sigmoid(_dot(hp, wpg_ref[...]))
        out[c] = x + ple_gate * _dot(p_ref[rows, :].astype(BF16), wpp_ref[...])

    n = x_ref.shape[0]
    chains = max(n // CHAIN_ROWS, 1)
    step = n // chains
    walks = [rows_chain(c, slice(c * step, (c + 1) * step)) for c in range(chains)]
    while walks:
        walks = [wk for wk in walks if next(wk, "done") != "done"]
    return jnp.concatenate([out[c] for c in range(chains)], axis=0)


N_TAIL = 12


def _mid_layer_kernel(*refs, per_head_t, n_prev):
    tail, (gn_ref, wt_ref) = refs[:N_TAIL], refs[N_TAIL:N_TAIL + 2]
    prev, xo_ref, outs = refs[N_TAIL + 2:N_TAIL + 2 + n_prev], refs[N_TAIL + 2 + n_prev], refs[N_TAIL + 3 + n_prev:]
    x = _layer_tail(*tail)
    xo_ref[...] = x
    _store_qkv((_rms_unit(x) * gn_ref[...]).astype(BF16), wt_ref, outs, per_head_t, prev)


def _last_layer_kernel(*refs):
    tail, (gn_ref, y_ref) = refs[:N_TAIL], refs[N_TAIL:]
    y_ref[...] = _rms_unit(_layer_tail(*tail)) * gn_ref[...]


def _row_spec(tm, width):
    return pl.BlockSpec((tm, width), lambda i: (i, 0))


def _kv_spec(layers, w, tm, seq):
    heads, tiles = w // HEAD_DIM, seq // tm
    return pl.BlockSpec((layers, 1, heads, HEAD_DIM, tm), lambda i: (0, i // tiles, 0, 0, i % tiles))


def _qkv_out(rows, w, tm, seq, layers):
    q_shapes = [jax.ShapeDtypeStruct((rows, w), BF16)] * 2
    q_specs = [_row_spec(tm, w)] * 2
    if seq is None:
        return q_shapes + [jax.ShapeDtypeStruct((rows, w), F32)] * 4, q_specs + [_row_spec(tm, w)] * 4
    kv_shape = jax.ShapeDtypeStruct((layers, rows // seq, w // HEAD_DIM, HEAD_DIM, seq), F32)
    return q_shapes + [kv_shape] * 4, q_specs + [_kv_spec(layers, w, tm, seq)] * 4


def _qkv_call(x, wts, layer, tm, seq):
    rows, d = x.shape
    wt, g = wts["w_in_t"], wts["g_mix"]
    out_shape, out_specs = _qkv_out(rows, wt.shape[1] // 6, tm, seq, layer + 1)
    return pl.pallas_call(
        functools.partial(_qkv_kernel, per_head_t=seq is not None), grid=(rows // tm,),
        in_specs=[_row_spec(tm, d), _resident(g.shape[1:], layer), _resident(wt.shape[1:], layer)],
        out_specs=out_specs, out_shape=out_shape,
        compiler_params=_params("parallel"), name="qkv_proj",
    )(x, g, wt)


TAIL_WEIGHTS = ("w_out", "g_ffn", "w_gate", "w_up", "w_down", "g_ple", "w_ple_gate", "w_ple_proj")


def _layer_call(x, a, ob, p, wts, layer, g_final, tm, seq, kv_prev=(), band_tail=False):
    rows, d = x.shape
    last = layer == wts["w_out"].shape[0] - 1
    tail_in = [x, a, ob, p] + [wts[k] for k in TAIL_WEIGHTS]
    tail_specs = ([_row_spec(tm, d), _row_spec(tm, a.shape[1]), _row_spec(tm, ob.shape[1]),
                   pl.BlockSpec((None, tm, p.shape[2]), lambda i: (layer, i, 0))]
                  + [_resident(wts[k].shape[1:], layer) for k in TAIL_WEIGHTS])
    if last:
        return pl.pallas_call(
            _last_layer_kernel, grid=(rows // tm,), in_specs=tail_specs + [_resident(g_final.shape)],
            out_specs=_row_spec(tm, d), out_shape=jax.ShapeDtypeStruct((rows, d), F32),
            compiler_params=_params("parallel"), name="layer_tail_final",
        )(*tail_in, g_final)
    nxt = [wts["g_mix"], wts["w_in_t"]]
    w = wts["w_in_t"].shape[1] // 6
    out_shape, out_specs = _qkv_out(rows, w, tm, seq, layer + 2)
    if band_tail:
        tiles, first_tail = seq // tm, (seq - BAND) // tm
        out_shape += [jax.ShapeDtypeStruct(out_shape[-1].shape[:-1] + (BAND,), F32)] * 2
        out_specs += [pl.BlockSpec(out_specs[-1].block_shape,
                                   lambda i: (0, i // tiles, 0, 0, jnp.maximum(i % tiles - first_tail, 0)))] * 2
    return pl.pallas_call(
        functools.partial(_mid_layer_kernel, per_head_t=seq is not None, n_prev=len(kv_prev)), grid=(rows // tm,),
        in_specs=(tail_specs + [_resident(t.shape[1:], layer + 1) for t in nxt]
                  + [_kv_spec(layer + 1, w, tm, seq) for _ in kv_prev]),
        out_specs=[_row_spec(tm, d)] + out_specs,
        out_shape=[jax.ShapeDtypeStruct((rows, d), F32)] + out_shape,
        compiler_params=_params("arbitrary" if band_tail else "parallel"), name="layer_tail_qkv",
    )(*tail_in, *nxt, *kv_prev)


def _suffix_matrix(tk):
    j = lax.broadcasted_iota(jnp.int32, (tk, 2 * tk), 0)
    c = lax.broadcasted_iota(jnp.int32, (tk, 2 * tk), 1)
    return jnp.where((c >= tk) | (j > c), -1.0, 0.0).astype(BF16)


def _sb_weights(z, suffix, carry_ref, causal, row0=0):
    tk = suffix.shape[0]
    sp = _softplus(z)
    w_parts = []
    for h in range(z.shape[1] // tk):
        cols = slice(h * tk, (h + 1) * tk)
        z_h, sp_h = z[:, cols], sp[:, cols]
        drop = sp_h if causal is None else jnp.where(causal, sp_h, 0.0)
        sums = _dot(drop.astype(BF16), suffix)
        carry = carry_ref[row0:, cols]
        log_w = (z_h - sp_h) + (sums[:, :tk] + carry)
        w = jnp.exp(log_w)
        if causal is not None:
            w = jnp.where(causal, w, 0.0)
        carry_ref[row0:, cols] = carry + sums[:, tk:]
        w_parts.append(w.astype(BF16))
    return jnp.concatenate(w_parts, axis=1)


def _sb_kernel(q_ref, kt_ref, vt_ref, suffix_ref, g_ref, o_ref, kk_ref, vv_ref, carry_ref, acc_ref, z_ref, w_ref,
               more_ref):
    tq, tk = SB_TQ, SB_TK
    per_q = tq // tk
    i = pl.program_id(2)

    @pl.when(i == 0)
    def _():
        kk_ref[...] = jnp.zeros_like(kk_ref)
        vv_ref[...] = jnp.zeros_like(vv_ref)
        for j in range(kk_ref.shape[0]):
            for h in range(2):
                rows, cols = slice(h * HEAD_DIM, (h + 1) * HEAD_DIM), slice(h * tk, (h + 1) * tk)
                kk_ref[j, rows, cols] = kt_ref[0, h, :, j * tk:(j + 1) * tk].astype(BF16)
                vv_ref[j, rows, cols] = vt_ref[0, h, :, j * tk:(j + 1) * tk].astype(BF16)

    carry_ref[...] = jnp.zeros_like(carry_ref)
    acc_ref[...] = jnp.zeros_like(acc_ref)
    suffix = suffix_ref[...]
    k_off = lax.broadcasted_iota(jnp.int32, (tq, tk), 1)
    row = lax.broadcasted_iota(jnp.int32, (tq, tk), 0)
    last_slot = (per_q - 1) % 2

    def row0(n):
        return (per_q - 1 - n) * tk if isinstance(n, int) and 0 <= n < per_q else 0

    class Chain:
        def __init__(self, c):
            self.c = c
            self.qi = i * SB_CHAINS + c
            self.rows = slice(c * tq, (c + 1) * tq)
            self.q = q_ref[0, self.rows, :]
            self.newest = (self.qi + 1) * per_q - 1
            self.carry, self.acc, self.z, self.w = carry_ref.at[c], acc_ref.at[c], z_ref.at[c], w_ref.at[c]

        def step(self, n, slot, diagonal, with_next=True):
            j = self.newest - n
            r_prev, r_cur, r_next = row0(n - 1) if diagonal else 0, row0(n), row0(n + 1)
            z = self.z[slot, r_cur:, :]
            if with_next:
                self.z[1 - slot, r_next:, :] = _dot(self.q[r_next:], kk_ref[jnp.maximum(j - 1, 0)])
            if not (diagonal and n == 0):
                self.acc[r_prev:, :] += _dot_nt(self.w[1 - slot, r_prev:, :], vv_ref[j + 1])
            causal = ((j * tk + k_off) < self.qi * tq + row)[r_cur:] if diagonal else None
            self.w[slot, r_cur:, :] = _sb_weights(z, suffix, self.carry, causal, r_cur)

        def emit(self, oldest):
            o = self.acc[...] + _dot_nt(self.w[last_slot], vv_ref[oldest])
            o_ref[0, self.rows, :] = _head_pair_rms(o, g_ref[...]).astype(o_ref.dtype)

        def first_steps(self, n_steps):
            self.z[0, row0(0):, :] = _dot(self.q[row0(0):], kk_ref[self.newest])
            yield
            for n in range(n_steps):
                self.step(n, n % 2, n < per_q, with_next=n + 1 < n_steps)
                yield
            self.emit(self.newest - (n_steps - 1))
            more = jnp.logical_and(jnp.max(self.carry[...]) > SB_DEAD, self.newest >= n_steps)
            more_ref[self.c] = more.astype(jnp.int32)

        def older_steps(self):
            @pl.when(more_ref[self.c] != 0)
            def _():
                self.z[0] = _dot(self.q, kk_ref[self.newest - 2 * per_q])

                def more(state):
                    t, live = state
                    return jnp.logical_and(t < self.qi, live)

                def trip(state):
                    t, _ = state
                    for d in range(per_q):
                        self.step(per_q * (t + 1) + d, d % 2, False)
                    return t + 1, jnp.max(self.carry[...]) > SB_DEAD
                trips, _ = lax.while_loop(more, trip, (jnp.int32(1), self.qi > 0))
                self.emit(per_q * (self.qi - trips))

    chains = [Chain(c) for c in range(SB_CHAINS)]

    def interleave(walks):
        while walks:
            walks = [w for w in walks if next(w, "done") != "done"]

    @pl.when(i == 0)
    def _():
        interleave([chains[0].first_steps(per_q)] + [ch.first_steps(2 * per_q) for ch in chains[1:]])

    @pl.when(i > 0)
    def _():
        interleave([ch.first_steps(2 * per_q) for ch in chains])

    for ch in chains:
        ch.older_steps()


def _pair_kv_spec(s, layer):
    return pl.BlockSpec((None, 1, 2, HEAD_DIM, s), lambda bi, pi, i: (layer, bi, pi, 0, 0))


def _sb_call(q, kt, vt, suffix, g, layer):
    b, s, w = q.shape
    nkb = s // SB_TK
    step_rows = SB_CHAINS * SB_TQ
    assert (SB_TQ // SB_TK) % 2 == 0 and SB_CHAINS >= 2
    return pl.pallas_call(
        _sb_kernel, grid=(b, w // LANES, s // step_rows),
        in_specs=[pl.BlockSpec((1, step_rows, LANES), lambda bi, pi, i: (bi, i, pi)),
                  _pair_kv_spec(s, layer), _pair_kv_spec(s, layer),
                  _resident(suffix.shape), pl.BlockSpec((None, 1, LANES), lambda bi, pi, i: (layer, 0, pi))],
        out_specs=pl.BlockSpec((1, step_rows, LANES), lambda bi, pi, i: (bi, i, pi)),
        out_shape=jax.ShapeDtypeStruct((b, s, w), BF16),
        scratch_shapes=[pltpu.VMEM((nkb, LANES, 2 * SB_TK), BF16), pltpu.VMEM((nkb, LANES, 2 * SB_TK), BF16),
                        pltpu.VMEM((SB_CHAINS, SB_TQ, 2 * SB_TK), F32), pltpu.VMEM((SB_CHAINS, SB_TQ, LANES), F32),
                        pltpu.VMEM((SB_CHAINS, 2, SB_TQ, 2 * SB_TK), F32),
                        pltpu.VMEM((SB_CHAINS, 2, SB_TQ, 2 * SB_TK), BF16), pltpu.SMEM((SB_CHAINS,), jnp.int32)],
        compiler_params=_params("parallel", "parallel", "arbitrary"), name="sb_attention",
    )(q, kt, vt, suffix, g)


def _softmax_unnormalised(s):
    e = jnp.exp(s - jnp.max(s, axis=-1, keepdims=True))
    return e, jnp.sum(e, axis=-1, keepdims=True)


def _cb_kernel(q_ref, qn_ref, kt_ref, vt_ref, bias_ref, g_ref, o_ref, kk_ref, vv_ref, s_ref, *, steps):
    i = pl.program_id(2)
    lead = BAND // CB_TQ
    n_window = CB_TK // CB_TQ
    last_block = kk_ref.shape[1] - n_window

    def ones_row(h):
        return (1 - h) * HEAD_DIM

    def window(ref, h, block):
        return jnp.concatenate([ref[h, block + c] for c in range(n_window)], axis=1)

    def put_scores(slot, q, block):
        for h in range(2):
            s_ref[slot, h] = _dot(q, window(kk_ref, h, block))

    @pl.when(i == 0)
    def _():
        kk_ref[...] = jnp.zeros_like(kk_ref)
        row = lax.broadcasted_iota(jnp.int32, vv_ref.shape[1:], 1)
        for h in range(2):
            vv_ref[h] = jnp.where(row == ones_row(h), 1.0, 0.0).astype(BF16)
            rows = slice(h * HEAD_DIM, (h + 1) * HEAD_DIM)
            for c in range(kk_ref.shape[1] - lead):
                kk_ref[h, lead + c, rows, :] = kt_ref[0, h, :, c * CB_TQ:(c + 1) * CB_TQ].astype(BF16)
                vv_ref[h, lead + c, rows, :] = vt_ref[0, h, :, c * CB_TQ:(c + 1) * CB_TQ].astype(BF16)
        put_scores(0, q_ref[0, :CB_TQ, :], 0)

    key = lax.broadcasted_iota(jnp.int32, (CB_TQ, CB_TK), 1)
    first = lax.broadcasted_iota(jnp.int32, (CB_TQ, LANES), 1) < HEAD_DIM
    def numerators(g):
        block, es = i * CB_GROUPS + g, []
        for h in range(2):
            s = s_ref[g % 2, h] + bias_ref[h]
            if g < lead:
                s = jnp.where(key >= BAND - block * CB_TQ, s, NEG)
            es.append(jnp.exp(s - jnp.max(s, axis=-1, keepdims=True)).astype(BF16))
        return es

    def finish(g, es):
        block, o = i * CB_GROUPS + g, None
        for h in range(2):
            o_h = _dot_nt(es[h], window(vv_ref, h, block))
            o_h = o_h / o_h[:, ones_row(h):ones_row(h) + 1]
            o = o_h if h == 0 else jnp.where(first, o, o_h)
        o_ref[0, g * CB_TQ:(g + 1) * CB_TQ, :] = _head_pair_rms(o, g_ref[...]).astype(o_ref.dtype)

    pending = None
    for g in range(CB_GROUPS):
        block = i * CB_GROUPS + g
        es = numerators(g)
        if g + 1 < CB_GROUPS:
            put_scores((g + 1) % 2, q_ref[0, (g + 1) * CB_TQ:(g + 2) * CB_TQ, :], block + 1)
        elif steps > 1:
            put_scores((g + 1) % 2, qn_ref[0], jnp.minimum(block + 1, last_block))
        if pending is not None:
            finish(*pending)
        pending = (g, es)
    finish(*pending)


def _cb_call(q, kt, vt, bias, g, layer):
    b, s, w = q.shape
    blocks = (BAND + s) // CB_TQ
    step_rows = CB_GROUPS * CB_TQ
    n_q = s // CB_TQ
    return pl.pallas_call(
        functools.partial(_cb_kernel, steps=s // step_rows), grid=(b, w // LANES, s // step_rows),
        in_specs=[pl.BlockSpec((1, step_rows, LANES), lambda bi, pi, i: (bi, i, pi)),
                  pl.BlockSpec((1, CB_TQ, LANES), lambda bi, pi, i: (bi, jnp.minimum((i + 1) * CB_GROUPS, n_q - 1), pi)),
                  _pair_kv_spec(s, layer), _pair_kv_spec(s, layer),
                  pl.BlockSpec((2, CB_TQ, CB_TK), lambda bi, pi, i: (pi, 0, 0)),
                  pl.BlockSpec((None, 1, LANES), lambda bi, pi, i: (layer, 0, pi))],
        out_specs=pl.BlockSpec((1, step_rows, LANES), lambda bi, pi, i: (bi, i, pi)),
        out_shape=jax.ShapeDtypeStruct((b, s, w), BF16),
        scratch_shapes=[pltpu.VMEM((2, blocks, LANES, CB_TQ), BF16), pltpu.VMEM((2, blocks, LANES, CB_TQ), BF16),
                        pltpu.VMEM((2, 2, CB_TQ, CB_TK), F32)],
        compiler_params=_params("parallel", "parallel", "arbitrary"), name="cb_attention",
    )(q, q, kt, vt, bias, g)


def _bias_tiles_kernel(table_ref, o_ref, *, heads):
    layer, head = pl.program_id(0), pl.program_id(1)
    n_rel = 2 * REL_MAX + 1
    shape = o_ref.shape[2:]
    dist = (2 * CHUNK + lax.broadcasted_iota(jnp.int32, shape, 0) - lax.broadcasted_iota(jnp.int32, shape, 1))
    idx = jnp.clip(dist, -REL_MAX, REL_MAX) + REL_MAX

    def pick(r, acc):
        return jnp.where(idx == r, table_ref[(layer * n_rel + r) * heads + head], acc)
    lowest = REL_MAX + 2 * CHUNK - (shape[1] - 1)
    o_ref[0, 0] = lax.fori_loop(max(lowest, 0), n_rel, pick, jnp.zeros(shape, F32), unroll=8)


def _bias_tiles(rel_table):
    n_layers, _, heads = rel_table.shape
    return pl.pallas_call(
        functools.partial(_bias_tiles_kernel, heads=heads), grid=(n_layers, heads),
        in_specs=[pl.BlockSpec(memory_space=pltpu.SMEM)],
        out_specs=pl.BlockSpec((1, 1, CHUNK, 3 * CHUNK), lambda l, h: (l, h, 0, 0)),
        out_shape=jax.ShapeDtypeStruct((n_layers, heads, CHUNK, 3 * CHUNK), F32),
        compiler_params=_params("parallel", "parallel"), name="rel_bias_tiles",
    )(rel_table.reshape(-1))


def _prompt_bias(tiles, far):
    heads = tiles.shape[0]
    qc, kc = CB_TQ // CHUNK, CB_TK // CHUNK
    far_tile = jnp.broadcast_to(far[:, None, None], (heads, CHUNK, CHUNK))
    hidden = jnp.full((heads, CHUNK, CHUNK), NEG, F32)
    rows = []
    for c in range(qc):
        row = []
        for k in range(kc):
            m = k - c
            if m < 0 or m > BAND_CHUNKS:
                row.append(hidden)
            elif m < BAND_CHUNKS - 2:
                row.append(far_tile)
            else:
                t = m - (BAND_CHUNKS - 2)
                row.append(tiles[:, :, t * CHUNK:(t + 1) * CHUNK])
        rows.append(jnp.concatenate(row, axis=2))
    return jnp.concatenate(rows, axis=1)


def _head_rows(x, heads):
    t, w = x.shape
    tiled = jnp.concatenate([x] * heads, axis=0)
    row_head = lax.div(lax.broadcasted_iota(jnp.int32, tiled.shape, 0), t)
    col_head = lax.div(lax.broadcasted_iota(jnp.int32, tiled.shape, 1), HEAD_DIM)
    return jnp.where(row_head == col_head, tiled, jnp.zeros_like(tiled))


def _fold_head_rows(o, heads):
    t = o.shape[0] // heads
    col_head = lax.div(lax.broadcasted_iota(jnp.int32, (t, o.shape[1]), 1), HEAD_DIM)
    out = jnp.zeros((t, o.shape[1]), F32)
    for h in range(heads):
        out = jnp.where(col_head == h, o[h * t:(h + 1) * t], out)
    return out


def _all_heads_rms(o, g):
    parts = [_head_pair_rms(o[:, c:c + LANES], g[:, c:c + LANES]) for c in range(0, o.shape[1], LANES)]
    return jnp.concatenate(parts, axis=1)


def _pad_rows(x, rows):
    return jnp.concatenate([x, jnp.zeros((rows - x.shape[0], x.shape[1]), x.dtype)], axis=0)


def _sample_kernel(qa_ref, ka_ref, va_ref, ckt_ref, cvt_ref, qb_ref, kb_ref, vb_ref, ckbt_ref, cvbt_ref,
                   suffix_ref, bias_ref, ga_ref, gb_ref, oa_ref, ob_ref, carry_ref, acc_ref, live_ref):
    t, w = qa_ref.shape[1:]
    heads = w // HEAD_DIM
    tk = SB_TK
    past = ckt_ref.shape[2]

    carry_ref[...] = jnp.zeros_like(carry_ref)
    q = _head_rows(qa_ref[0], heads)
    suffix = suffix_ref[...]
    frame = lax.rem(lax.broadcasted_iota(jnp.int32, (heads * t, tk), 0), t)
    key = lax.broadcasted_iota(jnp.int32, (heads * t, tk), 1)
    wts = _sb_weights(_dot_nt(q, _pad_rows(ka_ref[0], tk).astype(BF16)), suffix, carry_ref, key < frame)
    acc_ref[...] = _dot(wts, _pad_rows(va_ref[0], tk).astype(BF16))
    def cached(blk):
        cols = slice(blk * tk, (blk + 1) * tk)
        wts = _sb_weights(_dot(q, ckt_ref[0, :, cols].astype(BF16)), suffix, carry_ref, None)
        acc_ref[...] += _dot_nt(wts, cvt_ref[0, :, cols].astype(BF16))

    live_ref[0] = jnp.int32(1)
    newest_first = list(reversed(range(past // tk)))
    for first in range(0, len(newest_first), 2):
        @pl.when(live_ref[0] != 0)
        def _(first=first):
            for blk in newest_first[first:first + 2]:
                cached(blk)
            live_ref[0] = (jnp.max(carry_ref[...]) > SB_DEAD).astype(jnp.int32)
    oa_ref[0] = _all_heads_rms(_fold_head_rows(acc_ref[...], heads), ga_ref[...]).astype(oa_ref.dtype)

    qb = _head_rows(qb_ref[0], heads)
    s = jnp.concatenate([_dot(qb, ckbt_ref[0].astype(BF16)),
                         _dot_nt(qb, _pad_rows(kb_ref[0], LANES).astype(BF16))], axis=1)
    e, l = _softmax_unnormalised(s + bias_ref[...])
    e = e.astype(BF16)
    band = ckbt_ref.shape[2]
    o = _dot_nt(e[:, :band], cvbt_ref[0].astype(BF16)) + _dot(e[:, band:], _pad_rows(vb_ref[0], LANES).astype(BF16))
    ob_ref[0] = _all_heads_rms(_fold_head_rows(o / l, heads), gb_ref[...]).astype(ob_ref.dtype)


def _sample_call(qa, ka, va, ckt, cvt, qb, kb, vb, ckbt, cvbt, suffix, bias, ga, gb, layer):
    b, t, w = qa.shape
    heads = w // HEAD_DIM
    new = pl.BlockSpec((1, t, w), lambda bi: (bi, 0, 0))

    def cache(c):
        return pl.BlockSpec((None, 1) + c.shape[2:], lambda bi: (layer, bi, 0, 0))
    return pl.pallas_call(
        _sample_kernel, grid=(b,),
        in_specs=[new, new, new, cache(ckt), cache(cvt), new, new, new, cache(ckbt), cache(cvbt),
                  _resident(suffix.shape), _resident(bias.shape), _resident(ga.shape[1:], layer),
                  _resident(gb.shape[1:], layer)],
        out_specs=[new, new], out_shape=[jax.ShapeDtypeStruct((b, t, w), BF16)] * 2,
        scratch_shapes=[pltpu.VMEM((heads * t, SB_TK), F32), pltpu.VMEM((heads * t, w), F32),
                        pltpu.SMEM((1,), jnp.int32)],
        compiler_params=_params("parallel"), name="sample_attention",
    )(qa, ka, va, ckt, cvt, qb, kb, vb, ckbt, cvbt, suffix, bias, ga, gb)


def _sample_bias(tiles, far, t):
    heads = tiles.shape[0]
    near = BAND - 2 * CHUNK
    parts = [jnp.broadcast_to(far[:, None, None], (heads, t, near)), tiles[:, :t, :2 * CHUNK + t],
             jnp.full((heads, t, LANES - t), NEG, F32)]
    return jnp.concatenate(parts, axis=2).reshape(heads * t, BAND + LANES)


def _per_head_t(cache):
    n_l, n_b, n_p, n_h, n_d = cache.shape
    return jnp.transpose(cache, (0, 1, 3, 4, 2)).reshape(n_l, n_b, n_h * n_d, n_p)


def kernel(x_prompt, x_sample, p_prompt, p_sample, cache_sb_k, cache_sb_v, cache_cb_k, cache_cb_v, g_mix, w_in,
           rel_table, g_out_sb, g_out_cb, w_out, g_ffn, w_gate, w_up, w_down, g_ple, w_ple_gate, w_ple_proj,
           g_final):
    depth = w_in.shape[0]
    b, s, d = x_prompt.shape
    bs, t, _ = x_sample.shape
    w = g_out_sb.shape[1]
    heads = w // HEAD_DIM
    past, band = cache_sb_k.shape[2], cache_cb_k.shape[2]
    assert g_out_cb.shape[1] == w and w_in.shape[2] == 6 * w and w % LANES == 0
    assert s % (SB_CHAINS * SB_TQ) == 0 and s % (CB_GROUPS * CB_TQ) == 0 and s >= BAND and s % ROW_TILE == 0 and s % LAST_ROW_TILE == 0 and s % QKV_TILE == 0
    assert band == BAND and past % SB_TK == 0 and t <= CHUNK and t % 8 == 0

    gain = lambda g: g[:, None, :]
    wts = dict(
        g_mix=gain(g_mix), g_ffn=gain(g_ffn), g_ple=gain(g_ple),
        w_in_t=jnp.swapaxes(w_in, 1, 2).astype(BF16),
        w_out=w_out.astype(BF16), w_gate=w_gate.astype(BF16), w_up=w_up.astype(BF16), w_down=w_down.astype(BF16),
        w_ple_gate=w_ple_gate.astype(BF16), w_ple_proj=w_ple_proj.astype(BF16))
    g_sb, g_cb = gain(g_out_sb), gain(g_out_cb)
    g_fin = g_final.reshape(1, d)
    suffix = _suffix_matrix(SB_TK)
    tiles = _bias_tiles(rel_table)
    far = rel_table[:, 2 * REL_MAX, :]
    caches = [_per_head_t(c) for c in (cache_sb_k, cache_sb_v, cache_cb_k, cache_cb_v)]
    pp = p_prompt.reshape(depth, b * s, -1)
    ps = p_sample.reshape(depth, bs * t, -1)

    xp = x_prompt.reshape(b * s, d)
    xs = x_sample.reshape(bs * t, d)
    qkv_p = _qkv_call(xp, wts, 0, QKV_TILE, s)
    qkv_s = _qkv_call(xs, wts, 0, bs * t, None)
    new_s = []
    for i in range(depth):
        qa, qb = [u.reshape(b, s, w) for u in qkv_p[:2]]
        kat, vat, kbt, vbt = kv_all = qkv_p[2:]
        a = _sb_call(qa, kat, vat, suffix, g_sb, i)
        ob = _cb_call(qb, kbt, vbt, _prompt_bias(tiles[i], far[i]), g_cb, i)

        sqa, sqb, ska, sva, skb, svb = [u.reshape(bs, t, w) for u in qkv_s]
        new_s.append((ska, sva, skb, svb))
        sa, sob = _sample_call(sqa, ska, sva, caches[0], caches[1], sqb, skb, svb, caches[2], caches[3],
                               suffix, _sample_bias(tiles[i], far[i], t), g_sb, g_cb, i)

        out_p = _layer_call(xp, a.reshape(b * s, w), ob.reshape(b * s, w), pp, wts, i, g_fin,
                            LAST_ROW_TILE if i == depth - 1 else ROW_TILE, s, kv_all, band_tail=i + 2 == depth)
        out_s = _layer_call(xs, sa.reshape(bs * t, w), sob.reshape(bs * t, w), ps, wts, i, g_fin, bs * t, None)
        if i == depth - 1:
            xp, xs = out_p, out_s
        else:
            xp, qkv_p, band_tails = out_p[0], out_p[1:7], out_p[7:]
            xs, qkv_s = out_s[0], out_s[1:]
    if depth == 1:
        band_tails = [kbt[..., s - BAND:], vbt[..., s - BAND:]]

    def positions_major(kv_t):
        return jnp.transpose(kv_t, (0, 1, 4, 2, 3))

    def stack(items):
        return jnp.stack(items).reshape(depth, bs, t, heads, HEAD_DIM)

    return (xp.reshape(b, s, d), xs.reshape(bs, t, d),
            positions_major(kat), positions_major(vat),
            positions_major(band_tails[0]), positions_major(band_tails[1]),
            stack([e[0] for e in new_s]), stack([e[1] for e in new_s]),
            stack([e[2] for e in new_s]), stack([e[3] for e in new_s]))
```

```python
import functools

import jax
import jax.numpy as jnp
from jax import lax
from jax.experimental import pallas as pl
from jax.experimental.pallas import tpu as pltpu

F32 = jnp.float32
BF16 = jnp.bfloat16

HEAD_DIM = 64
CHUNK = 64
BAND_CHUNKS = 8
BAND = BAND_CHUNKS * CHUNK
REL_MAX = 128
EPS = 1e-6
Q_SCALE = HEAD_DIM ** -0.5

LANES = 128
NEG = -1e30

ROW_TILE = 256
LAST_ROW_TILE = 512
CHAIN_ROWS = 128
QKV_TILE = 512
SB_TQ = 256
SB_TK = 128
SB_CHAINS = 8
SB_DEAD = -120.0
CB_TQ = 256
CB_TK = CB_TQ + BAND
CB_GROUPS = 8
VMEM_LIMIT = 58 * 1024 * 1024


def _params(*semantics):
    return pltpu.CompilerParams(dimension_semantics=semantics, vmem_limit_bytes=VMEM_LIMIT)


def _resident(shape, layer=None):
    zeros = (0,) * len(shape)
    if layer is None:
        return pl.BlockSpec(shape, lambda *_: zeros, pipeline_mode=pl.Buffered(1))
    return pl.BlockSpec((None,) + tuple(shape), lambda *_: (layer,) + zeros, pipeline_mode=pl.Buffered(1))


def _rms_unit(x):
    return x * lax.rsqrt(jnp.mean(x * x, axis=-1, keepdims=True) + EPS)


def _dot(a, b):
    return jnp.dot(a, b, preferred_element_type=F32)


def _dot_nt(a, b):
    return lax.dot_general(a, b, (((1,), (1,)), ((), ())), preferred_element_type=F32)


def _softplus(z):
    return jnp.maximum(z, 0.0) + jnp.log(1.0 + jnp.exp(-jnp.abs(z)))


def _head_pair_rms(o, g):
    first = lax.broadcasted_iota(jnp.int32, o.shape, 1) < HEAD_DIM
    o2 = o * o
    s0 = jnp.sum(jnp.where(first, o2, 0.0), axis=-1, keepdims=True)
    s1 = jnp.sum(jnp.where(first, 0.0, o2), axis=-1, keepdims=True)
    ms = jnp.where(first, s0, s1) * (1.0 / HEAD_DIM)
    return o * lax.rsqrt(ms + EPS) * g


def _store_qkv(h, wt_ref, outs, per_head_t, prev=()):
    qa_ref, qb_ref = outs[:2]
    w = qa_ref.shape[-1]
    qa_ref[...] = (_dot_nt(h, wt_ref[0:w, :]) * Q_SCALE).astype(BF16)
    qb_ref[...] = (_dot_nt(h, wt_ref[3 * w:4 * w, :]) * Q_SCALE).astype(BF16)
    band_tails = outs[6:]
    for c, ref in enumerate(outs[2:6]):
        first_row = (1 + c + c // 2) * w
        w_t = wt_ref[first_row:first_row + w, :]
        if per_head_t:
            n_prev = ref.shape[0] - 1
            new = _dot_nt(w_t, h).reshape(ref.shape[2:])
            for target in [ref] + ([band_tails[c - 2]] if band_tails and c >= 2 else []):
                if n_prev:
                    target[:n_prev] = prev[c][...]
                target[n_prev, 0] = new
        else:
            ref[...] = _dot_nt(h, w_t)


def _qkv_kernel(x_ref, g_ref, wt_ref, *outs, per_head_t):
    h = (_rms_unit(x_ref[...]) * g_ref[...]).astype(BF16)
    _store_qkv(h, wt_ref, outs, per_head_t)


def _layer_tail(x_ref, a_ref, ob_ref, p_ref, wo_ref, gf_ref, wg_ref, wu_ref, wd_ref, gp_ref, wpg_ref, wpp_ref):
    w = a_ref.shape[-1]
    out = {}

    def rows_chain(c, rows):
        x = x_ref[rows, :] + _dot(a_ref[rows, :], wo_ref[:w, :]) + _dot(ob_ref[rows, :], wo_ref[w:, :])
        yield
        h = (_rms_unit(x) * gf_ref[...]).astype(BF16)
        yield
        gate = _dot(h, wg_ref[...])
        up = _dot(h, wu_ref[...])
        yield
        act = (gate * jax.nn.sigmoid(gate) * up).astype(BF16)
        x = x + _dot(act, wd_ref[...])
        yield
        hp = (_rms_unit(x) * gp_ref[...]).astype(BF16)
        yield
        ple_gate = jax.nn.sigmoid(_dot(hp, wpg_ref[...]))
        out[c] = x + ple_gate * _dot(p_ref[rows, :].astype(BF16), wpp_ref[...])

    n = x_ref.shape[0]
    chains = max(n // CHAIN_ROWS, 1)
    step = n // chains
    walks = [rows_chain(c, slice(c * step, (c + 1) * step)) for c in range(chains)]
    while walks:
        walks = [wk for wk in walks if next(wk, "done") != "done"]
    return jnp.concatenate([out[c] for c in range(chains)], axis=0)


N_TAIL = 12


def _mid_layer_kernel(*refs, per_head_t, n_prev):
    tail, (gn_ref, wt_ref) = refs[:N_TAIL], refs[N_TAIL:N_TAIL + 2]
    prev, xo_ref, outs = refs[N_TAIL + 2:N_TAIL + 2 + n_prev], refs[N_TAIL + 2 + n_prev], refs[N_TAIL + 3 + n_prev:]
    x = _layer_tail(*tail)
    xo_ref[...] = x
    _store_qkv((_rms_unit(x) * gn_ref[...]).astype(BF16), wt_ref, outs, per_head_t, prev)


def _last_layer_kernel(*refs):
    tail, (gn_ref, y_ref) = refs[:N_TAIL], refs[N_TAIL:]
    y_ref[...] = _rms_unit(_layer_tail(*tail)) * gn_ref[...]


def _row_spec(tm, width):
    return pl.BlockSpec((tm, width), lambda i: (i, 0))


def _kv_spec(layers, w, tm, seq):
    heads, tiles = w // HEAD_DIM, seq // tm
    return pl.BlockSpec((layers, 1, heads, HEAD_DIM, tm), lambda i: (0, i // tiles, 0, 0, i % tiles))


def _qkv_out(rows, w, tm, seq, layers):
    q_shapes = [jax.ShapeDtypeStruct((rows, w), BF16)] * 2
    q_specs = [_row_spec(tm, w)] * 2
    if seq is None:
        return q_shapes + [jax.ShapeDtypeStruct((rows, w), F32)] * 4, q_specs + [_row_spec(tm, w)] * 4
    kv_shape = jax.ShapeDtypeStruct((layers, rows // seq, w // HEAD_DIM, HEAD_DIM, seq), F32)
    return q_shapes + [kv_shape] * 4, q_specs + [_kv_spec(layers, w, tm, seq)] * 4


def _qkv_call(x, wts, layer, tm, seq):
    rows, d = x.shape
    wt, g = wts["w_in_t"], wts["g_mix"]
    out_shape, out_specs = _qkv_out(rows, wt.shape[1] // 6, tm, seq, layer + 1)
    return pl.pallas_call(
        functools.partial(_qkv_kernel, per_head_t=seq is not None), grid=(rows // tm,),
        in_specs=[_row_spec(tm, d), _resident(g.shape[1:], layer), _resident(wt.shape[1:], layer)],
        out_specs=out_specs, out_shape=out_shape,
        compiler_params=_params("parallel"), name="qkv_proj",
    )(x, g, wt)


TAIL_WEIGHTS = ("w_out", "g_ffn", "w_gate", "w_up", "w_down", "g_ple", "w_ple_gate", "w_ple_proj")


def _layer_call(x, a, ob, p, wts, layer, g_final, tm, seq, kv_prev=(), band_tail=False):
    rows, d = x.shape
    last = layer == wts["w_out"].shape[0] - 1
    tail_in = [x, a, ob, p] + [wts[k] for k in TAIL_WEIGHTS]
    tail_specs = ([_row_spec(tm, d), _row_spec(tm, a.shape[1]), _row_spec(tm, ob.shape[1]),
                   pl.BlockSpec((None, tm, p.shape[2]), lambda i: (layer, i, 0))]
                  + [_resident(wts[k].shape[1:], layer) for k in TAIL_WEIGHTS])
    if last:
        return pl.pallas_call(
            _last_layer_kernel, grid=(rows // tm,), in_specs=tail_specs + [_resident(g_final.shape)],
            out_specs=_row_spec(tm, d), out_shape=jax.ShapeDtypeStruct((rows, d), F32),
            compiler_params=_params("parallel"), name="layer_tail_final",
        )(*tail_in, g_final)
    nxt = [wts["g_mix"], wts["w_in_t"]]
    w = wts["w_in_t"].shape[1] // 6
    out_shape, out_specs = _qkv_out(rows, w, tm, seq, layer + 2)
    if band_tail:
        tiles, first_tail = seq // tm, (seq - BAND) // tm
        out_shape += [jax.ShapeDtypeStruct(out_shape[-1].shape[:-1] + (BAND,), F32)] * 2
        out_specs += [pl.BlockSpec(out_specs[-1].block_shape,
                                   lambda i: (0, i // tiles, 0, 0, jnp.maximum(i % tiles - first_tail, 0)))] * 2
    return pl.pallas_call(
        functools.partial(_mid_layer_kernel, per_head_t=seq is not None, n_prev=len(kv_prev)), grid=(rows // tm,),
        in_specs=(tail_specs + [_resident(t.shape[1:], layer + 1) for t in nxt]
                  + [_kv_spec(layer + 1, w, tm, seq) for _ in kv_prev]),
        out_specs=[_row_spec(tm, d)] + out_specs,
        out_shape=[jax.ShapeDtypeStruct((rows, d), F32)] + out_shape,
        compiler_params=_params("arbitrary" if band_tail else "parallel"), name="layer_tail_qkv",
    )(*tail_in, *nxt, *kv_prev)


def _suffix_matrix(tk):
    j = lax.broadcasted_iota(jnp.int32, (tk, 2 * tk), 0)
    c = lax.broadcasted_iota(jnp.int32, (tk, 2 * tk), 1)
    return jnp.where((c >= tk) | (j > c), -1.0, 0.0).astype(BF16)


def _sb_weights(z, suffix, carry_ref, causal, row0=0):
    tk = suffix.shape[0]
    sp = _softplus(z)
    w_parts = []
    for h in range(z.shape[1] // tk):
        cols = slice(h * tk, (h + 1) * tk)
        z_h, sp_h = z[:, cols], sp[:, cols]
        drop = sp_h if causal is None else jnp.where(causal, sp_h, 0.0)
        sums = _dot(drop.astype(BF16), suffix)
        carry = carry_ref[row0:, cols]
        log_w = (z_h - sp_h) + (sums[:, :tk] + carry)
        w = jnp.exp(log_w)
        if causal is not None:
            w = jnp.where(causal, w, 0.0)
        carry_ref[row0:, cols] = carry + sums[:, tk:]
        w_parts.append(w.astype(BF16))
    return jnp.concatenate(w_parts, axis=1)


def _sb_kernel(q_ref, kt_ref, vt_ref, suffix_ref, g_ref, o_ref, kk_ref, vv_ref, carry_ref, acc_ref, z_ref, w_ref,
               more_ref):
    tq, tk = SB_TQ, SB_TK
    per_q = tq // tk
    i = pl.program_id(2)

    @pl.when(i == 0)
    def _():
        kk_ref[...] = jnp.zeros_like(kk_ref)
        vv_ref[...] = jnp.zeros_like(vv_ref)
        for j in range(kk_ref.shape[0]):
            for h in range(2):
                rows, cols = slice(h * HEAD_DIM, (h + 1) * HEAD_DIM), slice(h * tk, (h + 1) * tk)
                kk_ref[j, rows, cols] = kt_ref[0, h, :, j * tk:(j + 1) * tk].astype(BF16)
                vv_ref[j, rows, cols] = vt_ref[0, h, :, j * tk:(j + 1) * tk].astype(BF16)

    carry_ref[...] = jnp.zeros_like(carry_ref)
    acc_ref[...] = jnp.zeros_like(acc_ref)
    suffix = suffix_ref[...]
    k_off = lax.broadcasted_iota(jnp.int32, (tq, tk), 1)
    row = lax.broadcasted_iota(jnp.int32, (tq, tk), 0)
    last_slot = (per_q - 1) % 2

    def row0(n):
        return (per_q - 1 - n) * tk if isinstance(n, int) and 0 <= n < per_q else 0

    class Chain:
        def __init__(self, c):
            self.c = c
            self.qi = i * SB_CHAINS + c
            self.rows = slice(c * tq, (c + 1) * tq)
            self.q = q_ref[0, self.rows, :]
            self.newest = (self.qi + 1) * per_q - 1
            self.carry, self.acc, self.z, self.w = carry_ref.at[c], acc_ref.at[c], z_ref.at[c], w_ref.at[c]

        def step(self, n, slot, diagonal, with_next=True):
            j = self.newest - n
            r_prev, r_cur, r_next = row0(n - 1) if diagonal else 0, row0(n), row0(n + 1)
            z = self.z[slot, r_cur:, :]
            if with_next:
                self.z[1 - slot, r_next:, :] = _dot(self.q[r_next:], kk_ref[jnp.maximum(j - 1, 0)])
            if not (diagonal and n == 0):
                self.acc[r_prev:, :] += _dot_nt(self.w[1 - slot, r_prev:, :], vv_ref[j + 1])
            causal = ((j * tk + k_off) < self.qi * tq + row)[r_cur:] if diagonal else None
            self.w[slot, r_cur:, :] = _sb_weights(z, suffix, self.carry, causal, r_cur)

        def emit(self, oldest):
            o = self.acc[...] + _dot_nt(self.w[last_slot], vv_ref[oldest])
            o_ref[0, self.rows, :] = _head_pair_rms(o, g_ref[...]).astype(o_ref.dtype)

        def first_steps(self, n_steps):
            self.z[0, row0(0):, :] = _dot(self.q[row0(0):], kk_ref[self.newest])
            yield
            for n in range(n_steps):
                self.step(n, n % 2, n < per_q, with_next=n + 1 < n_steps)
                yield
            self.emit(self.newest - (n_steps - 1))
            more = jnp.logical_and(jnp.max(self.carry[...]) > SB_DEAD, self.newest >= n_steps)
            more_ref[self.c] = more.astype(jnp.int32)

        def older_steps(self):
            @pl.when(more_ref[self.c] != 0)
            def _():
                self.z[0] = _dot(self.q, kk_ref[self.newest - 2 * per_q])

                def more(state):
                    t, live = state
                    return jnp.logical_and(t < self.qi, live)

                def trip(state):
                    t, _ = state
                    for d in range(per_q):
                        self.step(per_q * (t + 1) + d, d % 2, False)
                    return t + 1, jnp.max(self.carry[...]) > SB_DEAD
                trips, _ = lax.while_loop(more, trip, (jnp.int32(1), self.qi > 0))
                self.emit(per_q * (self.qi - trips))

    chains = [Chain(c) for c in range(SB_CHAINS)]

    def interleave(walks):
        while walks:
            walks = [w for w in walks if next(w, "done") != "done"]

    @pl.when(i == 0)
    def _():
        interleave([chains[0].first_steps(per_q)] + [ch.first_steps(2 * per_q) for ch in chains[1:]])

    @pl.when(i > 0)
    def _():
        interleave([ch.first_steps(2 * per_q) for ch in chains])

    for ch in chains:
        ch.older_steps()


def _pair_kv_spec(s, layer):
    return pl.BlockSpec((None, 1, 2, HEAD_DIM, s), lambda bi, pi, i: (layer, bi, pi, 0, 0))


def _sb_call(q, kt, vt, suffix, g, layer):
    b, s, w = q.shape
    nkb = s // SB_TK
    step_rows = SB_CHAINS * SB_TQ
    assert (SB_TQ // SB_TK) % 2 == 0 and SB_CHAINS >= 2
    return pl.pallas_call(
        _sb_kernel, grid=(b, w // LANES, s // step_rows),
        in_specs=[pl.BlockSpec((1, step_rows, LANES), lambda bi, pi, i: (bi, i, pi)),
                  _pair_kv_spec(s, layer), _pair_kv_spec(s, layer),
                  _resident(suffix.shape), pl.BlockSpec((None, 1, LANES), lambda bi, pi, i: (layer, 0, pi))],
        out_specs=pl.BlockSpec((1, step_rows, LANES), lambda bi, pi, i: (bi, i, pi)),
        out_shape=jax.ShapeDtypeStruct((b, s, w), BF16),
        scratch_shapes=[pltpu.VMEM((nkb, LANES, 2 * SB_TK), BF16), pltpu.VMEM((nkb, LANES, 2 * SB_TK), BF16),
                        pltpu.VMEM((SB_CHAINS, SB_TQ, 2 * SB_TK), F32), pltpu.VMEM((SB_CHAINS, SB_TQ, LANES), F32),
                        pltpu.VMEM((SB_CHAINS, 2, SB_TQ, 2 * SB_TK), F32),
                        pltpu.VMEM((SB_CHAINS, 2, SB_TQ, 2 * SB_TK), BF16), pltpu.SMEM((SB_CHAINS,), jnp.int32)],
        compiler_params=_params("parallel", "parallel", "arbitrary"), name="sb_attention",
    )(q, kt, vt, suffix, g)


def _softmax_unnormalised(s):
    e = jnp.exp(s - jnp.max(s, axis=-1, keepdims=True))
    return e, jnp.sum(e, axis=-1, keepdims=True)


def _cb_kernel(q_ref, kt_ref, vt_ref, bias_ref, g_ref, o_ref, kk_ref, vv_ref, s_ref):
    i = pl.program_id(2)
    lead = BAND // CB_TQ
    n_window = CB_TK // CB_TQ

    def ones_row(h):
        return (1 - h) * HEAD_DIM

    def window(ref, h, block):
        return jnp.concatenate([ref[h, block + c] for c in range(n_window)], axis=1)

    @pl.when(i == 0)
    def _():
        kk_ref[...] = jnp.zeros_like(kk_ref)
        row = lax.broadcasted_iota(jnp.int32, vv_ref.shape[1:], 1)
        for h in range(2):
            vv_ref[h] = jnp.where(row == ones_row(h), 1.0, 0.0).astype(BF16)
            rows = slice(h * HEAD_DIM, (h + 1) * HEAD_DIM)
            for c in range(kk_ref.shape[1] - lead):
                kk_ref[h, lead + c, rows, :] = kt_ref[0, h, :, c * CB_TQ:(c + 1) * CB_TQ].astype(BF16)
                vv_ref[h, lead + c, rows, :] = vt_ref[0, h, :, c * CB_TQ:(c + 1) * CB_TQ].astype(BF16)

    key = lax.broadcasted_iota(jnp.int32, (CB_TQ, CB_TK), 1)
    first = lax.broadcasted_iota(jnp.int32, (CB_TQ, LANES), 1) < HEAD_DIM

    def block_chain(g):
        block = i * CB_GROUPS + g
        rows = slice(g * CB_TQ, (g + 1) * CB_TQ)
        for h in range(2):
            s_ref[g, h] = _dot(q_ref[0, rows, :], window(kk_ref, h, block))
        yield
        es = []
        for h in range(2):
            s = s_ref[g, h] + bias_ref[h]
            if g < lead:
                s = jnp.where(key >= BAND - block * CB_TQ, s, NEG)
            es.append(jnp.exp(s - jnp.max(s, axis=-1, keepdims=True)).astype(BF16))
        yield
        o = None
        for h in range(2):
            o_h = _dot_nt(es[h], window(vv_ref, h, block))
            o_h = o_h / o_h[:, ones_row(h):ones_row(h) + 1]
            o = o_h if h == 0 else jnp.where(first, o, o_h)
        o_ref[0, rows, :] = _head_pair_rms(o, g_ref[...]).astype(o_ref.dtype)

    walks = [block_chain(g) for g in range(CB_GROUPS)]
    while walks:
        walks = [w for w in walks if next(w, "done") != "done"]


def _cb_call(q, kt, vt, bias, g, layer):
    b, s, w = q.shape
    blocks = (BAND + s) // CB_TQ
    step_rows = CB_GROUPS * CB_TQ
    return pl.pallas_call(
        _cb_kernel, grid=(b, w // LANES, s // step_rows),
        in_specs=[pl.BlockSpec((1, step_rows, LANES), lambda bi, pi, i: (bi, i, pi)),
                  _pair_kv_spec(s, layer), _pair_kv_spec(s, layer),
                  pl.BlockSpec((2, CB_TQ, CB_TK), lambda bi, pi, i: (pi, 0, 0)),
                  pl.BlockSpec((None, 1, LANES), lambda bi, pi, i: (layer, 0, pi))],
        out_specs=pl.BlockSpec((1, step_rows, LANES), lambda bi, pi, i: (bi, i, pi)),
        out_shape=jax.ShapeDtypeStruct((b, s, w), BF16),
        scratch_shapes=[pltpu.VMEM((2, blocks, LANES, CB_TQ), BF16), pltpu.VMEM((2, blocks, LANES, CB_TQ), BF16),
                        pltpu.VMEM((CB_GROUPS, 2, CB_TQ, CB_TK), F32)],
        compiler_params=_params("parallel", "parallel", "arbitrary"), name="cb_attention",
    )(q, kt, vt, bias, g)


def _bias_tiles_kernel(table_ref, o_ref, *, heads):
    layer, head = pl.program_id(0), pl.program_id(1)
    n_rel = 2 * REL_MAX + 1
    shape = o_ref.shape[2:]
    dist = (2 * CHUNK + lax.broadcasted_iota(jnp.int32, shape, 0) - lax.broadcasted_iota(jnp.int32, shape, 1))
    idx = jnp.clip(dist, -REL_MAX, REL_MAX) + REL_MAX

    def pick(r, acc):
        return jnp.where(idx == r, table_ref[(layer * n_rel + r) * heads + head], acc)
    lowest = REL_MAX + 2 * CHUNK - (shape[1] - 1)
    o_ref[0, 0] = lax.fori_loop(max(lowest, 0), n_rel, pick, jnp.zeros(shape, F32), unroll=8)


def _bias_tiles(rel_table):
    n_layers, _, heads = rel_table.shape
    return pl.pallas_call(
        functools.partial(_bias_tiles_kernel, heads=heads), grid=(n_layers, heads),
        in_specs=[pl.BlockSpec(memory_space=pltpu.SMEM)],
        out_specs=pl.BlockSpec((1, 1, CHUNK, 3 * CHUNK), lambda l, h: (l, h, 0, 0)),
        out_shape=jax.ShapeDtypeStruct((n_layers, heads, CHUNK, 3 * CHUNK), F32),
        compiler_params=_params("parallel", "parallel"), name="rel_bias_tiles",
    )(rel_table.reshape(-1))


def _prompt_bias(tiles, far):
    heads = tiles.shape[0]
    qc, kc = CB_TQ // CHUNK, CB_TK // CHUNK
    far_tile = jnp.broadcast_to(far[:, None, None], (heads, CHUNK, CHUNK))
    hidden = jnp.full((heads, CHUNK, CHUNK), NEG, F32)
    rows = []
    for c in range(qc):
        row = []
        for k in range(kc):
            m = k - c
            if m < 0 or m > BAND_CHUNKS:
                row.append(hidden)
            elif m < BAND_CHUNKS - 2:
                row.append(far_tile)
            else:
                t = m - (BAND_CHUNKS - 2)
                row.append(tiles[:, :, t * CHUNK:(t + 1) * CHUNK])
        rows.append(jnp.concatenate(row, axis=2))
    return jnp.concatenate(rows, axis=1)


def _head_rows(x, heads):
    t, w = x.shape
    tiled = jnp.concatenate([x] * heads, axis=0)
    row_head = lax.div(lax.broadcasted_iota(jnp.int32, tiled.shape, 0), t)
    col_head = lax.div(lax.broadcasted_iota(jnp.int32, tiled.shape, 1), HEAD_DIM)
    return jnp.where(row_head == col_head, tiled, jnp.zeros_like(tiled))


def _fold_head_rows(o, heads):
    t = o.shape[0] // heads
    col_head = lax.div(lax.broadcasted_iota(jnp.int32, (t, o.shape[1]), 1), HEAD_DIM)
    out = jnp.zeros((t, o.shape[1]), F32)
    for h in range(heads):
        out = jnp.where(col_head == h, o[h * t:(h + 1) * t], out)
    return out


def _all_heads_rms(o, g):
    parts = [_head_pair_rms(o[:, c:c + LANES], g[:, c:c + LANES]) for c in range(0, o.shape[1], LANES)]
    return jnp.concatenate(parts, axis=1)


def _pad_rows(x, rows):
    return jnp.concatenate([x, jnp.zeros((rows - x.shape[0], x.shape[1]), x.dtype)], axis=0)


def _sample_kernel(qa_ref, ka_ref, va_ref, ckt_ref, cvt_ref, qb_ref, kb_ref, vb_ref, ckbt_ref, cvbt_ref,
                   suffix_ref, bias_ref, ga_ref, gb_ref, oa_ref, ob_ref, carry_ref, acc_ref, live_ref):
    t, w = qa_ref.shape[1:]
    heads = w // HEAD_DIM
    tk = SB_TK
    past = ckt_ref.shape[2]

    carry_ref[...] = jnp.zeros_like(carry_ref)
    q = _head_rows(qa_ref[0], heads)
    suffix = suffix_ref[...]
    frame = lax.rem(lax.broadcasted_iota(jnp.int32, (heads * t, tk), 0), t)
    key = lax.broadcasted_iota(jnp.int32, (heads * t, tk), 1)
    wts = _sb_weights(_dot_nt(q, _pad_rows(ka_ref[0], tk).astype(BF16)), suffix, carry_ref, key < frame)
    acc_ref[...] = _dot(wts, _pad_rows(va_ref[0], tk).astype(BF16))
    def cached(blk):
        cols = slice(blk * tk, (blk + 1) * tk)
        wts = _sb_weights(_dot(q, ckt_ref[0, :, cols].astype(BF16)), suffix, carry_ref, None)
        acc_ref[...] += _dot_nt(wts, cvt_ref[0, :, cols].astype(BF16))

    live_ref[0] = jnp.int32(1)
    newest_first = list(reversed(range(past // tk)))
    for first in range(0, len(newest_first), 2):
        @pl.when(live_ref[0] != 0)
        def _(first=first):
            for blk in newest_first[first:first + 2]:
                cached(blk)
            live_ref[0] = (jnp.max(carry_ref[...]) > SB_DEAD).astype(jnp.int32)
    oa_ref[0] = _all_heads_rms(_fold_head_rows(acc_ref[...], heads), ga_ref[...]).astype(oa_ref.dtype)

    qb = _head_rows(qb_ref[0], heads)
    s = jnp.concatenate([_dot(qb, ckbt_ref[0].astype(BF16)),
                         _dot_nt(qb, _pad_rows(kb_ref[0], LANES).astype(BF16))], axis=1)
    e, l = _softmax_unnormalised(s + bias_ref[...])
    e = e.astype(BF16)
    band = ckbt_ref.shape[2]
    o = _dot_nt(e[:, :band], cvbt_ref[0].astype(BF16)) + _dot(e[:, band:], _pad_rows(vb_ref[0], LANES).astype(BF16))
    ob_ref[0] = _all_heads_rms(_fold_head_rows(o / l, heads), gb_ref[...]).astype(ob_ref.dtype)


def _sample_call(qa, ka, va, ckt, cvt, qb, kb, vb, ckbt, cvbt, suffix, bias, ga, gb, layer):
    b, t, w = qa.shape
    heads = w // HEAD_DIM
    new = pl.BlockSpec((1, t, w), lambda bi: (bi, 0, 0))

    def cache(c):
        return pl.BlockSpec((None, 1) + c.shape[2:], lambda bi: (layer, bi, 0, 0))
    return pl.pallas_call(
        _sample_kernel, grid=(b,),
        in_specs=[new, new, new, cache(ckt), cache(cvt), new, new, new, cache(ckbt), cache(cvbt),
                  _resident(suffix.shape), _resident(bias.shape), _resident(ga.shape[1:], layer),
                  _resident(gb.shape[1:], layer)],
        out_specs=[new, new], out_shape=[jax.ShapeDtypeStruct((b, t, w), BF16)] * 2,
        scratch_shapes=[pltpu.VMEM((heads * t, SB_TK), F32), pltpu.VMEM((heads * t, w), F32),
                        pltpu.SMEM((1,), jnp.int32)],
        compiler_params=_params("parallel"), name="sample_attention",
    )(qa, ka, va, ckt, cvt, qb, kb, vb, ckbt, cvbt, suffix, bias, ga, gb)


def _sample_bias(tiles, far, t):
    heads = tiles.shape[0]
    near = BAND - 2 * CHUNK
    parts = [jnp.broadcast_to(far[:, None, None], (heads, t, near)), tiles[:, :t, :2 * CHUNK + t],
             jnp.full((heads, t, LANES - t), NEG, F32)]
    return jnp.concatenate(parts, axis=2).reshape(heads * t, BAND + LANES)


def _per_head_t(cache):
    n_l, n_b, n_p, n_h, n_d = cache.shape
    return jnp.transpose(cache, (0, 1, 3, 4, 2)).reshape(n_l, n_b, n_h * n_d, n_p)


def kernel(x_prompt, x_sample, p_prompt, p_sample, cache_sb_k, cache_sb_v, cache_cb_k, cache_cb_v, g_mix, w_in,
           rel_table, g_out_sb, g_out_cb, w_out, g_ffn, w_gate, w_up, w_down, g_ple, w_ple_gate, w_ple_proj,
           g_final):
    depth = w_in.shape[0]
    b, s, d = x_prompt.shape
    bs, t, _ = x_sample.shape
    w = g_out_sb.shape[1]
    heads = w // HEAD_DIM
    past, band = cache_sb_k.shape[2], cache_cb_k.shape[2]
    assert g_out_cb.shape[1] == w and w_in.shape[2] == 6 * w and w % LANES == 0
    assert s % (SB_CHAINS * SB_TQ) == 0 and s % (CB_GROUPS * CB_TQ) == 0 and s >= BAND and s % ROW_TILE == 0 and s % LAST_ROW_TILE == 0 and s % QKV_TILE == 0
    assert band == BAND and past % SB_TK == 0 and t <= CHUNK and t % 8 == 0

    gain = lambda g: g[:, None, :]
    wts = dict(
        g_mix=gain(g_mix), g_ffn=gain(g_ffn), g_ple=gain(g_ple),
        w_in_t=jnp.swapaxes(w_in, 1, 2).astype(BF16),
        w_out=w_out.astype(BF16), w_gate=w_gate.astype(BF16), w_up=w_up.astype(BF16), w_down=w_down.astype(BF16),
        w_ple_gate=w_ple_gate.astype(BF16), w_ple_proj=w_ple_proj.astype(BF16))
    g_sb, g_cb = gain(g_out_sb), gain(g_out_cb)
    g_fin = g_final.reshape(1, d)
    suffix = _suffix_matrix(SB_TK)
    tiles = _bias_tiles(rel_table)
    far = rel_table[:, 2 * REL_MAX, :]
    caches = [_per_head_t(c) for c in (cache_sb_k, cache_sb_v, cache_cb_k, cache_cb_v)]
    pp = p_prompt.reshape(depth, b * s, -1)
    ps = p_sample.reshape(depth, bs * t, -1)

    xp = x_prompt.reshape(b * s, d)
    xs = x_sample.reshape(bs * t, d)
    qkv_p = _qkv_call(xp, wts, 0, QKV_TILE, s)
    qkv_s = _qkv_call(xs, wts, 0, bs * t, None)
    new_s = []
    for i in range(depth):
        qa, qb = [u.reshape(b, s, w) for u in qkv_p[:2]]
        kat, vat, kbt, vbt = kv_all = qkv_p[2:]
        a = _sb_call(qa, kat, vat, suffix, g_sb, i)
        ob = _cb_call(qb, kbt, vbt, _prompt_bias(tiles[i], far[i]), g_cb, i)

        sqa, sqb, ska, sva, skb, svb = [u.reshape(bs, t, w) for u in qkv_s]
        new_s.append((ska, sva, skb, svb))
        sa, sob = _sample_call(sqa, ska, sva, caches[0], caches[1], sqb, skb, svb, caches[2], caches[3],
                               suffix, _sample_bias(tiles[i], far[i], t), g_sb, g_cb, i)

        out_p = _layer_call(xp, a.reshape(b * s, w), ob.reshape(b * s, w), pp, wts, i, g_fin,
                            LAST_ROW_TILE if i == depth - 1 else ROW_TILE, s, kv_all, band_tail=i + 2 == depth)
        out_s = _layer_call(xs, sa.reshape(bs * t, w), sob.reshape(bs * t, w), ps, wts, i, g_fin, bs * t, None)
        if i == depth - 1:
            xp, xs = out_p, out_s
        else:
            xp, qkv_p, band_tails = out_p[0], out_p[1:7], out_p[7:]
            xs, qkv_s = out_s[0], out_s[1:]
    if depth == 1:
        band_tails = [kbt[..., s - BAND:], vbt[..., s - BAND:]]

    def positions_major(kv_t):
        return jnp.transpose(kv_t, (0, 1, 4, 2, 3))

    def stack(items):
        return jnp.stack(items).reshape(depth, bs, t, heads, HEAD_DIM)

    return (xp.reshape(b, s, d), xs.reshape(bs, t, d),
            positions_major(kat), positions_major(vat),
            positions_major(band_tails[0]), positions_major(band_tails[1]),
            stack([e[0] for e in new_s]), stack([e[1] for e in new_s]),
            stack([e[2] for e in new_s]), stack([e[3] for e in new_s]))
```

```python
import functools

import jax
import jax.numpy as jnp
from jax import lax
from jax.experimental import pallas as pl
from jax.experimental.pallas import tpu as pltpu

F32 = jnp.float32
BF16 = jnp.bfloat16

HEAD_DIM = 64
CHUNK = 64
BAND_CHUNKS = 8
BAND = BAND_CHUNKS * CHUNK
REL_MAX = 128
EPS = 1e-6
Q_SCALE = HEAD_DIM ** -0.5

LANES = 128
NEG = -1e30

ROW_TILE = 256
LAST_ROW_TILE = 512
CHAIN_ROWS = 128
QKV_TILE = 1024
SB_TQ = 256
SB_TK = 128
SB_CHAINS = 8
SB_DEAD = -120.0
CB_TQ = 256
CB_TK = CB_TQ + BAND
CB_GROUPS = 8
VMEM_LIMIT = 58 * 1024 * 1024


def _params(*semantics):
    return pltpu.CompilerParams(dimension_semantics=semantics, vmem_limit_bytes=VMEM_LIMIT)


def _resident(shape, layer=None):
    zeros = (0,) * len(shape)
    if layer is None:
        return pl.BlockSpec(shape, lambda *_: zeros, pipeline_mode=pl.Buffered(1))
    return pl.BlockSpec((None,) + tuple(shape), lambda *_: (layer,) + zeros, pipeline_mode=pl.Buffered(1))


def _rms_unit(x):
    return x * lax.rsqrt(jnp.mean(x * x, axis=-1, keepdims=True) + EPS)


def _dot(a, b):
    return jnp.dot(a, b, preferred_element_type=F32)


def _dot_nt(a, b):
    return lax.dot_general(a, b, (((1,), (1,)), ((), ())), preferred_element_type=F32)


def _softplus(z):
    return jnp.maximum(z, 0.0) + jnp.log(1.0 + jnp.exp(-jnp.abs(z)))


def _head_pair_rms(o, g):
    first = lax.broadcasted_iota(jnp.int32, o.shape, 1) < HEAD_DIM
    o2 = o * o
    s0 = jnp.sum(jnp.where(first, o2, 0.0), axis=-1, keepdims=True)
    s1 = jnp.sum(jnp.where(first, 0.0, o2), axis=-1, keepdims=True)
    ms = jnp.where(first, s0, s1) * (1.0 / HEAD_DIM)
    return o * lax.rsqrt(ms + EPS) * g


def _store_qkv(h, wt_ref, outs, per_head_t, prev=()):
    qa_ref, qb_ref = outs[:2]
    w = qa_ref.shape[-1]
    qa_ref[...] = (_dot_nt(h, wt_ref[0:w, :]) * Q_SCALE).astype(BF16)
    qb_ref[...] = (_dot_nt(h, wt_ref[3 * w:4 * w, :]) * Q_SCALE).astype(BF16)
    band_tails = outs[6:]
    for c, ref in enumerate(outs[2:6]):
        first_row = (1 + c + c // 2) * w
        w_t = wt_ref[first_row:first_row + w, :]
        if per_head_t:
            n_prev = ref.shape[0] - 1
            new = _dot_nt(w_t, h).reshape(ref.shape[2:])
            for target in [ref] + ([band_tails[c - 2]] if band_tails and c >= 2 else []):
                if n_prev:
                    target[:n_prev] = prev[c][...]
                target[n_prev, 0] = new
        else:
            ref[...] = _dot_nt(h, w_t)


def _qkv_kernel(x_ref, g_ref, wt_ref, *outs, per_head_t):
    h = (_rms_unit(x_ref[...]) * g_ref[...]).astype(BF16)
    _store_qkv(h, wt_ref, outs, per_head_t)


def _layer_tail(x_ref, a_ref, ob_ref, p_ref, wo_ref, gf_ref, wg_ref, wu_ref, wd_ref, gp_ref, wpg_ref, wpp_ref):
    w = a_ref.shape[-1]
    out = {}

    def rows_chain(c, rows):
        x = x_ref[rows, :] + _dot(a_ref[rows, :], wo_ref[:w, :]) + _dot(ob_ref[rows, :], wo_ref[w:, :])
        yield
        h = (_rms_unit(x) * gf_ref[...]).astype(BF16)
        yield
        gate = _dot(h, wg_ref[...])
        up = _dot(h, wu_ref[...])
        yield
        act = (gate * jax.nn.sigmoid(gate) * up).astype(BF16)
        x = x + _dot(act, wd_ref[...])
        yield
        hp = (_rms_unit(x) * gp_ref[...]).astype(BF16)
        yield
        ple_gate = jax.nn.sigmoid(_dot(hp, wpg_ref[...]))
        out[c] = x + ple_gate * _dot(p_ref[rows, :].astype(BF16), wpp_ref[...])

    n = x_ref.shape[0]
    chains = max(n // CHAIN_ROWS, 1)
    step = n // chains
    walks = [rows_chain(c, slice(c * step, (c + 1) * step)) for c in range(chains)]
    while walks:
        walks = [wk for wk in walks if next(wk, "done") != "done"]
    return jnp.concatenate([out[c] for c in range(chains)], axis=0)


N_TAIL = 12


def _mid_layer_kernel(*refs, per_head_t, n_prev):
    tail, (gn_ref, wt_ref) = refs[:N_TAIL], refs[N_TAIL:N_TAIL + 2]
    prev, xo_ref, outs = refs[N_TAIL + 2:N_TAIL + 2 + n_prev], refs[N_TAIL + 2 + n_prev], refs[N_TAIL + 3 + n_prev:]
    x = _layer_tail(*tail)
    xo_ref[...] = x
    _store_qkv((_rms_unit(x) * gn_ref[...]).astype(BF16), wt_ref, outs, per_head_t, prev)


def _last_layer_kernel(*refs):
    tail, (gn_ref, y_ref) = refs[:N_TAIL], refs[N_TAIL:]
    y_ref[...] = _rms_unit(_layer_tail(*tail)) * gn_ref[...]


def _row_spec(tm, width):
    return pl.BlockSpec((tm, width), lambda i: (i, 0))


def _kv_spec(layers, w, tm, seq):
    heads, tiles = w // HEAD_DIM, seq // tm
    return pl.BlockSpec((layers, 1, heads, HEAD_DIM, tm), lambda i: (0, i // tiles, 0, 0, i % tiles))


def _qkv_out(rows, w, tm, seq, layers):
    q_shapes = [jax.ShapeDtypeStruct((rows, w), BF16)] * 2
    q_specs = [_row_spec(tm, w)] * 2
    if seq is None:
        return q_shapes + [jax.ShapeDtypeStruct((rows, w), F32)] * 4, q_specs + [_row_spec(tm, w)] * 4
    kv_shape = jax.ShapeDtypeStruct((layers, rows // seq, w // HEAD_DIM, HEAD_DIM, seq), F32)
    return q_shapes + [kv_shape] * 4, q_specs + [_kv_spec(layers, w, tm, seq)] * 4


def _qkv_call(x, wts, layer, tm, seq):
    rows, d = x.shape
    wt, g = wts["w_in_t"], wts["g_mix"]
    out_shape, out_specs = _qkv_out(rows, wt.shape[1] // 6, tm, seq, layer + 1)
    return pl.pallas_call(
        functools.partial(_qkv_kernel, per_head_t=seq is not None), grid=(rows // tm,),
        in_specs=[_row_spec(tm, d), _resident(g.shape[1:], layer), _resident(wt.shape[1:], layer)],
        out_specs=out_specs, out_shape=out_shape,
        compiler_params=_params("parallel"), name="qkv_proj",
    )(x, g, wt)


TAIL_WEIGHTS = ("w_out", "g_ffn", "w_gate", "w_up", "w_down", "g_ple", "w_ple_gate", "w_ple_proj")


def _layer_call(x, a, ob, p, wts, layer, g_final, tm, seq, kv_prev=(), band_tail=False):
    rows, d = x.shape
    last = layer == wts["w_out"].shape[0] - 1
    tail_in = [x, a, ob, p] + [wts[k] for k in TAIL_WEIGHTS]
    tail_specs = ([_row_spec(tm, d), _row_spec(tm, a.shape[1]), _row_spec(tm, ob.shape[1]),
                   pl.BlockSpec((None, tm, p.shape[2]), lambda i: (layer, i, 0))]
                  + [_resident(wts[k].shape[1:], layer) for k in TAIL_WEIGHTS])
    if last:
        return pl.pallas_call(
            _last_layer_kernel, grid=(rows // tm,), in_specs=tail_specs + [_resident(g_final.shape)],
            out_specs=_row_spec(tm, d), out_shape=jax.ShapeDtypeStruct((rows, d), F32),
            compiler_params=_params("parallel"), name="layer_tail_final",
        )(*tail_in, g_final)
    nxt = [wts["g_mix"], wts["w_in_t"]]
    w = wts["w_in_t"].shape[1] // 6
    out_shape, out_specs = _qkv_out(rows, w, tm, seq, layer + 2)
    if band_tail:
        tiles, first_tail = seq // tm, (seq - BAND) // tm
        out_shape += [jax.ShapeDtypeStruct(out_shape[-1].shape[:-1] + (BAND,), F32)] * 2
        out_specs += [pl.BlockSpec(out_specs[-1].block_shape,
                                   lambda i: (0, i // tiles, 0, 0, jnp.maximum(i % tiles - first_tail, 0)))] * 2
    return pl.pallas_call(
        functools.partial(_mid_layer_kernel, per_head_t=seq is not None, n_prev=len(kv_prev)), grid=(rows // tm,),
        in_specs=(tail_specs + [_resident(t.shape[1:], layer + 1) for t in nxt]
                  + [_kv_spec(layer + 1, w, tm, seq) for _ in kv_prev]),
        out_specs=[_row_spec(tm, d)] + out_specs,
        out_shape=[jax.ShapeDtypeStruct((rows, d), F32)] + out_shape,
        compiler_params=_params("arbitrary" if band_tail else "parallel"), name="layer_tail_qkv",
    )(*tail_in, *nxt, *kv_prev)


def _suffix_matrix(tk):
    j = lax.broadcasted_iota(jnp.int32, (tk, 2 * tk), 0)
    c = lax.broadcasted_iota(jnp.int32, (tk, 2 * tk), 1)
    return jnp.where((c >= tk) | (j > c), -1.0, 0.0).astype(BF16)


def _sb_weights(z, suffix, carry_ref, causal, row0=0):
    tk = suffix.shape[0]
    sp = _softplus(z)
    w_parts = []
    for h in range(z.shape[1] // tk):
        cols = slice(h * tk, (h + 1) * tk)
        z_h, sp_h = z[:, cols], sp[:, cols]
        drop = sp_h if causal is None else jnp.where(causal, sp_h, 0.0)
        sums = _dot(drop.astype(BF16), suffix)
        carry = carry_ref[row0:, cols]
        log_w = (z_h - sp_h) + (sums[:, :tk] + carry)
        w = jnp.exp(log_w)
        if causal is not None:
            w = jnp.where(causal, w, 0.0)
        carry_ref[row0:, cols] = carry + sums[:, tk:]
        w_parts.append(w.astype(BF16))
    return jnp.concatenate(w_parts, axis=1)


def _sb_kernel(q_ref, kt_ref, vt_ref, suffix_ref, g_ref, o_ref, kk_ref, vv_ref, carry_ref, acc_ref, z_ref, w_ref,
               more_ref):
    tq, tk = SB_TQ, SB_TK
    per_q = tq // tk
    i = pl.program_id(2)

    @pl.when(i == 0)
    def _():
        kk_ref[...] = jnp.zeros_like(kk_ref)
        vv_ref[...] = jnp.zeros_like(vv_ref)
        for j in range(kk_ref.shape[0]):
            for h in range(2):
                rows, cols = slice(h * HEAD_DIM, (h + 1) * HEAD_DIM), slice(h * tk, (h + 1) * tk)
                kk_ref[j, rows, cols] = kt_ref[0, h, :, j * tk:(j + 1) * tk].astype(BF16)
                vv_ref[j, rows, cols] = vt_ref[0, h, :, j * tk:(j + 1) * tk].astype(BF16)

    carry_ref[...] = jnp.zeros_like(carry_ref)
    acc_ref[...] = jnp.zeros_like(acc_ref)
    suffix = suffix_ref[...]
    k_off = lax.broadcasted_iota(jnp.int32, (tq, tk), 1)
    row = lax.broadcasted_iota(jnp.int32, (tq, tk), 0)
    last_slot = (per_q - 1) % 2

    def row0(n):
        return (per_q - 1 - n) * tk if isinstance(n, int) and 0 <= n < per_q else 0

    class Chain:
        def __init__(self, c):
            self.c = c
            self.qi = i * SB_CHAINS + c
            self.rows = slice(c * tq, (c + 1) * tq)
            self.q = q_ref[0, self.rows, :]
            self.newest = (self.qi + 1) * per_q - 1
            self.carry, self.acc, self.z, self.w = carry_ref.at[c], acc_ref.at[c], z_ref.at[c], w_ref.at[c]

        def step(self, n, slot, diagonal, with_next=True):
            j = self.newest - n
            r_prev, r_cur, r_next = row0(n - 1) if diagonal else 0, row0(n), row0(n + 1)
            z = self.z[slot, r_cur:, :]
            if with_next:
                self.z[1 - slot, r_next:, :] = _dot(self.q[r_next:], kk_ref[jnp.maximum(j - 1, 0)])
            if not (diagonal and n == 0):
                self.acc[r_prev:, :] += _dot_nt(self.w[1 - slot, r_prev:, :], vv_ref[j + 1])
            causal = ((j * tk + k_off) < self.qi * tq + row)[r_cur:] if diagonal else None
            self.w[slot, r_cur:, :] = _sb_weights(z, suffix, self.carry, causal, r_cur)

        def emit(self, oldest):
            o = self.acc[...] + _dot_nt(self.w[last_slot], vv_ref[oldest])
            o_ref[0, self.rows, :] = _head_pair_rms(o, g_ref[...]).astype(o_ref.dtype)

        def first_steps(self, n_steps):
            self.z[0, row0(0):, :] = _dot(self.q[row0(0):], kk_ref[self.newest])
            yield
            for n in range(n_steps):
                self.step(n, n % 2, n < per_q, with_next=n + 1 < n_steps)
                yield
            self.emit(self.newest - (n_steps - 1))
            more = jnp.logical_and(jnp.max(self.carry[...]) > SB_DEAD, self.newest >= n_steps)
            more_ref[self.c] = more.astype(jnp.int32)

        def older_steps(self):
            @pl.when(more_ref[self.c] != 0)
            def _():
                self.z[0] = _dot(self.q, kk_ref[self.newest - 2 * per_q])

                def more(state):
                    t, live = state
                    return jnp.logical_and(t < self.qi, live)

                def trip(state):
                    t, _ = state
                    for d in range(per_q):
                        self.step(per_q * (t + 1) + d, d % 2, False)
                    return t + 1, jnp.max(self.carry[...]) > SB_DEAD
                trips, _ = lax.while_loop(more, trip, (jnp.int32(1), self.qi > 0))
                self.emit(per_q * (self.qi - trips))

    chains = [Chain(c) for c in range(SB_CHAINS)]

    def interleave(walks):
        while walks:
            walks = [w for w in walks if next(w, "done") != "done"]

    @pl.when(i == 0)
    def _():
        interleave([chains[0].first_steps(per_q)] + [ch.first_steps(2 * per_q) for ch in chains[1:]])

    @pl.when(i > 0)
    def _():
        interleave([ch.first_steps(2 * per_q) for ch in chains])

    for ch in chains:
        ch.older_steps()


def _pair_kv_spec(s, layer):
    return pl.BlockSpec((None, 1, 2, HEAD_DIM, s), lambda bi, pi, i: (layer, bi, pi, 0, 0))


def _sb_call(q, kt, vt, suffix, g, layer):
    b, s, w = q.shape
    nkb = s // SB_TK
    step_rows = SB_CHAINS * SB_TQ
    assert (SB_TQ // SB_TK) % 2 == 0 and SB_CHAINS >= 2
    return pl.pallas_call(
        _sb_kernel, grid=(b, w // LANES, s // step_rows),
        in_specs=[pl.BlockSpec((1, step_rows, LANES), lambda bi, pi, i: (bi, i, pi)),
                  _pair_kv_spec(s, layer), _pair_kv_spec(s, layer),
                  _resident(suffix.shape), pl.BlockSpec((None, 1, LANES), lambda bi, pi, i: (layer, 0, pi))],
        out_specs=pl.BlockSpec((1, step_rows, LANES), lambda bi, pi, i: (bi, i, pi)),
        out_shape=jax.ShapeDtypeStruct((b, s, w), BF16),
        scratch_shapes=[pltpu.VMEM((nkb, LANES, 2 * SB_TK), BF16), pltpu.VMEM((nkb, LANES, 2 * SB_TK), BF16),
                        pltpu.VMEM((SB_CHAINS, SB_TQ, 2 * SB_TK), F32), pltpu.VMEM((SB_CHAINS, SB_TQ, LANES), F32),
                        pltpu.VMEM((SB_CHAINS, 2, SB_TQ, 2 * SB_TK), F32),
                        pltpu.VMEM((SB_CHAINS, 2, SB_TQ, 2 * SB_TK), BF16), pltpu.SMEM((SB_CHAINS,), jnp.int32)],
        compiler_params=_params("parallel", "parallel", "arbitrary"), name="sb_attention",
    )(q, kt, vt, suffix, g)


def _softmax_unnormalised(s):
    e = jnp.exp(s - jnp.max(s, axis=-1, keepdims=True))
    return e, jnp.sum(e, axis=-1, keepdims=True)


def _cb_kernel(q_ref, kt_ref, vt_ref, bias_ref, g_ref, o_ref, kk_ref, vv_ref, s_ref):
    i = pl.program_id(2)
    lead = BAND // CB_TQ
    n_window = CB_TK // CB_TQ

    def ones_row(h):
        return (1 - h) * HEAD_DIM

    def window(ref, h, block):
        return jnp.concatenate([ref[h, block + c] for c in range(n_window)], axis=1)

    @pl.when(i == 0)
    def _():
        kk_ref[...] = jnp.zeros_like(kk_ref)
        row = lax.broadcasted_iota(jnp.int32, vv_ref.shape[1:], 1)
        for h in range(2):
            vv_ref[h] = jnp.where(row == ones_row(h), 1.0, 0.0).astype(BF16)
            rows = slice(h * HEAD_DIM, (h + 1) * HEAD_DIM)
            for c in range(kk_ref.shape[1] - lead):
                kk_ref[h, lead + c, rows, :] = kt_ref[0, h, :, c * CB_TQ:(c + 1) * CB_TQ].astype(BF16)
                vv_ref[h, lead + c, rows, :] = vt_ref[0, h, :, c * CB_TQ:(c + 1) * CB_TQ].astype(BF16)

    key = lax.broadcasted_iota(jnp.int32, (CB_TQ, CB_TK), 1)
    first = lax.broadcasted_iota(jnp.int32, (CB_TQ, LANES), 1) < HEAD_DIM

    def block_chain(g):
        block = i * CB_GROUPS + g
        rows = slice(g * CB_TQ, (g + 1) * CB_TQ)
        for h in range(2):
            s_ref[g, h] = _dot(q_ref[0, rows, :], window(kk_ref, h, block))
        yield
        es = []
        for h in range(2):
            s = s_ref[g, h] + bias_ref[h]
            if g < lead:
                s = jnp.where(key >= BAND - block * CB_TQ, s, NEG)
            es.append(jnp.exp(s - jnp.max(s, axis=-1, keepdims=True)).astype(BF16))
        yield
        o = None
        for h in range(2):
            o_h = _dot_nt(es[h], window(vv_ref, h, block))
            o_h = o_h / o_h[:, ones_row(h):ones_row(h) + 1]
            o = o_h if h == 0 else jnp.where(first, o, o_h)
        o_ref[0, rows, :] = _head_pair_rms(o, g_ref[...]).astype(o_ref.dtype)

    walks = [block_chain(g) for g in range(CB_GROUPS)]
    while walks:
        walks = [w for w in walks if next(w, "done") != "done"]


def _cb_call(q, kt, vt, bias, g, layer):
    b, s, w = q.shape
    blocks = (BAND + s) // CB_TQ
    step_rows = CB_GROUPS * CB_TQ
    return pl.pallas_call(
        _cb_kernel, grid=(b, w // LANES, s // step_rows),
        in_specs=[pl.BlockSpec((1, step_rows, LANES), lambda bi, pi, i: (bi, i, pi)),
                  _pair_kv_spec(s, layer), _pair_kv_spec(s, layer),
                  pl.BlockSpec((2, CB_TQ, CB_TK), lambda bi, pi, i: (pi, 0, 0)),
                  pl.BlockSpec((None, 1, LANES), lambda bi, pi, i: (layer, 0, pi))],
        out_specs=pl.BlockSpec((1, step_rows, LANES), lambda bi, pi, i: (bi, i, pi)),
        out_shape=jax.ShapeDtypeStruct((b, s, w), BF16),
        scratch_shapes=[pltpu.VMEM((2, blocks, LANES, CB_TQ), BF16), pltpu.VMEM((2, blocks, LANES, CB_TQ), BF16),
                        pltpu.VMEM((CB_GROUPS, 2, CB_TQ, CB_TK), F32)],
        compiler_params=_params("parallel", "parallel", "arbitrary"), name="cb_attention",
    )(q, kt, vt, bias, g)


def _bias_tiles_kernel(table_ref, o_ref, *, heads):
    layer, head = pl.program_id(0), pl.program_id(1)
    n_rel = 2 * REL_MAX + 1
    shape = o_ref.shape[2:]
    dist = (2 * CHUNK + lax.broadcasted_iota(jnp.int32, shape, 0) - lax.broadcasted_iota(jnp.int32, shape, 1))
    idx = jnp.clip(dist, -REL_MAX, REL_MAX) + REL_MAX

    def pick(r, acc):
        return jnp.where(idx == r, table_ref[(layer * n_rel + r) * heads + head], acc)
    lowest = REL_MAX + 2 * CHUNK - (shape[1] - 1)
    o_ref[0, 0] = lax.fori_loop(max(lowest, 0), n_rel, pick, jnp.zeros(shape, F32), unroll=8)


def _bias_tiles(rel_table):
    n_layers, _, heads = rel_table.shape
    return pl.pallas_call(
        functools.partial(_bias_tiles_kernel, heads=heads), grid=(n_layers, heads),
        in_specs=[pl.BlockSpec(memory_space=pltpu.SMEM)],
        out_specs=pl.BlockSpec((1, 1, CHUNK, 3 * CHUNK), lambda l, h: (l, h, 0, 0)),
        out_shape=jax.ShapeDtypeStruct((n_layers, heads, CHUNK, 3 * CHUNK), F32),
        compiler_params=_params("parallel", "parallel"), name="rel_bias_tiles",
    )(rel_table.reshape(-1))


def _prompt_bias(tiles, far):
    heads = tiles.shape[0]
    qc, kc = CB_TQ // CHUNK, CB_TK // CHUNK
    far_tile = jnp.broadcast_to(far[:, None, None], (heads, CHUNK, CHUNK))
    hidden = jnp.full((heads, CHUNK, CHUNK), NEG, F32)
    rows = []
    for c in range(qc):
        row = []
        for k in range(kc):
            m = k - c
            if m < 0 or m > BAND_CHUNKS:
                row.append(hidden)
            elif m < BAND_CHUNKS - 2:
                row.append(far_tile)
            else:
                t = m - (BAND_CHUNKS - 2)
                row.append(tiles[:, :, t * CHUNK:(t + 1) * CHUNK])
        rows.append(jnp.concatenate(row, axis=2))
    return jnp.concatenate(rows, axis=1)


def _head_rows(x, heads):
    t, w = x.shape
    tiled = jnp.concatenate([x] * heads, axis=0)
    row_head = lax.div(lax.broadcasted_iota(jnp.int32, tiled.shape, 0), t)
    col_head = lax.div(lax.broadcasted_iota(jnp.int32, tiled.shape, 1), HEAD_DIM)
    return jnp.where(row_head == col_head, tiled, jnp.zeros_like(tiled))


def _fold_head_rows(o, heads):
    t = o.shape[0] // heads
    col_head = lax.div(lax.broadcasted_iota(jnp.int32, (t, o.shape[1]), 1), HEAD_DIM)
    out = jnp.zeros((t, o.shape[1]), F32)
    for h in range(heads):
        out = jnp.where(col_head == h, o[h * t:(h + 1) * t], out)
    return out


def _all_heads_rms(o, g):
    parts = [_head_pair_rms(o[:, c:c + LANES], g[:, c:c + LANES]) for c in range(0, o.shape[1], LANES)]
    return jnp.concatenate(parts, axis=1)


def _pad_rows(x, rows):
    return jnp.concatenate([x, jnp.zeros((rows - x.shape[0], x.shape[1]), x.dtype)], axis=0)


def _sample_kernel(qa_ref, ka_ref, va_ref, ckt_ref, cvt_ref, qb_ref, kb_ref, vb_ref, ckbt_ref, cvbt_ref,
                   suffix_ref, bias_ref, ga_ref, gb_ref, oa_ref, ob_ref, carry_ref, acc_ref, live_ref):
    t, w = qa_ref.shape[1:]
    heads = w // HEAD_DIM
    tk = SB_TK
    past = ckt_ref.shape[2]

    carry_ref[...] = jnp.zeros_like(carry_ref)
    q = _head_rows(qa_ref[0], heads)
    suffix = suffix_ref[...]
    frame = lax.rem(lax.broadcasted_iota(jnp.int32, (heads * t, tk), 0), t)
    key = lax.broadcasted_iota(jnp.int32, (heads * t, tk), 1)
    wts = _sb_weights(_dot_nt(q, _pad_rows(ka_ref[0], tk).astype(BF16)), suffix, carry_ref, key < frame)
    acc_ref[...] = _dot(wts, _pad_rows(va_ref[0], tk).astype(BF16))
    def cached(blk):
        cols = slice(blk * tk, (blk + 1) * tk)
        wts = _sb_weights(_dot(q, ckt_ref[0, :, cols].astype(BF16)), suffix, carry_ref, None)
        acc_ref[...] += _dot_nt(wts, cvt_ref[0, :, cols].astype(BF16))

    live_ref[0] = jnp.int32(1)
    newest_first = list(reversed(range(past // tk)))
    for first in range(0, len(newest_first), 2):
        @pl.when(live_ref[0] != 0)
        def _(first=first):
            for blk in newest_first[first:first + 2]:
                cached(blk)
            live_ref[0] = (jnp.max(carry_ref[...]) > SB_DEAD).astype(jnp.int32)
    oa_ref[0] = _all_heads_rms(_fold_head_rows(acc_ref[...], heads), ga_ref[...]).astype(oa_ref.dtype)

    qb = _head_rows(qb_ref[0], heads)
    s = jnp.concatenate([_dot(qb, ckbt_ref[0].astype(BF16)),
                         _dot_nt(qb, _pad_rows(kb_ref[0], LANES).astype(BF16))], axis=1)
    e, l = _softmax_unnormalised(s + bias_ref[...])
    e = e.astype(BF16)
    band = ckbt_ref.shape[2]
    o = _dot_nt(e[:, :band], cvbt_ref[0].astype(BF16)) + _dot(e[:, band:], _pad_rows(vb_ref[0], LANES).astype(BF16))
    ob_ref[0] = _all_heads_rms(_fold_head_rows(o / l, heads), gb_ref[...]).astype(ob_ref.dtype)


def _sample_call(qa, ka, va, ckt, cvt, qb, kb, vb, ckbt, cvbt, suffix, bias, ga, gb, layer):
    b, t, w = qa.shape
    heads = w // HEAD_DIM
    new = pl.BlockSpec((1, t, w), lambda bi: (bi, 0, 0))

    def cache(c):
        return pl.BlockSpec((None, 1) + c.shape[2:], lambda bi: (layer, bi, 0, 0))
    return pl.pallas_call(
        _sample_kernel, grid=(b,),
        in_specs=[new, new, new, cache(ckt), cache(cvt), new, new, new, cache(ckbt), cache(cvbt),
                  _resident(suffix.shape), _resident(bias.shape), _resident(ga.shape[1:], layer),
                  _resident(gb.shape[1:], layer)],
        out_specs=[new, new], out_shape=[jax.ShapeDtypeStruct((b, t, w), BF16)] * 2,
        scratch_shapes=[pltpu.VMEM((heads * t, SB_TK), F32), pltpu.VMEM((heads * t, w), F32),
                        pltpu.SMEM((1,), jnp.int32)],
        compiler_params=_params("parallel"), name="sample_attention",
    )(qa, ka, va, ckt, cvt, qb, kb, vb, ckbt, cvbt, suffix, bias, ga, gb)


def _sample_bias(tiles, far, t):
    heads = tiles.shape[0]
    near = BAND - 2 * CHUNK
    parts = [jnp.broadcast_to(far[:, None, None], (heads, t, near)), tiles[:, :t, :2 * CHUNK + t],
             jnp.full((heads, t, LANES - t), NEG, F32)]
    return jnp.concatenate(parts, axis=2).reshape(heads * t, BAND + LANES)


def _per_head_t(cache):
    n_l, n_b, n_p, n_h, n_d = cache.shape
    return jnp.transpose(cache, (0, 1, 3, 4, 2)).reshape(n_l, n_b, n_h * n_d, n_p)


def kernel(x_prompt, x_sample, p_prompt, p_sample, cache_sb_k, cache_sb_v, cache_cb_k, cache_cb_v, g_mix, w_in,
           rel_table, g_out_sb, g_out_cb, w_out, g_ffn, w_gate, w_up, w_down, g_ple, w_ple_gate, w_ple_proj,
           g_final):
    depth = w_in.shape[0]
    b, s, d = x_prompt.shape
    bs, t, _ = x_sample.shape
    w = g_out_sb.shape[1]
    heads = w // HEAD_DIM
    past, band = cache_sb_k.shape[2], cache_cb_k.shape[2]
    assert g_out_cb.shape[1] == w and w_in.shape[2] == 6 * w and w % LANES == 0
    assert s % (SB_CHAINS * SB_TQ) == 0 and s % (CB_GROUPS * CB_TQ) == 0 and s >= BAND and s % ROW_TILE == 0 and s % LAST_ROW_TILE == 0 and s % QKV_TILE == 0
    assert band == BAND and past % SB_TK == 0 and t <= CHUNK and t % 8 == 0

    gain = lambda g: g[:, None, :]
    wts = dict(
        g_mix=gain(g_mix), g_ffn=gain(g_ffn), g_ple=gain(g_ple),
        w_in_t=jnp.swapaxes(w_in, 1, 2).astype(BF16),
        w_out=w_out.astype(BF16), w_gate=w_gate.astype(BF16), w_up=w_up.astype(BF16), w_down=w_down.astype(BF16),
        w_ple_gate=w_ple_gate.astype(BF16), w_ple_proj=w_ple_proj.astype(BF16))
    g_sb, g_cb = gain(g_out_sb), gain(g_out_cb)
    g_fin = g_final.reshape(1, d)
    suffix = _suffix_matrix(SB_TK)
    tiles = _bias_tiles(rel_table)
    far = rel_table[:, 2 * REL_MAX, :]
    caches = [_per_head_t(c) for c in (cache_sb_k, cache_sb_v, cache_cb_k, cache_cb_v)]
    pp = p_prompt.reshape(depth, b * s, -1)
    ps = p_sample.reshape(depth, bs * t, -1)

    xp = x_prompt.reshape(b * s, d)
    xs = x_sample.reshape(bs * t, d)
    qkv_p = _qkv_call(xp, wts, 0, QKV_TILE, s)
    qkv_s = _qkv_call(xs, wts, 0, bs * t, None)
    new_s = []
    for i in range(depth):
        qa, qb = [u.reshape(b, s, w) for u in qkv_p[:2]]
        kat, vat, kbt, vbt = kv_all = qkv_p[2:]
        a = _sb_call(qa, kat, vat, suffix, g_sb, i)
        ob = _cb_call(qb, kbt, vbt, _prompt_bias(tiles[i], far[i]), g_cb, i)

        sqa, sqb, ska, sva, skb, svb = [u.reshape(bs, t, w) for u in qkv_s]
        new_s.append((ska, sva, skb, svb))
        sa, sob = _sample_call(sqa, ska, sva, caches[0], caches[1], sqb, skb, svb, caches[2], caches[3],
                               suffix, _sample_bias(tiles[i], far[i], t), g_sb, g_cb, i)

        out_p = _layer_call(xp, a.reshape(b * s, w), ob.reshape(b * s, w), pp, wts, i, g_fin,
                            LAST_ROW_TILE if i == depth - 1 else ROW_TILE, s, kv_all, band_tail=i + 2 == depth)
        out_s = _layer_call(xs, sa.reshape(bs * t, w), sob.reshape(bs * t, w), ps, wts, i, g_fin, bs * t, None)
        if i == depth - 1:
            xp, xs = out_p, out_s
        else:
            xp, qkv_p, band_tails = out_p[0], out_p[1:7], out_p[7:]
            xs, qkv_s = out_s[0], out_s[1:]
    if depth == 1:
        band_tails = [kbt[..., s - BAND:], vbt[..., s - BAND:]]

    def positions_major(kv_t):
        return jnp.transpose(kv_t, (0, 1, 4, 2, 3))

    def stack(items):
        return jnp.stack(items).reshape(depth, bs, t, heads, HEAD_DIM)

    return (xp.reshape(b, s, d), xs.reshape(bs, t, d),
            positions_major(kat), positions_major(vat),
            positions_major(band_tails[0]), positions_major(band_tails[1]),
            stack([e[0] for e in new_s]), stack([e[1] for e in new_s]),
            stack([e[2] for e in new_s]), stack([e[3] for e in new_s]))
```
